```python
import math
import jax
import jax.numpy as jnp
from jax import lax
import numpy as np

D_MODEL = 1024
BATCH = 8
SEQ = 4096
DEPTH = 4

GRID_W = 64
CTX_LEN = 256
N_MIXERS = 4
EPS = 1e-6
ROPE_BASE = 10000.0
Q_BLOCK = 128
NEG_INF = -1e30

NA_HEAD_DIM = 64
NA_HEADS = D_MODEL // NA_HEAD_DIM
NA_KH = 8
NA_KW = 16

SW_HEAD_DIM = 64
SW_HEADS = D_MODEL // SW_HEAD_DIM
SW_KV_HEADS = 4
SW_WINDOW = 128

MLA_HEADS = 16
MLA_Q_RANK = 384
MLA_KV_RANK = 256
MLA_NOPE = 64
MLA_ROPE = 32
MLA_V = 64

DIFF_HEAD_DIM = 64
DIFF_HEADS = D_MODEL // (2 * DIFF_HEAD_DIM)

N_GROUPS = 4
EXPERTS_PER_GROUP = 8
N_EXPERTS = N_GROUPS * EXPERTS_PER_GROUP
TOP_K = 2
D_EXPERT = 256
MOE_BLOCK = 128

kernel_name = 'hybrid_dit_na_swa_mla_diff_hmoe'


def _layers_of_kind(kind):
    return len(range(kind, DEPTH, N_MIXERS))


def rms_norm(x, g):
    xf = x.astype(jnp.float32)
    xf = xf * lax.rsqrt(jnp.mean(xf * xf, axis=-1, keepdims=True) + EPS)
    return (xf * g.astype(jnp.float32)).astype(x.dtype)


def modulate(x, g, shift, scale):
    return rms_norm(x, g) * (1.0 + scale) + shift


def split_heads(t, n_heads):
    b, n, _ = t.shape
    return t.reshape(b, n, n_heads, -1).transpose(0, 2, 1, 3)


def merge_heads(t):
    b, h, n, d = t.shape
    return t.transpose(0, 2, 1, 3).reshape(b, n, h * d)


def axial_rope(n_tokens, rot_dim):
    t = jnp.arange(n_tokens, dtype=jnp.int32)
    n_freq = rot_dim // 4
    inv_freq = ROPE_BASE ** (-jnp.arange(n_freq, dtype=jnp.float32) / n_freq)
    rows = (t // GRID_W).astype(jnp.float32)
    cols = (t % GRID_W).astype(jnp.float32)
    ang = jnp.concatenate([rows[:, None] * inv_freq, cols[:, None] * inv_freq], axis=-1)
    return jnp.cos(ang), jnp.sin(ang)


def apply_rope(x, cos, sin):
    x1, x2 = jnp.split(x.astype(jnp.float32), 2, axis=-1)
    return jnp.concatenate([x1 * cos - x2 * sin, x2 * cos + x1 * sin], axis=-1).astype(x.dtype)


def softmax_f32(logits, sink=None):
    if sink is None:
        return jax.nn.softmax(logits, axis=-1)
    sink_col = jnp.broadcast_to(sink, logits.shape[:-1] + (1,))
    return jax.nn.softmax(jnp.concatenate([logits, sink_col], axis=-1), axis=-1)[..., :-1]


def context_attention(q, k, v, scale, sink=None):
    b, h, n, dq = q.shape
    hk = k.shape[1]
    g = h // hk
    qg = q.reshape(b, hk, g, n, dq)
    logits = jnp.einsum('bkgqd,bkcd->bkgqc', qg, k).astype(jnp.float32) * scale
    sink_g = None if sink is None else sink.reshape(hk, g)[None, :, :, None, None].astype(jnp.float32)
    p = softmax_f32(logits, sink_g).astype(v.dtype)
    return jnp.einsum('bkgqc,bkcd->bkgqd', p, v).reshape(b, h, n, -1)


def dense_latent_attention(q, k, v, k_ctx, v_ctx, scale):
    b, h, s, dq = q.shape
    dv = v.shape[-1]
    nblk = s // Q_BLOCK
    qb = jnp.moveaxis(q.reshape(b, h, nblk, Q_BLOCK, dq), 2, 0)

    def one_block(qi):
        logits = jnp.concatenate([
            jnp.einsum('bhqd,bhkd->bhqk', qi, k),
            jnp.einsum('bhqd,bhcd->bhqc', qi, k_ctx)], axis=-1).astype(jnp.float32) * scale
        p = jax.nn.softmax(logits, axis=-1).astype(v.dtype)
        return (jnp.einsum('bhqk,bhkd->bhqd', p[..., :s], v)
                + jnp.einsum('bhqc,bhcd->bhqd', p[..., s:], v_ctx))

    o = lax.map(one_block, qb)
    return jnp.moveaxis(o, 0, 2).reshape(b, h, s, dv)


def window_latent_attention(q, k, v, k_ctx, v_ctx, sink, scale):
    b, h, s, d = q.shape
    hk = k.shape[1]
    g = h // hk
    nblk = s // Q_BLOCK
    span = Q_BLOCK + 2 * SW_WINDOW
    pad = ((0, 0), (0, 0), (SW_WINDOW, SW_WINDOW), (0, 0))
    kp, vp = jnp.pad(k, pad), jnp.pad(v, pad)
    qb = jnp.moveaxis(q.reshape(b, hk, g, nblk, Q_BLOCK, d), 3, 0)
    q_off = jnp.arange(Q_BLOCK)
    k_off = jnp.arange(span) - SW_WINDOW
    in_window = jnp.abs(k_off[None, :] - q_off[:, None]) <= SW_WINDOW
    sink_g = sink.reshape(hk, g)[None, :, :, None, None].astype(jnp.float32)

    def one_block(args):
        i, qi = args
        start = i * Q_BLOCK
        kb = lax.dynamic_slice_in_dim(kp, start, span, axis=2)
        vb = lax.dynamic_slice_in_dim(vp, start, span, axis=2)
        k_pos = start + k_off
        valid = in_window & ((k_pos >= 0) & (k_pos < s))[None, :]
        s_loc = jnp.einsum('bkgqd,bkcd->bkgqc', qi, kb).astype(jnp.float32) * scale
        s_loc = jnp.where(valid, s_loc, NEG_INF)
        s_ctx = jnp.einsum('bkgqd,bkcd->bkgqc', qi, k_ctx).astype(jnp.float32) * scale
        p = softmax_f32(jnp.concatenate([s_loc, s_ctx], axis=-1), sink_g).astype(v.dtype)
        return (jnp.einsum('bkgqc,bkcd->bkgqd', p[..., :span], vb)
                + jnp.einsum('bkgqc,bkcd->bkgqd', p[..., span:], v_ctx))

    o = lax.map(one_block, (jnp.arange(nblk), qb))
    return jnp.moveaxis(o, 0, 3).reshape(b, h, s, d)


def neighbourhood_latent_attention(q, k, v, k_ctx, v_ctx, rpb, scale):
    b, h, s, d = q.shape
    rows = s // GRID_W
    kh = min(NA_KH, rows)
    n_nb = kh * NA_KW
    q_col = jnp.arange(GRID_W)
    col0 = jnp.clip(q_col - NA_KW // 2, 0, GRID_W - NA_KW)
    key_col = col0[:, None] + jnp.arange(NA_KW)[None, :]
    col_off = key_col - q_col[:, None] + (NA_KW - 1)
    q_rows = jnp.moveaxis(q.reshape(b, h, rows, GRID_W, d), 2, 0)

    def one_row(args):
        r, qi = args
        row0 = jnp.clip(r - kh // 2, 0, rows - kh)
        key_row = row0 + jnp.arange(kh)
        row_off = key_row - r + (NA_KH - 1)
        key_idx = (key_row[None, :, None] * GRID_W + key_col[:, None, :]).reshape(GRID_W, n_nb)
        bias_idx = (row_off[None, :, None] * (2 * NA_KW - 1) + col_off[:, None, :]).reshape(GRID_W, n_nb)
        kg, vg = k[:, :, key_idx], v[:, :, key_idx]
        s_nb = (jnp.einsum('bhqd,bhqkd->bhqk', qi, kg).astype(jnp.float32) * scale
                + rpb[:, bias_idx].astype(jnp.float32)[None])
        s_ctx = jnp.einsum('bhqd,bhcd->bhqc', qi, k_ctx).astype(jnp.float32) * scale
        p = jax.nn.softmax(jnp.concatenate([s_nb, s_ctx], axis=-1), axis=-1).astype(v.dtype)
        return (jnp.einsum('bhqk,bhqkd->bhqd', p[..., :n_nb], vg)
                + jnp.einsum('bhqc,bhcd->bhqd', p[..., n_nb:], v_ctx))

    o = lax.map(one_row, (jnp.arange(rows), q_rows))
    return jnp.moveaxis(o, 0, 2).reshape(b, h, s, d)


def mixer_neighbourhood(h_lat, h_ctx, w_qkv, w_o, q_norm, k_norm, rpb, need_ctx):
    def project(h):
        q, k, v = jnp.split(h @ w_qkv, 3, axis=-1)
        return (rms_norm(split_heads(q, NA_HEADS), q_norm),
                rms_norm(split_heads(k, NA_HEADS), k_norm),
                split_heads(v, NA_HEADS))
    scale = NA_HEAD_DIM ** -0.5
    q, k, v = project(h_lat)
    qc, kc, vc = project(h_ctx)
    y_lat = merge_heads(neighbourhood_latent_attention(q, k, v, kc, vc, rpb, scale)) @ w_o
    y_ctx = merge_heads(context_attention(qc, kc, vc, scale)) @ w_o if need_ctx else None
    return y_lat, y_ctx


def mixer_window(h_lat, h_ctx, w_qkv, w_o, q_norm, k_norm, sink, rope, need_ctx):
    split_at = [SW_HEADS * SW_HEAD_DIM, (SW_HEADS + SW_KV_HEADS) * SW_HEAD_DIM]
    def project(h, rope_tab):
        q, k, v = jnp.split(h @ w_qkv, split_at, axis=-1)
        q = rms_norm(split_heads(q, SW_HEADS), q_norm)
        k = rms_norm(split_heads(k, SW_KV_HEADS), k_norm)
        if rope_tab is not None:
            q, k = apply_rope(q, *rope_tab), apply_rope(k, *rope_tab)
        return q, k, split_heads(v, SW_KV_HEADS)
    scale = SW_HEAD_DIM ** -0.5
    q, k, v = project(h_lat, rope)
    qc, kc, vc = project(h_ctx, None)
    y_lat = merge_heads(window_latent_attention(q, k, v, kc, vc, sink, scale)) @ w_o
    y_ctx = merge_heads(context_attention(qc, kc, vc, scale, sink)) @ w_o if need_ctx else None
    return y_lat, y_ctx


def mixer_mla(h_lat, h_ctx, w_dqkv, q_a_norm, kv_a_norm, w_uq, w_ukv, q_norm, k_norm, w_o, rope, need_ctx):
    def project(h, rope_tab):
        cq, ckv, k_rope = jnp.split(h @ w_dqkv, [MLA_Q_RANK, MLA_Q_RANK + MLA_KV_RANK], axis=-1)
        q = split_heads(rms_norm(cq, q_a_norm) @ w_uq, MLA_HEADS)
        kv = split_heads(rms_norm(ckv, kv_a_norm) @ w_ukv, MLA_HEADS)
        q_nope = rms_norm(q[..., :MLA_NOPE], q_norm[:MLA_NOPE])
        q_rope = rms_norm(q[..., MLA_NOPE:], q_norm[MLA_NOPE:])
        k_nope = rms_norm(kv[..., :MLA_NOPE], k_norm[:MLA_NOPE])
        v = kv[..., MLA_NOPE:]
        k_rope = rms_norm(k_rope, k_norm[MLA_NOPE:])[:, None]
        if rope_tab is not None:
            q_rope, k_rope = apply_rope(q_rope, *rope_tab), apply_rope(k_rope, *rope_tab)
        q = jnp.concatenate([q_nope, q_rope], axis=-1)
        k = jnp.concatenate([k_nope, jnp.broadcast_to(k_rope, k_nope.shape[:-1] + (MLA_ROPE,))], axis=-1)
        return q, k, v
    scale = (MLA_NOPE + MLA_ROPE) ** -0.5
    q, k, v = project(h_lat, rope)
    qc, kc, vc = project(h_ctx, None)
    y_lat = merge_heads(dense_latent_attention(q, k, v, kc, vc, scale)) @ w_o
    y_ctx = merge_heads(context_attention(qc, kc, vc, scale)) @ w_o if need_ctx else None
    return y_lat, y_ctx


def mixer_diff(h_lat, h_ctx, w_qkv, q_norm, k_norm, lam, subln, w_o, rope, lambda_init, need_ctx):
    qk_width = 2 * DIFF_HEADS * DIFF_HEAD_DIM
    def project(h, rope_tab):
        q, k, v = jnp.split(h @ w_qkv, [qk_width, 2 * qk_width], axis=-1)
        q = rms_norm(split_heads(q, 2 * DIFF_HEADS), q_norm)
        k = rms_norm(split_heads(k, 2 * DIFF_HEADS), k_norm)
        if rope_tab is not None:
            q, k = apply_rope(q, *rope_tab), apply_rope(k, *rope_tab)
        return q[:, 0::2], q[:, 1::2], k[:, 0::2], k[:, 1::2], split_heads(v, DIFF_HEADS)
    lf = lam.astype(jnp.float32)
    lam_full = jnp.exp(jnp.sum(lf[0] * lf[1])) - jnp.exp(jnp.sum(lf[2] * lf[3])) + lambda_init
    scale = DIFF_HEAD_DIM ** -0.5

    def combine(o1, o2):
        o = o1 - lam_full.astype(o1.dtype) * o2
        return merge_heads(rms_norm(o, subln) * (1.0 - lambda_init)) @ w_o

    q1, q2, k1, k2, v = project(h_lat, rope)
    q1c, q2c, k1c, k2c, vc = project(h_ctx, None)
    y_lat = combine(dense_latent_attention(q1, k1, v, k1c, vc, scale),
                    dense_latent_attention(q2, k2, v, k2c, vc, scale))
    y_ctx = (combine(context_attention(q1c, k1c, vc, scale), context_attention(q2c, k2c, vc, scale))
             if need_ctx else None)
    return y_lat, y_ctx


def hierarchical_moe(h, w_rg, b_rg, w_re, b_re, w_gu, w_down):
    t, d = h.shape
    hf = h.astype(jnp.float32)
    g_prob = jax.nn.softmax(hf @ w_rg.astype(jnp.float32) + b_rg.astype(jnp.float32), axis=-1)
    g_w, g_idx = lax.top_k(g_prob, 1)
    e_logits = (hf @ w_re.astype(jnp.float32) + b_re.astype(jnp.float32)).reshape(t, N_GROUPS, EXPERTS_PER_GROUP)
    e_logits = jnp.take_along_axis(e_logits, g_idx[:, :, None], axis=1)[:, 0]
    e_w, e_idx = lax.top_k(jax.nn.softmax(e_logits, axis=-1), TOP_K)
    gate = g_w * e_w / jnp.sum(e_w, axis=-1, keepdims=True)
    expert = (g_idx * EXPERTS_PER_GROUP + e_idx).astype(jnp.int32)
    n_rows = t * TOP_K
    flat_expert = expert.reshape(-1)
    order = jnp.argsort(flat_expert).astype(jnp.int32)
    sorted_expert = flat_expert[order]
    counts = jnp.zeros((N_EXPERTS,), jnp.int32).at[flat_expert].add(1)
    padded = (counts + MOE_BLOCK - 1) // MOE_BLOCK * MOE_BLOCK
    seg_end_p = jnp.cumsum(padded)
    seg_start_p = seg_end_p - padded
    seg_start = jnp.cumsum(counts) - counts
    dest = seg_start_p[sorted_expert] + jnp.arange(n_rows, dtype=jnp.int32) - seg_start[sorted_expert]
    n_pad_rows = -(-(n_rows + N_EXPERTS * (MOE_BLOCK - 1)) // MOE_BLOCK) * MOE_BLOCK
    n_blocks = n_pad_rows // MOE_BLOCK
    row_token = jnp.zeros((n_pad_rows,), jnp.int32).at[dest].set(order // TOP_K)
    row_gate = jnp.zeros((n_pad_rows,), jnp.float32).at[dest].set(gate.reshape(-1)[order])
    block_expert = jnp.minimum(
        jnp.searchsorted(seg_end_p, jnp.arange(n_blocks, dtype=jnp.int32) * MOE_BLOCK, side='right'),
        N_EXPERTS - 1)

    def one_block(args):
        e, tok, gw = args
        rows = h[tok]
        gte, up = jnp.split(rows @ w_gu[e], 2, axis=-1)
        return (jax.nn.silu(gte) * up * gw[:, None].astype(h.dtype)) @ w_down[e]

    out = lax.map(one_block, (block_expert,
                              row_token.reshape(n_blocks, MOE_BLOCK),
                              row_gate.reshape(n_blocks, MOE_BLOCK)))
    return jnp.zeros_like(h).at[row_token].add(out.reshape(n_pad_rows, d))


def setup_inputs(seed: int = 0) -> dict:
    keys = iter(jax.random.split(jax.random.key(seed), 40))

    def normal(shape, std):
        return std * jax.random.normal(next(keys), shape, jnp.float32)

    def gain(shape):
        return 1.0 + 0.1 * jax.random.normal(next(keys), shape, jnp.float32)

    d = D_MODEL
    n_a, n_b, n_c, n_d = (_layers_of_kind(kind) for kind in range(N_MIXERS))
    qk_diff = 2 * DIFF_HEADS * DIFF_HEAD_DIM
    return {
        'x': normal((BATCH, SEQ, d), 1.0),
        'c': normal((BATCH, d), 1.0),
        'ctx': normal((BATCH, CTX_LEN, d), 1.0),
        'c_ctx': normal((d,), 1.0),
        'ada_w': normal((DEPTH, d, 6 * d), 0.5 * d ** -0.5),
        'ada_b': normal((DEPTH, 6 * d), 0.02),
        'norm_g': gain((DEPTH, 2, d)),
        'moe_w_router_group': normal((DEPTH, d, N_GROUPS), d ** -0.5),
        'moe_b_router_group': normal((DEPTH, N_GROUPS), 0.01),
        'moe_w_router_expert': normal((DEPTH, d, N_EXPERTS), d ** -0.5),
        'moe_b_router_expert': normal((DEPTH, N_EXPERTS), 0.01),
        'moe_w_gate_up': normal((DEPTH, N_EXPERTS, d, 2 * D_EXPERT), d ** -0.5),
        'moe_w_down': normal((DEPTH, N_EXPERTS, D_EXPERT, d), D_EXPERT ** -0.5),
        'na_w_qkv': normal((n_a, d, 3 * NA_HEADS * NA_HEAD_DIM), d ** -0.5),
        'na_w_o': normal((n_a, NA_HEADS * NA_HEAD_DIM, d), (NA_HEADS * NA_HEAD_DIM) ** -0.5),
        'na_q_norm': gain((n_a, NA_HEAD_DIM)),
        'na_k_norm': gain((n_a, NA_HEAD_DIM)),
        'na_rpb': normal((n_a, NA_HEADS, (2 * NA_KH - 1) * (2 * NA_KW - 1)), 0.5),
        'sw_w_qkv': normal((n_b, d, (SW_HEADS + 2 * SW_KV_HEADS) * SW_HEAD_DIM), d ** -0.5),
        'sw_w_o': normal((n_b, SW_HEADS * SW_HEAD_DIM, d), (SW_HEADS * SW_HEAD_DIM) ** -0.5),
        'sw_q_norm': gain((n_b, SW_HEAD_DIM)),
        'sw_k_norm': gain((n_b, SW_HEAD_DIM)),
        'sw_sink': normal((n_b, SW_HEADS), 1.0),
        'mla_w_dqkv': normal((n_c, d, MLA_Q_RANK + MLA_KV_RANK + MLA_ROPE), d ** -0.5),
        'mla_q_a_norm': gain((n_c, MLA_Q_RANK)),
        'mla_kv_a_norm': gain((n_c, MLA_KV_RANK)),
        'mla_w_uq': normal((n_c, MLA_Q_RANK, MLA_HEADS * (MLA_NOPE + MLA_ROPE)), MLA_Q_RANK ** -0.5),
        'mla_w_ukv': normal((n_c, MLA_KV_RANK, MLA_HEADS * (MLA_NOPE + MLA_V)), MLA_KV_RANK ** -0.5),
        'mla_q_norm': gain((n_c, MLA_NOPE + MLA_ROPE)),
        'mla_k_norm': gain((n_c, MLA_NOPE + MLA_ROPE)),
        'mla_w_o': normal((n_c, MLA_HEADS * MLA_V, d), (MLA_HEADS * MLA_V) ** -0.5),
        'diff_w_qkv': normal((n_d, d, 2 * qk_diff + DIFF_HEADS * 2 * DIFF_HEAD_DIM), d ** -0.5),
        'diff_q_norm': gain((n_d, DIFF_HEAD_DIM)),
        'diff_k_norm': gain((n_d, DIFF_HEAD_DIM)),
        'diff_lambda': normal((n_d, 4, DIFF_HEAD_DIM), 0.1),
        'diff_subln': gain((n_d, 2 * DIFF_HEAD_DIM)),
        'diff_w_o': normal((n_d, DIFF_HEADS * 2 * DIFF_HEAD_DIM, d), (DIFF_HEADS * 2 * DIFF_HEAD_DIM) ** -0.5),
    }


def reference(x, c, ctx, c_ctx, ada_w, ada_b, norm_g,
              moe_w_router_group, moe_b_router_group, moe_w_router_expert, moe_b_router_expert,
              moe_w_gate_up, moe_w_down,
              na_w_qkv, na_w_o, na_q_norm, na_k_norm, na_rpb,
              sw_w_qkv, sw_w_o, sw_q_norm, sw_k_norm, sw_sink,
              mla_w_dqkv, mla_q_a_norm, mla_kv_a_norm, mla_w_uq, mla_w_ukv, mla_q_norm, mla_k_norm, mla_w_o,
              diff_w_qkv, diff_q_norm, diff_k_norm, diff_lambda, diff_subln, diff_w_o):
    b, s, d = x.shape
    n_ctx = ctx.shape[1]
    rope_head = axial_rope(s, SW_HEAD_DIM)
    rope_mla = axial_rope(s, MLA_ROPE)
    x_lat, x_ctx = x, ctx
    for l in range(DEPTH):
        kind, j = l % N_MIXERS, l // N_MIXERS
        need_ctx = l < DEPTH - 1
        mod_lat = (jax.nn.silu(c) @ ada_w[l] + ada_b[l])[:, None, :]
        mod_ctx = (jax.nn.silu(c_ctx) @ ada_w[l] + ada_b[l])[None, None, :]
        sh1, sc1, g1, sh2, sc2, g2 = jnp.split(mod_lat, 6, axis=-1)
        csh1, csc1, cg1, csh2, csc2, cg2 = jnp.split(mod_ctx, 6, axis=-1)
        h_lat = modulate(x_lat, norm_g[l, 0], sh1, sc1)
        h_ctx = modulate(x_ctx, norm_g[l, 0], csh1, csc1)
        if kind == 0:
            y_lat, y_ctx = mixer_neighbourhood(h_lat, h_ctx, na_w_qkv[j], na_w_o[j], na_q_norm[j],
                                               na_k_norm[j], na_rpb[j], need_ctx)
        elif kind == 1:
            y_lat, y_ctx = mixer_window(h_lat, h_ctx, sw_w_qkv[j], sw_w_o[j], sw_q_norm[j], sw_k_norm[j],
                                        sw_sink[j], rope_head, need_ctx)
        elif kind == 2:
            y_lat, y_ctx = mixer_mla(h_lat, h_ctx, mla_w_dqkv[j], mla_q_a_norm[j], mla_kv_a_norm[j],
                                     mla_w_uq[j], mla_w_ukv[j], mla_q_norm[j], mla_k_norm[j], mla_w_o[j],
                                     rope_mla, need_ctx)
        else:
            y_lat, y_ctx = mixer_diff(h_lat, h_ctx, diff_w_qkv[j], diff_q_norm[j], diff_k_norm[j],
                                      diff_lambda[j], diff_subln[j], diff_w_o[j], rope_head,
                                      0.8 - 0.6 * math.exp(-0.3 * l), need_ctx)
        x_lat = x_lat + g1 * y_lat
        h_lat = modulate(x_lat, norm_g[l, 1], sh2, sc2)
        moe_args = (moe_w_router_group[l], moe_b_router_group[l], moe_w_router_expert[l],
                    moe_b_router_expert[l], moe_w_gate_up[l], moe_w_down[l])
        if need_ctx:
            x_ctx = x_ctx + cg1 * y_ctx
            h_ctx = modulate(x_ctx, norm_g[l, 1], csh2, csc2)
            tokens = jnp.concatenate([h_lat.reshape(-1, d), h_ctx.reshape(-1, d)], axis=0)
            y = hierarchical_moe(tokens, *moe_args)
            x_lat = x_lat + g2 * y[:b * s].reshape(b, s, d)
            x_ctx = x_ctx + cg2 * y[b * s:].reshape(b, n_ctx, d)
        else:
            y = hierarchical_moe(h_lat.reshape(-1, d), *moe_args)
            x_lat = x_lat + g2 * y.reshape(b, s, d)
    return x_lat
```

```python
import functools
import math

import jax
import jax.numpy as jnp
import numpy as np
from jax import lax
from jax.experimental import pallas as pl
from jax.experimental.pallas import tpu as pltpu

F32 = jnp.float32
BF16 = jnp.bfloat16

D_MODEL = 1024
SEQ = 4096
CTX = 256
SL = SEQ + CTX
GRID_W = 64
GRID_ROWS = SEQ // GRID_W
EPS = 1e-6
ROPE_BASE = 10000.0
NEG = -1e30
LOG2E = 1.4426950408889634

NA_KH, NA_KW = 8, 16
SW_WINDOW = 128
SW_HEADS, SW_KV_HEADS = 16, 4
MLA_Q_RANK, MLA_KV_RANK, MLA_NOPE, MLA_ROPE, MLA_V, MLA_HEADS = 384, 256, 64, 32, 64, 16
DIFF_HEADS = 8
N_GROUPS, EXPERTS_PER_GROUP, N_EXPERTS, D_EXPERT = 4, 8, 32, 256

LANES = 128
MXU_N = 256
TM = 256
TILES_PER_SLAB = SL // TM
MOE_BLK = 256
VMEM_LIMIT = 56 * 1024 * 1024


def _cparams(*sem):
    return pltpu.CompilerParams(dimension_semantics=sem, vmem_limit_bytes=VMEM_LIMIT)


def _lane_iota(shape):
    return lax.broadcasted_iota(jnp.int32, shape, len(shape) - 1)


def _dot(a, b):
    return jnp.dot(a, b, preferred_element_type=F32)


def _dot_nt(a, b):
    return lax.dot_general(a, b, (((1,), (1,)), ((), ())), preferred_element_type=F32)


def _ada_kernel(cc_ref, w_ref, b_ref, o_ref):
    a = cc_ref[...]
    a = a * jax.nn.sigmoid(a)
    hi = a.astype(BF16)
    lo = (a - hi.astype(F32)).astype(BF16)
    w = w_ref[...]
    whi = w.astype(BF16)
    wlo = (w - whi.astype(F32)).astype(BF16)
    o_ref[...] = _dot(hi, whi) + _dot(lo, whi) + _dot(hi, wlo) + b_ref[...]


def _ada_table(cc, ada_w, ada_b):
    depth, d, n = ada_w.shape
    tn = 1024
    return pl.pallas_call(
        _ada_kernel,
        grid=(depth, n // tn),
        in_specs=[pl.BlockSpec((16, d), lambda l, j: (0, 0)),
                  pl.BlockSpec((None, d, tn), lambda l, j: (l, 0, j)),
                  pl.BlockSpec((None, 1, tn), lambda l, j: (l, 0, j))],
        out_specs=pl.BlockSpec((None, 16, tn), lambda l, j: (l, 0, j)),
        out_shape=jax.ShapeDtypeStruct((depth, 16, n), F32),
        compiler_params=_cparams("parallel", "parallel"),
        name="ada_table",
    )(cc, ada_w, ada_b.reshape(depth, 1, n))


def _mod_spec(part):
    def index(i):
        m = jnp.where(i % TILES_PER_SLAB == TILES_PER_SLAB - 1, 8, i // TILES_PER_SLAB)
        return (m * 6 + part, 0, 0)
    return pl.BlockSpec((None, 1, D_MODEL), index)


def _modulated(x, g, sc, sh):
    ms = jnp.mean(x * x, axis=-1, keepdims=True)
    return (x * lax.rsqrt(ms + EPS)) * g * (1.0 + sc) + sh


def _group_norm_rope(y, bd, invcnt, gain, cos=None, sin=None, half=None):
    ss = _dot((y * y).astype(BF16), bd)
    yn = y * lax.rsqrt(ss * invcnt + EPS) * gain
    if cos is not None:
        lane = _lane_iota(yn.shape)
        fwd = pltpu.roll(yn, LANES - half, 1)
        bwd = pltpu.roll(yn, half, 1)
        partner = jnp.where((lane & (2 * half - 1)) < half, fwd, bwd)
        yn = yn * cos + partner * sin
    return yn


def _block_diag(group_of_lane):
    g = np.asarray(group_of_lane)
    return jnp.asarray((g[:, None] == g[None, :]).astype(np.float32), dtype=BF16)


def _rope_tables(rot_dim, lane_map):
    t = jnp.arange(SEQ, dtype=jnp.int32)
    n_freq = rot_dim // 4
    inv_freq = ROPE_BASE ** (-jnp.arange(n_freq, dtype=F32) / n_freq)
    rows = (t // GRID_W).astype(F32)
    cols = (t % GRID_W).astype(F32)
    ang = jnp.concatenate([rows[:, None] * inv_freq, cols[:, None] * inv_freq], axis=-1)
    cos, sin = jnp.cos(ang), jnp.sin(ang)
    lm = np.asarray(lane_map)
    rot = lm >= 0
    idx = np.where(rot, lm % (rot_dim // 2), 0)
    sign = np.where(lm < rot_dim // 2, -1.0, 1.0).astype(np.float32)
    cos_t = jnp.where(rot[None, :], cos[:, idx], 1.0)
    sin_t = jnp.where(rot[None, :], sin[:, idx] * sign[None, :], 0.0)
    ident_c = jnp.ones((CTX, LANES), F32)
    ident_s = jnp.zeros((CTX, LANES), F32)
    return jnp.concatenate([cos_t, ident_c], 0), jnp.concatenate([sin_t, ident_s], 0)


def _proj_kernel(plan, rope_half, x_ref, g_ref, sc_ref, sh_ref, w_ref, bd_ref, vec_ref, *rest):
    if rope_half is not None:
        cos_ref, sin_ref = rest[:2]
        outs = rest[2:]
        cos, sin = cos_ref[...], sin_ref[...]
    else:
        outs = rest
        cos = sin = None
    h = _modulated(x_ref[...], g_ref[...], sc_ref[...], sh_ref[...]).astype(BF16)
    bd = bd_ref[...]
    for c0, ep, out_idx, oc0 in plan:
        acc = _dot(h, w_ref[:, c0:c0 + MXU_N])
        for s in range(MXU_N // LANES):
            y = acc[:, s * LANES:(s + 1) * LANES]
            if ep is not None:
                y = _group_norm_rope(y, bd, vec_ref[ep, 0:1, :], vec_ref[ep, 1:2, :], cos, sin, rope_half)
            lo = oc0 + s * LANES
            outs[out_idx][:, lo:lo + LANES] = y.astype(BF16)


def _project(x2, mod3, g, w_bf, plan, out_widths, bd, vec, rope=None, rope_half=None, name="proj"):
    t, d = x2.shape
    n = w_bf.shape[1]
    n_tiles = t // TM
    in_specs = [pl.BlockSpec((TM, d), lambda i: (i, 0)),
                pl.BlockSpec((1, d), lambda i: (0, 0)),
                _mod_spec(1), _mod_spec(0),
                pl.BlockSpec((d, n), lambda i: (0, 0)),
                pl.BlockSpec(bd.shape, lambda i: (0, 0)),
                pl.BlockSpec(vec.shape, lambda i: (0, 0, 0))]
    args = [x2, g.reshape(1, d), mod3, mod3, w_bf, bd, vec]
    if rope is not None:
        in_specs += [pl.BlockSpec((TM, LANES), lambda i: (i % TILES_PER_SLAB, 0))] * 2
        args += list(rope)
    return pl.pallas_call(
        functools.partial(_proj_kernel, plan, rope_half),
        grid=(n_tiles,),
        in_specs=in_specs,
        out_specs=[pl.BlockSpec((TM, w), lambda i: (i, 0)) for w in out_widths],
        out_shape=[jax.ShapeDtypeStruct((t, w), BF16) for w in out_widths],
        compiler_params=_cparams("parallel"),
        name=name,
    )(*args)


def _qkv_plan(nq, nk, nv):
    plan = []
    for c0 in range(0, nq, MXU_N):
        plan.append((c0, 0, 0, c0))
    for c0 in range(0, nk, MXU_N):
        plan.append((nq + c0, 1, 1, c0))
    for c0 in range(0, nv, MXU_N):
        plan.append((nq + nk + c0, None, 2, c0))
    return tuple(plan)


def _head64_vec(q_norm, k_norm, q_scale):
    inv = jnp.full((LANES,), 1.0 / 64, F32)
    z = jnp.zeros((6, LANES), F32)
    vq = jnp.concatenate([inv[None], (jnp.tile(q_norm, 2) * q_scale)[None], z], 0)
    vk = jnp.concatenate([inv[None], jnp.tile(k_norm, 2)[None], z], 0)
    return jnp.stack([vq, vk], 0)


def _mla_proj_kernel(x_ref, g_ref, sc_ref, sh_ref, wd_ref, wuq_ref, wuk_ref, wuv_ref, bd_ref, vec_ref,
                     an_ref, cos_ref, sin_ref, q_out, k_out, v_out):
    h = _modulated(x_ref[...], g_ref[...], sc_ref[...], sh_ref[...]).astype(BF16)
    cos, sin = cos_ref[...], sin_ref[...]
    half = MLA_ROPE // 2
    cq = _dot(h, wd_ref[:, 0:MLA_Q_RANK])
    ckv = _dot(h, wd_ref[:, MLA_Q_RANK:MLA_Q_RANK + MLA_KV_RANK])
    kr = _dot(h, wd_ref[:, MLA_Q_RANK + MLA_KV_RANK:])
    cq = cq * lax.rsqrt(jnp.mean(cq * cq, axis=-1, keepdims=True) + EPS) * an_ref[0:1, 0:MLA_Q_RANK]
    ckv = ckv * lax.rsqrt(jnp.mean(ckv * ckv, axis=-1, keepdims=True) + EPS) * an_ref[1:2, 0:MLA_KV_RANK]
    cq = cq.astype(BF16)
    ckv = ckv.astype(BF16)
    kr = _group_norm_rope(kr, bd_ref[1], vec_ref[2, 0:1, :], vec_ref[2, 1:2, :], cos[:, LANES:], sin[:, LANES:], half)
    kr = pltpu.roll(kr, MLA_NOPE, 1)
    for hd in range(MLA_HEADS):
        c0 = hd * LANES
        qh = _dot(cq, wuq_ref[:, c0:c0 + LANES])
        qh = _group_norm_rope(qh, bd_ref[0], vec_ref[0, 0:1, :], vec_ref[0, 1:2, :], cos[:, :LANES], sin[:, :LANES], half)
        q_out[:, c0:c0 + LANES] = qh.astype(BF16)
        kh = _dot(ckv, wuk_ref[:, c0:c0 + LANES])
        kh = _group_norm_rope(kh, bd_ref[0], vec_ref[1, 0:1, :], vec_ref[1, 1:2, :]) + kr
        k_out[:, c0:c0 + LANES] = kh.astype(BF16)
    for c0 in range(0, MLA_HEADS * MLA_V, MXU_N):
        v_out[:, c0:c0 + MXU_N] = _dot(ckv, wuv_ref[:, c0:c0 + MXU_N]).astype(BF16)


def _mla_project(x2, mod3, g, p):
    t, d = x2.shape
    hq = MLA_NOPE + MLA_ROPE
    n_tiles = t // TM
    wd = jnp.pad(p["w_dqkv"], ((0, 0), (0, LANES - MLA_ROPE))).astype(BF16)
    wuq = p["w_uq"].reshape(MLA_Q_RANK, MLA_HEADS, hq)
    wuq = jnp.pad(wuq, ((0, 0), (0, 0), (0, LANES - hq))).reshape(MLA_Q_RANK, MLA_HEADS * LANES).astype(BF16)
    wukv = p["w_ukv"].reshape(MLA_KV_RANK, MLA_HEADS, MLA_NOPE + MLA_V)
    wuk = jnp.pad(wukv[:, :, :MLA_NOPE], ((0, 0), (0, 0), (0, LANES - MLA_NOPE)))
    wuk = wuk.reshape(MLA_KV_RANK, MLA_HEADS * LANES).astype(BF16)
    wuv = wukv[:, :, MLA_NOPE:].reshape(MLA_KV_RANK, MLA_HEADS * MLA_V).astype(BF16)
    scale = (MLA_NOPE + MLA_ROPE) ** -0.5 * LOG2E
    lane = np.arange(LANES)
    grp_head = np.where(lane < 64, 0, np.where(lane < 96, 1, 2))
    grp_kr = np.where(lane < 32, 0, 1)
    bd = jnp.stack([_block_diag(grp_head), _block_diag(grp_kr)], 0)
    inv_head = jnp.asarray(np.where(lane < 64, 1 / 64, 1 / 32), F32)
    zpad = jnp.zeros((LANES - hq,), F32)
    gq = jnp.concatenate([p["q_norm"], zpad]) * scale
    gk = jnp.concatenate([p["k_norm"][:MLA_NOPE], jnp.zeros((LANES - MLA_NOPE,), F32)])
    gkr = jnp.concatenate([p["k_norm"][MLA_NOPE:], jnp.zeros((LANES - MLA_ROPE,), F32)])
    inv_kr = jnp.asarray(np.where(lane < 32, 1 / 32, 1 / 96), F32)
    z6 = jnp.zeros((6, LANES), F32)
    vec = jnp.stack([jnp.concatenate([inv_head[None], gq[None], z6], 0),
                     jnp.concatenate([inv_head[None], gk[None], z6], 0),
                     jnp.concatenate([inv_kr[None], gkr[None], z6], 0)], 0)
    an = jnp.zeros((8, MLA_Q_RANK), F32)
    an = an.at[0].set(p["q_a_norm"]).at[1, :MLA_KV_RANK].set(p["kv_a_norm"])
    lm_head = np.where((lane >= 64) & (lane < 96), lane - 64, -1)
    lm_kr = np.where(lane < 32, lane, -1)
    ch, sh_ = _rope_tables(MLA_ROPE, lm_head)
    ck, sk = _rope_tables(MLA_ROPE, lm_kr)
    cos2 = jnp.concatenate([ch, ck], 1)
    sin2 = jnp.concatenate([sh_, sk], 1)

    full = lambda a: pl.BlockSpec(a.shape, lambda i: (0,) * a.ndim)
    rope_spec = pl.BlockSpec((TM, 2 * LANES), lambda i: (i % TILES_PER_SLAB, 0))
    out_w = (MLA_HEADS * LANES, MLA_HEADS * LANES, MLA_HEADS * MLA_V)
    return pl.pallas_call(
        _mla_proj_kernel,
        grid=(n_tiles,),
        in_specs=[pl.BlockSpec((TM, d), lambda i: (i, 0)), pl.BlockSpec((1, d), lambda i: (0, 0)),
                  _mod_spec(1), _mod_spec(0), full(wd), full(wuq), full(wuk), full(wuv), full(bd), full(vec),
                  full(an), rope_spec, rope_spec],
        out_specs=[pl.BlockSpec((TM, w), lambda i: (i, 0)) for w in out_w],
        out_shape=[jax.ShapeDtypeStruct((t, w), BF16) for w in out_w],
        compiler_params=_cparams("parallel"),
        name="mla_proj",
    )(x2, g.reshape(1, d), mod3, mod3, wd, wuq, wuk, wuv, bd, vec, an, cos2, sin2)


def _softmax_pv(s_list, v_list, sink=None):
    m = functools.reduce(jnp.maximum, [jnp.max(s, axis=-1, keepdims=True) for s in s_list])
    if sink is not None:
        m = jnp.maximum(m, sink)
    ps = [jnp.exp2(s - m) for s in s_list]
    l = functools.reduce(lambda a, b: a + b, [jnp.sum(p, axis=-1, keepdims=True) for p in ps])
    if sink is not None:
        l = l + jnp.exp2(sink - m)
    o = functools.reduce(lambda a, b: a + b, [_dot(p.astype(BF16), v) for p, v in zip(ps, v_list)])
    return o * (1.0 / l)


def _split_pair(x, lo):
    zero = jnp.zeros_like(x)
    return jnp.where(lo, x, zero), jnp.where(lo, zero, x)


def _na_kernel(q_ref, k_ref, v_ref, bias_ref, o_ref):
    lo = _lane_iota((1, LANES)) < 64
    kc = k_ref[SEQ:SL, :]
    vc = v_ref[SEQ:SL, :]
    nb = NA_KH * GRID_W

    def pair_attend(q, ks, vs, bias):
        n = q.shape[0]
        qa, qb = _split_pair(q, lo)
        qs = jnp.concatenate([qa, qb], axis=0)
        s_list = [_dot_nt(qs, k) for k in ks]
        if bias is not None:
            s_list[0] = s_list[0] + bias
        o = _softmax_pv(s_list, vs)
        return jnp.where(lo, o[:n], o[n:])

    def row_body(r, carry):
        row0 = jnp.clip(r - NA_KH // 2, 0, GRID_ROWS - NA_KH)
        q = q_ref[pl.ds(pl.multiple_of(r * GRID_W, GRID_W), GRID_W), :]
        ks = pl.multiple_of(row0 * GRID_W, GRID_W)
        kn = k_ref[pl.ds(ks, nb), :]
        vn = v_ref[pl.ds(ks, nb), :]
        o = pair_attend(q, [kn, kc], [vn, vc], bias_ref[r - row0])
        o_ref[pl.ds(pl.multiple_of(r * GRID_W, GRID_W), GRID_W), :] = o.astype(BF16)
        return carry

    lax.fori_loop(0, GRID_ROWS, row_body, 0)
    o_ref[SEQ:SL, :] = pair_attend(q_ref[SEQ:SL, :], [kc], [vc], None).astype(BF16)


def _na_bias_table(rpb):
    heads = rpb.shape[0]
    c = np.arange(NA_KH)[:, None, None, None]
    qc = np.arange(GRID_W)[None, :, None, None]
    kr = np.arange(NA_KH)[None, None, :, None]
    kc = np.arange(GRID_W)[None, None, None, :]
    col0 = np.clip(qc - NA_KW // 2, 0, GRID_W - NA_KW)
    valid = (kc >= col0) & (kc < col0 + NA_KW)
    row_off = kr - c + (NA_KH - 1)
    col_off = kc - qc + (NA_KW - 1)
    idx = np.where(valid, row_off * (2 * NA_KW - 1) + col_off, 0)
    idx = np.broadcast_to(idx, (NA_KH, GRID_W, NA_KH, GRID_W)).reshape(NA_KH, GRID_W, NA_KH * GRID_W)
    valid = np.broadcast_to(valid, (NA_KH, GRID_W, NA_KH, GRID_W)).reshape(NA_KH, GRID_W, NA_KH * GRID_W)
    tab = jnp.where(valid[None], rpb[:, idx] * LOG2E, NEG)
    tab = tab.reshape(heads // 2, 2, NA_KH, GRID_W, NA_KH * GRID_W).transpose(2, 0, 1, 3, 4)
    return tab.reshape(NA_KH, heads // 2, 2 * GRID_W, NA_KH * GRID_W)


def _na_attention(q, k, v, bias, b):
    hp = q.shape[-1] // LANES
    slab = lambda: pl.BlockSpec((None, SL, LANES), lambda i, j: (i, 0, j))
    return pl.pallas_call(
        _na_kernel,
        grid=(b, hp),
        in_specs=[slab(), slab(), slab(),
                  pl.BlockSpec((NA_KH, None, 2 * GRID_W, NA_KH * GRID_W), lambda i, j: (0, j, 0, 0))],
        out_specs=slab(),
        out_shape=jax.ShapeDtypeStruct(q.shape, BF16),
        compiler_params=_cparams("parallel", "parallel"),
        name="na_attention",
    )(q, k, v, bias)


Q_BLK = 128


def _sw_kernel(sink_ref, q_ref, k_ref, v_ref, o_ref):
    g = pl.program_id(1)
    lane = _lane_iota((1, LANES))
    lo = lane < 64
    first_half = (jnp.zeros((1, LANES), jnp.int32) + (g % 2)) == 0
    target = (lane // 64) == (g % 2)
    kc = k_ref[SEQ:SL, :]
    vc = v_ref[SEQ:SL, :]
    span = Q_BLK + 2 * SW_WINDOW
    sinks = [sink_ref[4 * g + t] for t in range(4)]

    def swap(x):
        return jnp.concatenate([x[:, 64:], x[:, :64]], axis=1)

    def stack_heads(q):
        parts = []
        for blk in range(2):
            qb = q[:, blk * LANES:(blk + 1) * LANES]
            qs = swap(qb)
            even = jnp.where(first_half, qb, qs)
            odd = jnp.where(first_half, qs, qb)
            zero = jnp.zeros_like(qb)
            parts += [jnp.where(target, even, zero), jnp.where(target, odd, zero)]
        return jnp.concatenate(parts, axis=0)

    def unstack_heads(o, n):
        blocks = []
        for blk in range(2):
            oe = o[(2 * blk) * n:(2 * blk + 1) * n]
            oo = o[(2 * blk + 1) * n:(2 * blk + 2) * n]
            oe = jnp.where(first_half, oe, pltpu.roll(oe, 64, 1))
            oo = jnp.where(first_half, pltpu.roll(oo, 64, 1), oo)
            blocks.append(jnp.where(lo, oe, oo))
        return jnp.concatenate(blocks, axis=1)

    def sink_col(n):
        row = lax.broadcasted_iota(jnp.int32, (4 * n, 1), 0)
        return jnp.where(row < n, sinks[0], jnp.where(row < 2 * n, sinks[1],
                         jnp.where(row < 3 * n, sinks[2], sinks[3])))

    def block_body(i, carry):
        q0 = pl.multiple_of(i * Q_BLK, Q_BLK)
        start = pl.multiple_of(jnp.clip((i - 1) * Q_BLK, 0, SEQ - span), Q_BLK)
        qs = stack_heads(q_ref[pl.ds(q0, Q_BLK), :])
        kw = k_ref[pl.ds(start, span), :]
        vw = v_ref[pl.ds(start, span), :]
        s_loc = _dot_nt(qs, kw)
        qpos = q0 + (lax.broadcasted_iota(jnp.int32, (4 * Q_BLK, 1), 0) & (Q_BLK - 1))
        kpos = start + _lane_iota((1, span))
        s_loc = jnp.where(jnp.abs(kpos - qpos) <= SW_WINDOW, s_loc, NEG)
        s_ctx = _dot_nt(qs, kc)
        o = _softmax_pv([s_loc, s_ctx], [vw, vc], sink_col(Q_BLK))
        o_ref[pl.ds(q0, Q_BLK), :] = unstack_heads(o, Q_BLK).astype(BF16)
        return carry

    lax.fori_loop(0, SEQ // Q_BLK, block_body, 0)
    qs = stack_heads(q_ref[SEQ:SL, :])
    o = _softmax_pv([_dot_nt(qs, kc)], [vc], sink_col(CTX))
    o_ref[SEQ:SL, :] = unstack_heads(o, CTX).astype(BF16)


def _sw_attention(q, k, v, sink2, b):
    nq = 4 * 64
    slabq = lambda: pl.BlockSpec((None, SL, nq), lambda i, g: (i, 0, g))
    slabk = lambda: pl.BlockSpec((None, SL, LANES), lambda i, g: (i, 0, g // 2))
    return pl.pallas_call(
        _sw_kernel,
        grid=(b, SW_KV_HEADS),
        in_specs=[pl.BlockSpec(memory_space=pltpu.SMEM), slabq(), slabk(), slabk()],
        out_specs=slabq(),
        out_shape=jax.ShapeDtypeStruct(q.shape, BF16),
        compiler_params=_cparams("parallel", "parallel"),
        name="sw_attention",
    )(sink2, q, k, v)


DENSE_TQ = 256
DENSE_TK = 512


def _online_step(s, v, m_ref, l_ref, acc_ref):
    m_prev = m_ref[...]
    m_new = jnp.maximum(m_prev, jnp.max(s, axis=-1, keepdims=True))
    alpha = jnp.exp2(m_prev - m_new)
    p = jnp.exp2(s - m_new)
    l_ref[...] = alpha * l_ref[...] + jnp.sum(p, axis=-1, keepdims=True)
    acc_ref[...] = alpha * acc_ref[...] + _dot(p.astype(BF16), v)
    m_ref[...] = m_new


def _dense_rows(qs, k_ref, v_ref, n_lat, kc, vc, m_ref, l_ref, acc_ref):
    rows = qs.shape[0]
    m_ref[0:rows, :] = jnp.full((rows, 1), NEG, F32)
    l_ref[0:rows, :] = jnp.zeros((rows, 1), F32)
    acc_ref[0:rows, :] = jnp.zeros((rows, acc_ref.shape[1]), F32)
    mr, lr, ar = m_ref.at[0:rows, :], l_ref.at[0:rows, :], acc_ref.at[0:rows, :]

    def kv_body(c, carry):
        k0 = pl.multiple_of(c * DENSE_TK, DENSE_TK)
        _online_step(_dot_nt(qs, k_ref[pl.ds(k0, DENSE_TK), :]), v_ref[pl.ds(k0, DENSE_TK), :], mr, lr, ar)
        return carry

    if n_lat:
        lax.fori_loop(0, n_lat // DENSE_TK, kv_body, 0)
    _online_step(_dot_nt(qs, kc), vc, mr, lr, ar)
    return ar[...] * (1.0 / lr[...])


def _diff_kernel(lambda_init, lam_ref, subln_ref, q_ref, k_ref, v_ref, o_ref, m_ref, l_ref, acc_ref):
    lo = _lane_iota((1, LANES)) < 64
    lam = lam_ref[...]
    lam_full = (jnp.exp(jnp.sum(lam[0:1] * lam[1:2], axis=-1, keepdims=True))
                - jnp.exp(jnp.sum(lam[2:3] * lam[3:4], axis=-1, keepdims=True)) + lambda_init)
    kc = k_ref[SEQ:SL, :]
    vc = v_ref[SEQ:SL, :]
    gain = subln_ref[...] * (1.0 - lambda_init)

    def q_body(i, carry):
        q0 = pl.multiple_of(i * DENSE_TQ, DENSE_TQ)
        qa, qb = _split_pair(q_ref[pl.ds(q0, DENSE_TQ), :], lo)
        qs = jnp.concatenate([qa, qb], axis=0)
        o = _dense_rows(qs, k_ref, v_ref, SEQ, kc, vc, m_ref, l_ref, acc_ref)
        od = o[:DENSE_TQ] - lam_full * o[DENSE_TQ:]
        od = od * lax.rsqrt(jnp.mean(od * od, axis=-1, keepdims=True) + EPS) * gain
        o_ref[pl.ds(q0, DENSE_TQ), :] = od.astype(BF16)
        return carry

    lax.fori_loop(0, SEQ // DENSE_TQ, q_body, 0)
    o_ref[SEQ:SL, :] = jnp.zeros((CTX, LANES), BF16)


def _diff_attention(q, k, v, lam, subln, lambda_init, b):
    slab = lambda: pl.BlockSpec((None, SL, LANES), lambda i, j: (i, 0, j))
    rows = 2 * DENSE_TQ
    return pl.pallas_call(
        functools.partial(_diff_kernel, lambda_init),
        grid=(b, DIFF_HEADS),
        in_specs=[pl.BlockSpec(lam.shape, lambda i, j: (0, 0)),
                  pl.BlockSpec((1, LANES), lambda i, j: (0, 0)),
                  slab(), slab(), slab()],
        out_specs=slab(),
        out_shape=jax.ShapeDtypeStruct(v.shape, BF16),
        scratch_shapes=[pltpu.VMEM((rows, 1), F32), pltpu.VMEM((rows, 1), F32), pltpu.VMEM((rows, LANES), F32)],
        compiler_params=_cparams("parallel", "parallel"),
        name="diff_attention",
    )(lam, subln.reshape(1, LANES), q, k, v)


def _mla_kernel(q_ref, k_ref, v_ref, o_ref, m_ref, l_ref, acc_ref):
    lo = _lane_iota((1, LANES)) < 64
    vc = v_ref[SEQ:SL, :]

    def two_heads(q, n_lat):
        outs = []
        for hd in range(2):
            sl = slice(hd * LANES, (hd + 1) * LANES)
            outs.append(_dense_rows(q[:, sl], k_ref.at[:, sl], v_ref, n_lat, k_ref[SEQ:SL, sl], vc,
                                    m_ref, l_ref, acc_ref))
        return jnp.where(lo, outs[0], outs[1])

    def q_body(i, carry):
        q0 = pl.multiple_of(i * DENSE_TQ, DENSE_TQ)
        o = two_heads(q_ref[pl.ds(q0, DENSE_TQ), :], SEQ)
        o_ref[pl.ds(q0, DENSE_TQ), :] = o.astype(BF16)
        return carry

    lax.fori_loop(0, SEQ // DENSE_TQ, q_body, 0)
    o_ref[SEQ:SL, :] = two_heads(q_ref[SEQ:SL, :], 0).astype(BF16)


def _mla_attention(q, k, v, b):
    slabq = lambda: pl.BlockSpec((None, SL, 2 * LANES), lambda i, j: (i, 0, j))
    slabv = lambda: pl.BlockSpec((None, SL, LANES), lambda i, j: (i, 0, j))
    return pl.pallas_call(
        _mla_kernel,
        grid=(b, MLA_HEADS // 2),
        in_specs=[slabq(), slabq(), slabv()],
        out_specs=slabv(),
        out_shape=jax.ShapeDtypeStruct(v.shape, BF16),
        scratch_shapes=[pltpu.VMEM((DENSE_TQ, 1), F32), pltpu.VMEM((DENSE_TQ, 1), F32),
                        pltpu.VMEM((DENSE_TQ, LANES), F32)],
        compiler_params=_cparams("parallel", "parallel"),
        name="mla_attention",
    )(q, k, v)


def _oproj_kernel(a_ref, x_ref, wo_ref, g1_ref, ng_ref, sc_ref, sh_ref, wr_hi_ref, wr_lo_ref, br_ref,
                  xo_ref, h_ref, route_ref):
    y = _dot(a_ref[...], wo_ref[...])
    x = x_ref[...] + g1_ref[...] * y
    xo_ref[...] = x
    h = _modulated(x, ng_ref[...], sc_ref[...], sh_ref[...])
    hi = h.astype(BF16)
    h_ref[...] = hi
    lo = (h - hi.astype(F32)).astype(BF16)
    z = _dot(hi, wr_hi_ref[...]) + _dot(lo, wr_hi_ref[...]) + _dot(hi, wr_lo_ref[...]) + br_ref[...]
    lane = _lane_iota(z.shape)
    lane_f = lane.astype(F32)
    big = jnp.float32(1e9)
    is_g = lane < N_GROUPS
    zg = jnp.where(is_g, z, NEG)
    gmax = jnp.max(zg, axis=-1, keepdims=True)
    gsum = jnp.sum(jnp.where(is_g, jnp.exp(zg - gmax), 0.0), axis=-1, keepdims=True)
    g_w = 1.0 / gsum
    g_idx = jnp.min(jnp.where(is_g & (zg == gmax), lane_f, big), axis=-1, keepdims=True)
    e_lo = N_GROUPS + EXPERTS_PER_GROUP * g_idx
    is_e = (lane_f >= e_lo) & (lane_f < e_lo + EXPERTS_PER_GROUP)
    ze = jnp.where(is_e, z, NEG)
    z1 = jnp.max(ze, axis=-1, keepdims=True)
    i1 = jnp.min(jnp.where(is_e & (ze == z1), lane_f, big), axis=-1, keepdims=True)
    ze2 = jnp.where(lane_f == i1, NEG, ze)
    z2 = jnp.max(ze2, axis=-1, keepdims=True)
    i2 = jnp.min(jnp.where(is_e & (ze2 == z2) & (lane_f != i1), lane_f, big), axis=-1, keepdims=True)
    r = jnp.exp(z2 - z1)
    gate1 = g_w / (1.0 + r)
    gate2 = g_w * r / (1.0 + r)
    route = jnp.where(lane == 0, gate1, jnp.where(lane == 1, gate2,
                      jnp.where(lane == 2, i1 - N_GROUPS, jnp.where(lane == 3, i2 - N_GROUPS, 0.0))))
    route_ref[...] = route


def _oproj_router(attn2, x2, mod3, wo_bf, ng, wr_hi, wr_lo, br):
    t, d = x2.shape
    n_in = attn2.shape[1]
    full = lambda a: pl.BlockSpec(a.shape, lambda i: (0,) * a.ndim)
    row = lambda w: pl.BlockSpec((TM, w), lambda i: (i, 0))
    return pl.pallas_call(
        _oproj_kernel,
        grid=(t // TM,),
        in_specs=[row(n_in), row(d), full(wo_bf), _mod_spec(2), pl.BlockSpec((1, d), lambda i: (0, 0)),
                  _mod_spec(4), _mod_spec(3), full(wr_hi), full(wr_lo), full(br)],
        out_specs=[row(d), row(d), row(LANES)],
        out_shape=[jax.ShapeDtypeStruct((t, d), F32), jax.ShapeDtypeStruct((t, d), BF16),
                   jax.ShapeDtypeStruct((t, LANES), F32)],
        compiler_params=_cparams("parallel"),
        name="oproj_router",
    )(attn2, x2, wo_bf, mod3, ng.reshape(1, d), mod3, mod3, wr_hi, wr_lo, br)


def _expert_kernel(be_ref, nu_ref, x_ref, wgu_ref, wd_ref, o_ref):
    i = pl.program_id(0)

    @pl.when(i < nu_ref[0])
    def _():
        gu = _dot(x_ref[...], wgu_ref[...])
        gte, up = gu[:, :D_EXPERT], gu[:, D_EXPERT:]
        a = (gte * jax.nn.sigmoid(gte) * up).astype(BF16)
        o_ref[...] = _dot(a, wd_ref[...]).astype(BF16)

    @pl.when(i >= nu_ref[0])
    def _():
        o_ref[...] = jnp.zeros_like(o_ref)


def _experts(xs, block_expert, n_used, wgu_bf, wd_bf):
    n_rows, d = xs.shape
    n_blocks = n_rows // MOE_BLK
    return pl.pallas_call(
        _expert_kernel,
        grid_spec=pltpu.PrefetchScalarGridSpec(
            num_scalar_prefetch=2,
            grid=(n_blocks,),
            in_specs=[pl.BlockSpec((MOE_BLK, d), lambda i, be, nu: (i, 0)),
                      pl.BlockSpec((None, d, 2 * D_EXPERT), lambda i, be, nu: (be[i], 0, 0)),
                      pl.BlockSpec((None, D_EXPERT, d), lambda i, be, nu: (be[i], 0, 0))],
            out_specs=pl.BlockSpec((MOE_BLK, d), lambda i, be, nu: (i, 0))),
        out_shape=jax.ShapeDtypeStruct((n_rows, d), BF16),
        compiler_params=_cparams("arbitrary"),
        name="experts",
    )(block_expert, n_used, xs, wgu_bf, wd_bf)


def _combine_kernel(x_ref, y0_ref, y1_ref, route_ref, g2_ref, o_ref):
    route = route_ref[...]
    y = route[:, 0:1] * y0_ref[...].astype(F32) + route[:, 1:2] * y1_ref[...].astype(F32)
    o_ref[...] = x_ref[...] + g2_ref[...] * y


def _combine(x2, y0, y1, route, mod3):
    t, d = x2.shape
    row = lambda w: pl.BlockSpec((TM, w), lambda i: (i, 0))
    return pl.pallas_call(
        _combine_kernel,
        grid=(t // TM,),
        in_specs=[row(d), row(d), row(d), row(LANES), _mod_spec(5)],
        out_specs=row(d),
        out_shape=jax.ShapeDtypeStruct((t, d), F32),
        compiler_params=_cparams("parallel"),
        name="moe_combine",
    )(x2, y0, y1, route, mod3)


def _moe_dispatch(route, t):
    expert = route[:, 2:4].astype(jnp.int32).reshape(-1)
    onehot = (expert[:, None] == jnp.arange(N_EXPERTS, dtype=jnp.int32)[None, :]).astype(jnp.int32)
    csum = jnp.cumsum(onehot, axis=0)
    rank = jnp.take_along_axis(csum, expert[:, None], axis=1)[:, 0] - 1
    counts = csum[-1]
    padded = (counts + MOE_BLK - 1) // MOE_BLK * MOE_BLK
    seg_end = jnp.cumsum(padded)
    seg_start = seg_end - padded
    pos = seg_start[expert] + rank
    n_rows = -(-(2 * t + N_EXPERTS * (MOE_BLK - 1)) // MOE_BLK) * MOE_BLK
    n_blocks = n_rows // MOE_BLK
    row_token = jnp.zeros((n_rows,), jnp.int32).at[pos].set(jnp.arange(2 * t, dtype=jnp.int32) // 2)
    block_expert = jnp.minimum(
        jnp.searchsorted(seg_end, jnp.arange(n_blocks, dtype=jnp.int32) * MOE_BLK, side="right"),
        N_EXPERTS - 1).astype(jnp.int32)
    n_used = (seg_end[-1] // MOE_BLK).astype(jnp.int32).reshape(1)
    return row_token, pos.reshape(t, 2), block_expert, n_used


def _moe(x2, h_bf, route, mod3, wgu_bf, wd_bf):
    t = x2.shape[0]
    row_token, pos, block_expert, n_used = _moe_dispatch(route, t)
    xs = jnp.take(h_bf, row_token, axis=0)
    ys = _experts(xs, block_expert, n_used, wgu_bf, wd_bf)
    y0 = jnp.take(ys, pos[:, 0], axis=0)
    y1 = jnp.take(ys, pos[:, 1], axis=0)
    return _combine(x2, y0, y1, route, mod3)


def kernel(x, c, ctx, c_ctx, ada_w, ada_b, norm_g, moe_w_router_group, moe_b_router_group, moe_w_router_expert, moe_b_router_expert, moe_w_gate_up, moe_w_down, na_w_qkv, na_w_o, na_q_norm, na_k_norm, na_rpb, sw_w_qkv, sw_w_o, sw_q_norm, sw_k_norm, sw_sink, mla_w_dqkv, mla_q_a_norm, mla_kv_a_norm, mla_w_uq, mla_w_ukv, mla_q_norm, mla_k_norm, mla_w_o, diff_w_qkv, diff_q_norm, diff_k_norm, diff_lambda, diff_subln, diff_w_o):
    slab = _stack(x, c, ctx, c_ctx, ada_w, ada_b, norm_g, moe_w_router_group, moe_b_router_group, moe_w_router_expert, moe_b_router_expert, moe_w_gate_up, moe_w_down, na_w_qkv, na_w_o, na_q_norm, na_k_norm, na_rpb, sw_w_qkv, sw_w_o, sw_q_norm, sw_k_norm, sw_sink, mla_w_dqkv, mla_q_a_norm, mla_kv_a_norm, mla_w_uq, mla_w_ukv, mla_q_norm, mla_k_norm, mla_w_o, diff_w_qkv, diff_q_norm, diff_k_norm, diff_lambda, diff_subln, diff_w_o)
    return slab[:, :SEQ]


def _stack(x, c, ctx, c_ctx, ada_w, ada_b, norm_g, moe_w_router_group, moe_b_router_group, moe_w_router_expert, moe_b_router_expert, moe_w_gate_up, moe_w_down, na_w_qkv, na_w_o, na_q_norm, na_k_norm, na_rpb, sw_w_qkv, sw_w_o, sw_q_norm, sw_k_norm, sw_sink, mla_w_dqkv, mla_q_a_norm, mla_kv_a_norm, mla_w_uq, mla_w_ukv, mla_q_norm, mla_k_norm, mla_w_o, diff_w_qkv, diff_q_norm, diff_k_norm, diff_lambda, diff_subln, diff_w_o):
    b, s, d = x.shape
    assert (s, d) == (SEQ, D_MODEL) and ctx.shape == (b, CTX, d) and b <= 8
    depth = ada_w.shape[0]
    assert depth <= 4
    t = b * SL

    cc = jnp.zeros((16, d), F32).at[:b].set(c).at[8].set(c_ctx)
    mod = _ada_table(cc, ada_w, ada_b)
    x2 = jnp.concatenate([x, ctx], axis=1).reshape(t, d)

    lane = np.arange(LANES)
    bd64 = _block_diag(lane // 64)
    lm64 = lane % 64
    rope64 = _rope_tables(64, lm64)
    scale64 = 64 ** -0.5 * LOG2E

    r3 = lambda a: a.reshape(b, SL, a.shape[-1])
    for l in range(depth):
        mod3 = mod[l].reshape(16 * 6, 1, d)
        if l == 0:
            w = na_w_qkv[0].astype(BF16)
            q, k, v = _project(x2, mod3, norm_g[l, 0], w, _qkv_plan(1024, 1024, 1024), (1024, 1024, 1024),
                               bd64, _head64_vec(na_q_norm[0], na_k_norm[0], scale64), name="na_proj")
            attn = _na_attention(r3(q), r3(k), r3(v), _na_bias_table(na_rpb[0]), b)
            w_o = na_w_o[0]
        elif l == 1:
            w = sw_w_qkv[0].astype(BF16)
            q, k, v = _project(x2, mod3, norm_g[l, 0], w, _qkv_plan(1024, 256, 256), (1024, 256, 256),
                               bd64, _head64_vec(sw_q_norm[0], sw_k_norm[0], scale64),
                               rope=rope64, rope_half=32, name="sw_proj")
            attn = _sw_attention(r3(q), r3(k), r3(v), sw_sink[0] * LOG2E, b)
            w_o = sw_w_o[0]
        elif l == 2:
            p = dict(w_dqkv=mla_w_dqkv[0], q_a_norm=mla_q_a_norm[0], kv_a_norm=mla_kv_a_norm[0],
                     w_uq=mla_w_uq[0], w_ukv=mla_w_ukv[0], q_norm=mla_q_norm[0], k_norm=mla_k_norm[0])
            q, k, v = _mla_project(x2, mod3, norm_g[l, 0], p)
            attn = _mla_attention(r3(q), r3(k), r3(v), b)
            w_o = mla_w_o[0]
        else:
            w = diff_w_qkv[0].astype(BF16)
            q, k, v = _project(x2, mod3, norm_g[l, 0], w, _qkv_plan(1024, 1024, 1024), (1024, 1024, 1024),
                               bd64, _head64_vec(diff_q_norm[0], diff_k_norm[0], scale64),
                               rope=rope64, rope_half=32, name="diff_proj")
            lambda_init = 0.8 - 0.6 * math.exp(-0.3 * l)
            attn = _diff_attention(r3(q), r3(k), r3(v), diff_lambda[0], diff_subln[0], lambda_init, b)
            w_o = diff_w_o[0]

        wr = jnp.concatenate([moe_w_router_group[l], moe_w_router_expert[l]], axis=1)
        wr = jnp.pad(wr, ((0, 0), (0, LANES - wr.shape[1])))
        wr_hi = wr.astype(BF16)
        wr_lo = (wr - wr_hi.astype(F32)).astype(BF16)
        br = jnp.concatenate([moe_b_router_group[l], moe_b_router_expert[l]])
        br = jnp.pad(br, (0, LANES - br.shape[0])).reshape(1, LANES)
        x2, h_bf, route = _oproj_router(attn.reshape(t, -1), x2, mod3, w_o.astype(BF16), norm_g[l, 1], wr_hi, wr_lo, br)
        x2 = _moe(x2, h_bf, route, mod3, moe_w_gate_up[l].astype(BF16), moe_w_down[l].astype(BF16))

    return x2.reshape(b, SL, d)
```

```python
import functools
import math

import jax
import jax.numpy as jnp
import numpy as np
from jax import lax
from jax.experimental import pallas as pl
from jax.experimental.pallas import tpu as pltpu

F32 = jnp.float32
BF16 = jnp.bfloat16

D_MODEL = 1024
SEQ = 4096
CTX = 256
SL = SEQ + CTX
GRID_W = 64
GRID_ROWS = SEQ // GRID_W
EPS = 1e-6
ROPE_BASE = 10000.0
NEG = -1e30
LOG2E = 1.4426950408889634

NA_KH, NA_KW = 8, 16
SW_WINDOW = 128
SW_HEADS, SW_KV_HEADS = 16, 4
MLA_Q_RANK, MLA_KV_RANK, MLA_NOPE, MLA_ROPE, MLA_V, MLA_HEADS = 384, 256, 64, 32, 64, 16
DIFF_HEADS = 8
N_GROUPS, EXPERTS_PER_GROUP, N_EXPERTS, D_EXPERT = 4, 8, 32, 256

LANES = 128
MXU_N = 256
TM = 256
TILES_PER_SLAB = SL // TM
MOE_BLK = 256
VMEM_LIMIT = 56 * 1024 * 1024


def _cparams(*sem):
    return pltpu.CompilerParams(dimension_semantics=sem, vmem_limit_bytes=VMEM_LIMIT)


def _lane_iota(shape):
    return lax.broadcasted_iota(jnp.int32, shape, len(shape) - 1)


def _dot(a, b):
    return jnp.dot(a, b, preferred_element_type=F32)


def _dot_nt(a, b):
    return lax.dot_general(a, b, (((1,), (1,)), ((), ())), preferred_element_type=F32)


def _ada_kernel(cc_ref, w_ref, b_ref, o_ref):
    a = cc_ref[...]
    a = a * jax.nn.sigmoid(a)
    hi = a.astype(BF16)
    lo = (a - hi.astype(F32)).astype(BF16)
    w = w_ref[...]
    whi = w.astype(BF16)
    wlo = (w - whi.astype(F32)).astype(BF16)
    o_ref[...] = _dot(hi, whi) + _dot(lo, whi) + _dot(hi, wlo) + b_ref[...]


def _ada_table(cc, ada_w, ada_b):
    depth, d, n = ada_w.shape
    tn = 1024
    return pl.pallas_call(
        _ada_kernel,
        grid=(depth, n // tn),
        in_specs=[pl.BlockSpec((16, d), lambda l, j: (0, 0)),
                  pl.BlockSpec((None, d, tn), lambda l, j: (l, 0, j)),
                  pl.BlockSpec((None, 1, tn), lambda l, j: (l, 0, j))],
        out_specs=pl.BlockSpec((None, 16, tn), lambda l, j: (l, 0, j)),
        out_shape=jax.ShapeDtypeStruct((depth, 16, n), F32),
        compiler_params=_cparams("parallel", "parallel"),
        name="ada_table",
    )(cc, ada_w, ada_b.reshape(depth, 1, n))


def _mod_spec(part):
    def index(i):
        m = jnp.where(i % TILES_PER_SLAB == TILES_PER_SLAB - 1, 8, i // TILES_PER_SLAB)
        return (m * 6 + part, 0, 0)
    return pl.BlockSpec((None, 1, D_MODEL), index)


def _modulated(x, g, sc, sh):
    ms = jnp.mean(x * x, axis=-1, keepdims=True)
    return (x * lax.rsqrt(ms + EPS)) * g * (1.0 + sc) + sh


def _group_norm_rope(y, bd, invcnt, gain, cos=None, sin=None, half=None):
    ss = _dot((y * y).astype(BF16), bd)
    yn = y * lax.rsqrt(ss * invcnt + EPS) * gain
    if cos is not None:
        lane = _lane_iota(yn.shape)
        fwd = pltpu.roll(yn, LANES - half, 1)
        bwd = pltpu.roll(yn, half, 1)
        partner = jnp.where((lane & (2 * half - 1)) < half, fwd, bwd)
        yn = yn * cos + partner * sin
    return yn


def _block_diag(group_of_lane):
    g = np.asarray(group_of_lane)
    return jnp.asarray((g[:, None] == g[None, :]).astype(np.float32), dtype=BF16)


def _rope_tables(rot_dim, lane_map):
    t = jnp.arange(SEQ, dtype=jnp.int32)
    n_freq = rot_dim // 4
    inv_freq = ROPE_BASE ** (-jnp.arange(n_freq, dtype=F32) / n_freq)
    rows = (t // GRID_W).astype(F32)
    cols = (t % GRID_W).astype(F32)
    ang = jnp.concatenate([rows[:, None] * inv_freq, cols[:, None] * inv_freq], axis=-1)
    cos, sin = jnp.cos(ang), jnp.sin(ang)
    lm = np.asarray(lane_map)
    rot = lm >= 0
    idx = np.where(rot, lm % (rot_dim // 2), 0)
    sign = np.where(lm < rot_dim // 2, -1.0, 1.0).astype(np.float32)
    cos_t = jnp.where(rot[None, :], cos[:, idx], 1.0)
    sin_t = jnp.where(rot[None, :], sin[:, idx] * sign[None, :], 0.0)
    ident_c = jnp.ones((CTX, LANES), F32)
    ident_s = jnp.zeros((CTX, LANES), F32)
    return jnp.concatenate([cos_t, ident_c], 0), jnp.concatenate([sin_t, ident_s], 0)


def _proj_kernel(plan, rope_half, x_ref, g_ref, sc_ref, sh_ref, w_ref, bd_ref, vec_ref, *rest):
    if rope_half is not None:
        cos_ref, sin_ref = rest[:2]
        outs = rest[2:]
        cos, sin = cos_ref[...], sin_ref[...]
    else:
        outs = rest
        cos = sin = None
    h = _modulated(x_ref[...], g_ref[...], sc_ref[...], sh_ref[...]).astype(BF16)
    bd = bd_ref[...]
    for c0, ep, out_idx, oc0 in plan:
        acc = _dot(h, w_ref[:, c0:c0 + MXU_N])
        for s in range(MXU_N // LANES):
            y = acc[:, s * LANES:(s + 1) * LANES]
            if ep is not None:
                y = _group_norm_rope(y, bd, vec_ref[ep, 0:1, :], vec_ref[ep, 1:2, :], cos, sin, rope_half)
            lo = oc0 + s * LANES
            outs[out_idx][:, lo:lo + LANES] = y.astype(BF16)


def _project(x2, mod3, g, w_bf, plan, out_widths, bd, vec, rope=None, rope_half=None, name="proj"):
    t, d = x2.shape
    n = w_bf.shape[1]
    n_tiles = t // TM
    in_specs = [pl.BlockSpec((TM, d), lambda i: (i, 0)),
                pl.BlockSpec((1, d), lambda i: (0, 0)),
                _mod_spec(1), _mod_spec(0),
                pl.BlockSpec((d, n), lambda i: (0, 0)),
                pl.BlockSpec(bd.shape, lambda i: (0, 0)),
                pl.BlockSpec(vec.shape, lambda i: (0, 0, 0))]
    args = [x2, g.reshape(1, d), mod3, mod3, w_bf, bd, vec]
    if rope is not None:
        in_specs += [pl.BlockSpec((TM, LANES), lambda i: (i % TILES_PER_SLAB, 0))] * 2
        args += list(rope)
    return pl.pallas_call(
        functools.partial(_proj_kernel, plan, rope_half),
        grid=(n_tiles,),
        in_specs=in_specs,
        out_specs=[pl.BlockSpec((TM, w), lambda i: (i, 0)) for w in out_widths],
        out_shape=[jax.ShapeDtypeStruct((t, w), BF16) for w in out_widths],
        compiler_params=_cparams("parallel"),
        name=name,
    )(*args)


def _qkv_plan(nq, nk, nv):
    plan = []
    for c0 in range(0, nq, MXU_N):
        plan.append((c0, 0, 0, c0))
    for c0 in range(0, nk, MXU_N):
        plan.append((nq + c0, 1, 1, c0))
    for c0 in range(0, nv, MXU_N):
        plan.append((nq + nk + c0, None, 2, c0))
    return tuple(plan)


def _head64_vec(q_norm, k_norm, q_scale):
    inv = jnp.full((LANES,), 1.0 / 64, F32)
    z = jnp.zeros((6, LANES), F32)
    vq = jnp.concatenate([inv[None], (jnp.tile(q_norm, 2) * q_scale)[None], z], 0)
    vk = jnp.concatenate([inv[None], jnp.tile(k_norm, 2)[None], z], 0)
    return jnp.stack([vq, vk], 0)


def _mla_proj_kernel(x_ref, g_ref, sc_ref, sh_ref, wd_ref, wuq_ref, wuk_ref, wuv_ref, bd_ref, vec_ref,
                     an_ref, cos_ref, sin_ref, q_out, k_out, v_out):
    h = _modulated(x_ref[...], g_ref[...], sc_ref[...], sh_ref[...]).astype(BF16)
    cos, sin = cos_ref[...], sin_ref[...]
    half = MLA_ROPE // 2
    cq = _dot(h, wd_ref[:, 0:MLA_Q_RANK])
    ckv = _dot(h, wd_ref[:, MLA_Q_RANK:MLA_Q_RANK + MLA_KV_RANK])
    kr = _dot(h, wd_ref[:, MLA_Q_RANK + MLA_KV_RANK:])
    cq = cq * lax.rsqrt(jnp.mean(cq * cq, axis=-1, keepdims=True) + EPS) * an_ref[0:1, 0:MLA_Q_RANK]
    ckv = ckv * lax.rsqrt(jnp.mean(ckv * ckv, axis=-1, keepdims=True) + EPS) * an_ref[1:2, 0:MLA_KV_RANK]
    cq = cq.astype(BF16)
    ckv = ckv.astype(BF16)
    kr = _group_norm_rope(kr, bd_ref[1], vec_ref[2, 0:1, :], vec_ref[2, 1:2, :], cos[:, LANES:], sin[:, LANES:], half)
    kr = pltpu.roll(kr, MLA_NOPE, 1)
    for hd in range(MLA_HEADS):
        c0 = hd * LANES
        qh = _dot(cq, wuq_ref[:, c0:c0 + LANES])
        qh = _group_norm_rope(qh, bd_ref[0], vec_ref[0, 0:1, :], vec_ref[0, 1:2, :], cos[:, :LANES], sin[:, :LANES], half)
        q_out[:, c0:c0 + LANES] = qh.astype(BF16)
        kh = _dot(ckv, wuk_ref[:, c0:c0 + LANES])
        kh = _group_norm_rope(kh, bd_ref[0], vec_ref[1, 0:1, :], vec_ref[1, 1:2, :]) + kr
        k_out[:, c0:c0 + LANES] = kh.astype(BF16)
    for c0 in range(0, MLA_HEADS * MLA_V, MXU_N):
        v_out[:, c0:c0 + MXU_N] = _dot(ckv, wuv_ref[:, c0:c0 + MXU_N]).astype(BF16)


def _mla_project(x2, mod3, g, p):
    t, d = x2.shape
    hq = MLA_NOPE + MLA_ROPE
    n_tiles = t // TM
    wd = jnp.pad(p["w_dqkv"], ((0, 0), (0, LANES - MLA_ROPE))).astype(BF16)
    wuq = p["w_uq"].reshape(MLA_Q_RANK, MLA_HEADS, hq)
    wuq = jnp.pad(wuq, ((0, 0), (0, 0), (0, LANES - hq))).reshape(MLA_Q_RANK, MLA_HEADS * LANES).astype(BF16)
    wukv = p["w_ukv"].reshape(MLA_KV_RANK, MLA_HEADS, MLA_NOPE + MLA_V)
    wuk = jnp.pad(wukv[:, :, :MLA_NOPE], ((0, 0), (0, 0), (0, LANES - MLA_NOPE)))
    wuk = wuk.reshape(MLA_KV_RANK, MLA_HEADS * LANES).astype(BF16)
    wuv = wukv[:, :, MLA_NOPE:].reshape(MLA_KV_RANK, MLA_HEADS * MLA_V).astype(BF16)
    scale = (MLA_NOPE + MLA_ROPE) ** -0.5 * LOG2E
    lane = np.arange(LANES)
    grp_head = np.where(lane < 64, 0, np.where(lane < 96, 1, 2))
    grp_kr = np.where(lane < 32, 0, 1)
    bd = jnp.stack([_block_diag(grp_head), _block_diag(grp_kr)], 0)
    inv_head = jnp.asarray(np.where(lane < 64, 1 / 64, 1 / 32), F32)
    zpad = jnp.zeros((LANES - hq,), F32)
    gq = jnp.concatenate([p["q_norm"], zpad]) * scale
    gk = jnp.concatenate([p["k_norm"][:MLA_NOPE], jnp.zeros((LANES - MLA_NOPE,), F32)])
    gkr = jnp.concatenate([p["k_norm"][MLA_NOPE:], jnp.zeros((LANES - MLA_ROPE,), F32)])
    inv_kr = jnp.asarray(np.where(lane < 32, 1 / 32, 1 / 96), F32)
    z6 = jnp.zeros((6, LANES), F32)
    vec = jnp.stack([jnp.concatenate([inv_head[None], gq[None], z6], 0),
                     jnp.concatenate([inv_head[None], gk[None], z6], 0),
                     jnp.concatenate([inv_kr[None], gkr[None], z6], 0)], 0)
    an = jnp.zeros((8, MLA_Q_RANK), F32)
    an = an.at[0].set(p["q_a_norm"]).at[1, :MLA_KV_RANK].set(p["kv_a_norm"])
    lm_head = np.where((lane >= 64) & (lane < 96), lane - 64, -1)
    lm_kr = np.where(lane < 32, lane, -1)
    ch, sh_ = _rope_tables(MLA_ROPE, lm_head)
    ck, sk = _rope_tables(MLA_ROPE, lm_kr)
    cos2 = jnp.concatenate([ch, ck], 1)
    sin2 = jnp.concatenate([sh_, sk], 1)

    full = lambda a: pl.BlockSpec(a.shape, lambda i: (0,) * a.ndim)
    rope_spec = pl.BlockSpec((TM, 2 * LANES), lambda i: (i % TILES_PER_SLAB, 0))
    out_w = (MLA_HEADS * LANES, MLA_HEADS * LANES, MLA_HEADS * MLA_V)
    return pl.pallas_call(
        _mla_proj_kernel,
        grid=(n_tiles,),
        in_specs=[pl.BlockSpec((TM, d), lambda i: (i, 0)), pl.BlockSpec((1, d), lambda i: (0, 0)),
                  _mod_spec(1), _mod_spec(0), full(wd), full(wuq), full(wuk), full(wuv), full(bd), full(vec),
                  full(an), rope_spec, rope_spec],
        out_specs=[pl.BlockSpec((TM, w), lambda i: (i, 0)) for w in out_w],
        out_shape=[jax.ShapeDtypeStruct((t, w), BF16) for w in out_w],
        compiler_params=_cparams("parallel"),
        name="mla_proj",
    )(x2, g.reshape(1, d), mod3, mod3, wd, wuq, wuk, wuv, bd, vec, an, cos2, sin2)


def _softmax_pv(s_list, v_list, sink=None):
    m = functools.reduce(jnp.maximum, [jnp.max(s, axis=-1, keepdims=True) for s in s_list])
    if sink is not None:
        m = jnp.maximum(m, sink)
    ps = [jnp.exp2(s - m) for s in s_list]
    l = functools.reduce(lambda a, b: a + b, [jnp.sum(p, axis=-1, keepdims=True) for p in ps])
    if sink is not None:
        l = l + jnp.exp2(sink - m)
    o = functools.reduce(lambda a, b: a + b, [_dot(p.astype(BF16), v) for p, v in zip(ps, v_list)])
    return o * (1.0 / l)


def _split_pair(x, lo):
    zero = jnp.zeros_like(x)
    return jnp.where(lo, x, zero), jnp.where(lo, zero, x)


def _na_kernel(q_ref, k_ref, v_ref, bias_ref, o_ref):
    lo = _lane_iota((1, LANES)) < 64
    kc = k_ref[SEQ:SL, :]
    vc = v_ref[SEQ:SL, :]
    nb = NA_KH * GRID_W

    def pair_attend(q, ks, vs, bias):
        n = q.shape[0]
        qa, qb = _split_pair(q, lo)
        qs = jnp.concatenate([qa, qb], axis=0)
        s_list = [_dot_nt(qs, k) for k in ks]
        if bias is not None:
            s_list[0] = s_list[0] + bias
        o = _softmax_pv(s_list, vs)
        return jnp.where(lo, o[:n], o[n:])

    def row_body(r, carry):
        row0 = jnp.clip(r - NA_KH // 2, 0, GRID_ROWS - NA_KH)
        q = q_ref[pl.ds(pl.multiple_of(r * GRID_W, GRID_W), GRID_W), :]
        ks = pl.multiple_of(row0 * GRID_W, GRID_W)
        kn = k_ref[pl.ds(ks, nb), :]
        vn = v_ref[pl.ds(ks, nb), :]
        o = pair_attend(q, [kn, kc], [vn, vc], bias_ref[r - row0])
        o_ref[pl.ds(pl.multiple_of(r * GRID_W, GRID_W), GRID_W), :] = o.astype(BF16)
        return carry

    lax.fori_loop(0, GRID_ROWS, row_body, 0)
    o_ref[SEQ:SL, :] = pair_attend(q_ref[SEQ:SL, :], [kc], [vc], None).astype(BF16)


def _na_bias_table(rpb):
    heads = rpb.shape[0]
    c = np.arange(NA_KH)[:, None, None, None]
    qc = np.arange(GRID_W)[None, :, None, None]
    kr = np.arange(NA_KH)[None, None, :, None]
    kc = np.arange(GRID_W)[None, None, None, :]
    col0 = np.clip(qc - NA_KW // 2, 0, GRID_W - NA_KW)
    valid = (kc >= col0) & (kc < col0 + NA_KW)
    row_off = kr - c + (NA_KH - 1)
    col_off = kc - qc + (NA_KW - 1)
    idx = np.where(valid, row_off * (2 * NA_KW - 1) + col_off, 0)
    idx = np.broadcast_to(idx, (NA_KH, GRID_W, NA_KH, GRID_W)).reshape(NA_KH, GRID_W, NA_KH * GRID_W)
    valid = np.broadcast_to(valid, (NA_KH, GRID_W, NA_KH, GRID_W)).reshape(NA_KH, GRID_W, NA_KH * GRID_W)
    tab = jnp.where(valid[None], rpb[:, idx] * LOG2E, NEG)
    tab = tab.reshape(heads // 2, 2, NA_KH, GRID_W, NA_KH * GRID_W).transpose(2, 0, 1, 3, 4)
    return tab.reshape(NA_KH, heads // 2, 2 * GRID_W, NA_KH * GRID_W)


def _na_attention(q, k, v, bias, b):
    hp = q.shape[-1] // LANES
    slab = lambda: pl.BlockSpec((None, SL, LANES), lambda i, j: (i, 0, j))
    return pl.pallas_call(
        _na_kernel,
        grid=(b, hp),
        in_specs=[slab(), slab(), slab(),
                  pl.BlockSpec((NA_KH, None, 2 * GRID_W, NA_KH * GRID_W), lambda i, j: (0, j, 0, 0))],
        out_specs=slab(),
        out_shape=jax.ShapeDtypeStruct(q.shape, BF16),
        compiler_params=_cparams("parallel", "parallel"),
        name="na_attention",
    )(q, k, v, bias)


Q_BLK = 128


def _sw_kernel(sink_ref, q_ref, k_ref, v_ref, o_ref):
    g = pl.program_id(1)
    lane = _lane_iota((1, LANES))
    lo = lane < 64
    first_half = (jnp.zeros((1, LANES), jnp.int32) + (g % 2)) == 0
    target = (lane // 64) == (g % 2)
    kc = k_ref[SEQ:SL, :]
    vc = v_ref[SEQ:SL, :]
    span = Q_BLK + 2 * SW_WINDOW
    sinks = [sink_ref[4 * g + t] for t in range(4)]

    def swap(x):
        return jnp.concatenate([x[:, 64:], x[:, :64]], axis=1)

    def stack_heads(q):
        parts = []
        for blk in range(2):
            qb = q[:, blk * LANES:(blk + 1) * LANES]
            qs = swap(qb)
            even = jnp.where(first_half, qb, qs)
            odd = jnp.where(first_half, qs, qb)
            zero = jnp.zeros_like(qb)
            parts += [jnp.where(target, even, zero), jnp.where(target, odd, zero)]
        return jnp.concatenate(parts, axis=0)

    def unstack_heads(o, n):
        blocks = []
        for blk in range(2):
            oe = o[(2 * blk) * n:(2 * blk + 1) * n]
            oo = o[(2 * blk + 1) * n:(2 * blk + 2) * n]
            oe = jnp.where(first_half, oe, pltpu.roll(oe, 64, 1))
            oo = jnp.where(first_half, pltpu.roll(oo, 64, 1), oo)
            blocks.append(jnp.where(lo, oe, oo))
        return jnp.concatenate(blocks, axis=1)

    def sink_col(n):
        row = lax.broadcasted_iota(jnp.int32, (4 * n, 1), 0)
        return jnp.where(row < n, sinks[0], jnp.where(row < 2 * n, sinks[1],
                         jnp.where(row < 3 * n, sinks[2], sinks[3])))

    def block_body(i, carry):
        q0 = pl.multiple_of(i * Q_BLK, Q_BLK)
        start = pl.multiple_of(jnp.clip((i - 1) * Q_BLK, 0, SEQ - span), Q_BLK)
        qs = stack_heads(q_ref[pl.ds(q0, Q_BLK), :])
        kw = k_ref[pl.ds(start, span), :]
        vw = v_ref[pl.ds(start, span), :]
        s_loc = _dot_nt(qs, kw)
        qpos = q0 + (lax.broadcasted_iota(jnp.int32, (4 * Q_BLK, 1), 0) & (Q_BLK - 1))
        kpos = start + _lane_iota((1, span))
        s_loc = jnp.where(jnp.abs(kpos - qpos) <= SW_WINDOW, s_loc, NEG)
        s_ctx = _dot_nt(qs, kc)
        o = _softmax_pv([s_loc, s_ctx], [vw, vc], sink_col(Q_BLK))
        o_ref[pl.ds(q0, Q_BLK), :] = unstack_heads(o, Q_BLK).astype(BF16)
        return carry

    lax.fori_loop(0, SEQ // Q_BLK, block_body, 0)
    qs = stack_heads(q_ref[SEQ:SL, :])
    o = _softmax_pv([_dot_nt(qs, kc)], [vc], sink_col(CTX))
    o_ref[SEQ:SL, :] = unstack_heads(o, CTX).astype(BF16)


def _sw_attention(q, k, v, sink2, b):
    nq = 4 * 64
    slabq = lambda: pl.BlockSpec((None, SL, nq), lambda i, g: (i, 0, g))
    slabk = lambda: pl.BlockSpec((None, SL, LANES), lambda i, g: (i, 0, g // 2))
    return pl.pallas_call(
        _sw_kernel,
        grid=(b, SW_KV_HEADS),
        in_specs=[pl.BlockSpec(memory_space=pltpu.SMEM), slabq(), slabk(), slabk()],
        out_specs=slabq(),
        out_shape=jax.ShapeDtypeStruct(q.shape, BF16),
        compiler_params=_cparams("parallel", "parallel"),
        name="sw_attention",
    )(sink2, q, k, v)


DENSE_TQ = 256
DENSE_TK = 512


def _online_step(s, v, m_ref, l_ref, acc_ref):
    chunks = s.shape[1] // LANES
    m_prev = m_ref[...]
    m_new = jnp.maximum(m_prev, jnp.max(s, axis=-1, keepdims=True))
    alpha = jnp.exp2(m_prev - m_new)
    p = jnp.exp2(s - jnp.concatenate([m_new] * chunks, axis=1))
    psum = functools.reduce(lambda a, b: a + b, [p[:, c * LANES:(c + 1) * LANES] for c in range(chunks)])
    l_ref[...] = alpha * l_ref[...] + psum
    acc_ref[...] = alpha * acc_ref[...] + _dot(p.astype(BF16), v)
    m_ref[...] = m_new


def _dense_rows(qs, k_ref, v_ref, n_lat, kc, vc, m_ref, l_ref, acc_ref):
    rows = qs.shape[0]
    m_ref[0:rows, :] = jnp.full((rows, LANES), NEG, F32)
    l_ref[0:rows, :] = jnp.zeros((rows, LANES), F32)
    acc_ref[0:rows, :] = jnp.zeros((rows, LANES), F32)
    mr, lr, ar = m_ref.at[0:rows, :], l_ref.at[0:rows, :], acc_ref.at[0:rows, :]

    def kv_body(c, carry):
        k0 = pl.multiple_of(c * DENSE_TK, DENSE_TK)
        _online_step(_dot_nt(qs, k_ref[pl.ds(k0, DENSE_TK), :]), v_ref[pl.ds(k0, DENSE_TK), :], mr, lr, ar)
        return carry

    if n_lat:
        lax.fori_loop(0, n_lat // DENSE_TK, kv_body, 0, unroll=2)
    _online_step(_dot_nt(qs, kc), vc, mr, lr, ar)
    return ar[...] * (1.0 / jnp.sum(lr[...], axis=-1, keepdims=True))


def _diff_kernel(lambda_init, lam_ref, subln_ref, q_ref, k_ref, v_ref, o_ref, m_ref, l_ref, acc_ref):
    lo = _lane_iota((1, LANES)) < 64
    lam = lam_ref[...]
    lam_full = (jnp.exp(jnp.sum(lam[0:1] * lam[1:2], axis=-1, keepdims=True))
                - jnp.exp(jnp.sum(lam[2:3] * lam[3:4], axis=-1, keepdims=True)) + lambda_init)
    kc = k_ref[SEQ:SL, :]
    vc = v_ref[SEQ:SL, :]
    gain = subln_ref[...] * (1.0 - lambda_init)

    def q_body(i, carry):
        q0 = pl.multiple_of(i * DENSE_TQ, DENSE_TQ)
        qa, qb = _split_pair(q_ref[pl.ds(q0, DENSE_TQ), :], lo)
        qs = jnp.concatenate([qa, qb], axis=0)
        o = _dense_rows(qs, k_ref, v_ref, SEQ, kc, vc, m_ref, l_ref, acc_ref)
        od = o[:DENSE_TQ] - lam_full * o[DENSE_TQ:]
        od = od * lax.rsqrt(jnp.mean(od * od, axis=-1, keepdims=True) + EPS) * gain
        o_ref[pl.ds(q0, DENSE_TQ), :] = od.astype(BF16)
        return carry

    lax.fori_loop(0, SEQ // DENSE_TQ, q_body, 0)
    o_ref[SEQ:SL, :] = jnp.zeros((CTX, LANES), BF16)


def _diff_attention(q, k, v, lam, subln, lambda_init, b):
    slab = lambda: pl.BlockSpec((None, SL, LANES), lambda i, j: (i, 0, j))
    rows = 2 * DENSE_TQ
    return pl.pallas_call(
        functools.partial(_diff_kernel, lambda_init),
        grid=(b, DIFF_HEADS),
        in_specs=[pl.BlockSpec(lam.shape, lambda i, j: (0, 0)),
                  pl.BlockSpec((1, LANES), lambda i, j: (0, 0)),
                  slab(), slab(), slab()],
        out_specs=slab(),
        out_shape=jax.ShapeDtypeStruct(v.shape, BF16),
        scratch_shapes=[pltpu.VMEM((rows, LANES), F32)] * 3,
        compiler_params=_cparams("parallel", "parallel"),
        name="diff_attention",
    )(lam, subln.reshape(1, LANES), q, k, v)


def _mla_kernel(q_ref, k_ref, v_ref, o_ref, m_ref, l_ref, acc_ref):
    lo = _lane_iota((1, LANES)) < 64
    vc = v_ref[SEQ:SL, :]

    def two_heads(q, n_lat):
        outs = []
        for hd in range(2):
            sl = slice(hd * LANES, (hd + 1) * LANES)
            outs.append(_dense_rows(q[:, sl], k_ref.at[:, sl], v_ref, n_lat, k_ref[SEQ:SL, sl], vc,
                                    m_ref, l_ref, acc_ref))
        return jnp.where(lo, outs[0], outs[1])

    def q_body(i, carry):
        q0 = pl.multiple_of(i * DENSE_TQ, DENSE_TQ)
        o = two_heads(q_ref[pl.ds(q0, DENSE_TQ), :], SEQ)
        o_ref[pl.ds(q0, DENSE_TQ), :] = o.astype(BF16)
        return carry

    lax.fori_loop(0, SEQ // DENSE_TQ, q_body, 0)
    o_ref[SEQ:SL, :] = two_heads(q_ref[SEQ:SL, :], 0).astype(BF16)


def _mla_attention(q, k, v, b):
    slabq = lambda: pl.BlockSpec((None, SL, 2 * LANES), lambda i, j: (i, 0, j))
    slabv = lambda: pl.BlockSpec((None, SL, LANES), lambda i, j: (i, 0, j))
    return pl.pallas_call(
        _mla_kernel,
        grid=(b, MLA_HEADS // 2),
        in_specs=[slabq(), slabq(), slabv()],
        out_specs=slabv(),
        out_shape=jax.ShapeDtypeStruct(v.shape, BF16),
        scratch_shapes=[pltpu.VMEM((DENSE_TQ, LANES), F32)] * 3,
        compiler_params=_cparams("parallel", "parallel"),
        name="mla_attention",
    )(q, k, v)


def _oproj_kernel(a_ref, x_ref, wo_ref, g1_ref, ng_ref, sc_ref, sh_ref, wr_hi_ref, wr_lo_ref, br_ref,
                  xo_ref, h_ref, route_ref):
    y = _dot(a_ref[...], wo_ref[...])
    x = x_ref[...] + g1_ref[...] * y
    xo_ref[...] = x
    h = _modulated(x, ng_ref[...], sc_ref[...], sh_ref[...])
    hi = h.astype(BF16)
    h_ref[...] = hi
    lo = (h - hi.astype(F32)).astype(BF16)
    z = _dot(hi, wr_hi_ref[...]) + _dot(lo, wr_hi_ref[...]) + _dot(hi, wr_lo_ref[...]) + br_ref[...]
    lane = _lane_iota(z.shape)
    lane_f = lane.astype(F32)
    big = jnp.float32(1e9)
    is_g = lane < N_GROUPS
    zg = jnp.where(is_g, z, NEG)
    gmax = jnp.max(zg, axis=-1, keepdims=True)
    gsum = jnp.sum(jnp.where(is_g, jnp.exp(zg - gmax), 0.0), axis=-1, keepdims=True)
    g_w = 1.0 / gsum
    g_idx = jnp.min(jnp.where(is_g & (zg == gmax), lane_f, big), axis=-1, keepdims=True)
    e_lo = N_GROUPS + EXPERTS_PER_GROUP * g_idx
    is_e = (lane_f >= e_lo) & (lane_f < e_lo + EXPERTS_PER_GROUP)
    ze = jnp.where(is_e, z, NEG)
    z1 = jnp.max(ze, axis=-1, keepdims=True)
    i1 = jnp.min(jnp.where(is_e & (ze == z1), lane_f, big), axis=-1, keepdims=True)
    ze2 = jnp.where(lane_f == i1, NEG, ze)
    z2 = jnp.max(ze2, axis=-1, keepdims=True)
    i2 = jnp.min(jnp.where(is_e & (ze2 == z2) & (lane_f != i1), lane_f, big), axis=-1, keepdims=True)
    r = jnp.exp(z2 - z1)
    gate1 = g_w / (1.0 + r)
    gate2 = g_w * r / (1.0 + r)
    route = jnp.where(lane == 0, gate1, jnp.where(lane == 1, gate2,
                      jnp.where(lane == 2, i1 - N_GROUPS, jnp.where(lane == 3, i2 - N_GROUPS, 0.0))))
    route_ref[...] = route


def _oproj_router(attn2, x2, mod3, wo_bf, ng, wr_hi, wr_lo, br):
    t, d = x2.shape
    n_in = attn2.shape[1]
    full = lambda a: pl.BlockSpec(a.shape, lambda i: (0,) * a.ndim)
    row = lambda w: pl.BlockSpec((TM, w), lambda i: (i, 0))
    return pl.pallas_call(
        _oproj_kernel,
        grid=(t // TM,),
        in_specs=[row(n_in), row(d), full(wo_bf), _mod_spec(2), pl.BlockSpec((1, d), lambda i: (0, 0)),
                  _mod_spec(4), _mod_spec(3), full(wr_hi), full(wr_lo), full(br)],
        out_specs=[row(d), row(d), row(LANES)],
        out_shape=[jax.ShapeDtypeStruct((t, d), F32), jax.ShapeDtypeStruct((t, d), BF16),
                   jax.ShapeDtypeStruct((t, LANES), F32)],
        compiler_params=_cparams("parallel"),
        name="oproj_router",
    )(attn2, x2, wo_bf, mod3, ng.reshape(1, d), mod3, mod3, wr_hi, wr_lo, br)


def _expert_kernel(be_ref, nu_ref, x_ref, wgu_ref, wd_ref, o_ref):
    i = pl.program_id(0)

    @pl.when(i < nu_ref[0])
    def _():
        gu = _dot(x_ref[...], wgu_ref[...])
        gte, up = gu[:, :D_EXPERT], gu[:, D_EXPERT:]
        a = (gte * jax.nn.sigmoid(gte) * up).astype(BF16)
        o_ref[...] = _dot(a, wd_ref[...]).astype(BF16)

    @pl.when(i >= nu_ref[0])
    def _():
        o_ref[...] = jnp.zeros_like(o_ref)


def _experts(xs, block_expert, n_used, wgu_bf, wd_bf):
    n_rows, d = xs.shape
    n_blocks = n_rows // MOE_BLK
    return pl.pallas_call(
        _expert_kernel,
        grid_spec=pltpu.PrefetchScalarGridSpec(
            num_scalar_prefetch=2,
            grid=(n_blocks,),
            in_specs=[pl.BlockSpec((MOE_BLK, d), lambda i, be, nu: (i, 0)),
                      pl.BlockSpec((None, d, 2 * D_EXPERT), lambda i, be, nu: (be[i], 0, 0)),
                      pl.BlockSpec((None, D_EXPERT, d), lambda i, be, nu: (be[i], 0, 0))],
            out_specs=pl.BlockSpec((MOE_BLK, d), lambda i, be, nu: (i, 0))),
        out_shape=jax.ShapeDtypeStruct((n_rows, d), BF16),
        compiler_params=_cparams("arbitrary"),
        name="experts",
    )(block_expert, n_used, xs, wgu_bf, wd_bf)


def _combine_kernel(x_ref, y0_ref, y1_ref, route_ref, g2_ref, o_ref):
    route = route_ref[...]
    y = route[:, 0:1] * y0_ref[...].astype(F32) + route[:, 1:2] * y1_ref[...].astype(F32)
    o_ref[...] = x_ref[...] + g2_ref[...] * y


def _combine(x2, y0, y1, route, mod3):
    t, d = x2.shape
    row = lambda w: pl.BlockSpec((TM, w), lambda i: (i, 0))
    return pl.pallas_call(
        _combine_kernel,
        grid=(t // TM,),
        in_specs=[row(d), row(d), row(d), row(LANES), _mod_spec(5)],
        out_specs=row(d),
        out_shape=jax.ShapeDtypeStruct((t, d), F32),
        compiler_params=_cparams("parallel"),
        name="moe_combine",
    )(x2, y0, y1, route, mod3)


def _moe_dispatch(route, t):
    expert = route[:, 2:4].astype(jnp.int32).reshape(-1)
    onehot = (expert[:, None] == jnp.arange(N_EXPERTS, dtype=jnp.int32)[None, :]).astype(jnp.int32)
    csum = jnp.cumsum(onehot, axis=0)
    rank = jnp.take_along_axis(csum, expert[:, None], axis=1)[:, 0] - 1
    counts = csum[-1]
    padded = (counts + MOE_BLK - 1) // MOE_BLK * MOE_BLK
    seg_end = jnp.cumsum(padded)
    seg_start = seg_end - padded
    pos = seg_start[expert] + rank
    n_rows = -(-(2 * t + N_EXPERTS * (MOE_BLK - 1)) // MOE_BLK) * MOE_BLK
    n_blocks = n_rows // MOE_BLK
    row_token = jnp.zeros((n_rows,), jnp.int32).at[pos].set(jnp.arange(2 * t, dtype=jnp.int32) // 2)
    block_expert = jnp.minimum(
        jnp.searchsorted(seg_end, jnp.arange(n_blocks, dtype=jnp.int32) * MOE_BLK, side="right"),
        N_EXPERTS - 1).astype(jnp.int32)
    n_used = (seg_end[-1] // MOE_BLK).astype(jnp.int32).reshape(1)
    return row_token, pos.reshape(t, 2), block_expert, n_used


def _moe(x2, h_bf, route, mod3, wgu_bf, wd_bf):
    t = x2.shape[0]
    row_token, pos, block_expert, n_used = _moe_dispatch(route, t)
    xs = jnp.take(h_bf, row_token, axis=0)
    ys = _experts(xs, block_expert, n_used, wgu_bf, wd_bf)
    y0 = jnp.take(ys, pos[:, 0], axis=0)
    y1 = jnp.take(ys, pos[:, 1], axis=0)
    return _combine(x2, y0, y1, route, mod3)


def kernel(x, c, ctx, c_ctx, ada_w, ada_b, norm_g, moe_w_router_group, moe_b_router_group, moe_w_router_expert, moe_b_router_expert, moe_w_gate_up, moe_w_down, na_w_qkv, na_w_o, na_q_norm, na_k_norm, na_rpb, sw_w_qkv, sw_w_o, sw_q_norm, sw_k_norm, sw_sink, mla_w_dqkv, mla_q_a_norm, mla_kv_a_norm, mla_w_uq, mla_w_ukv, mla_q_norm, mla_k_norm, mla_w_o, diff_w_qkv, diff_q_norm, diff_k_norm, diff_lambda, diff_subln, diff_w_o):
    slab = _stack(x, c, ctx, c_ctx, ada_w, ada_b, norm_g, moe_w_router_group, moe_b_router_group, moe_w_router_expert, moe_b_router_expert, moe_w_gate_up, moe_w_down, na_w_qkv, na_w_o, na_q_norm, na_k_norm, na_rpb, sw_w_qkv, sw_w_o, sw_q_norm, sw_k_norm, sw_sink, mla_w_dqkv, mla_q_a_norm, mla_kv_a_norm, mla_w_uq, mla_w_ukv, mla_q_norm, mla_k_norm, mla_w_o, diff_w_qkv, diff_q_norm, diff_k_norm, diff_lambda, diff_subln, diff_w_o)
    return slab[:, :SEQ]


def _stack(x, c, ctx, c_ctx, ada_w, ada_b, norm_g, moe_w_router_group, moe_b_router_group, moe_w_router_expert, moe_b_router_expert, moe_w_gate_up, moe_w_down, na_w_qkv, na_w_o, na_q_norm, na_k_norm, na_rpb, sw_w_qkv, sw_w_o, sw_q_norm, sw_k_norm, sw_sink, mla_w_dqkv, mla_q_a_norm, mla_kv_a_norm, mla_w_uq, mla_w_ukv, mla_q_norm, mla_k_norm, mla_w_o, diff_w_qkv, diff_q_norm, diff_k_norm, diff_lambda, diff_subln, diff_w_o):
    b, s, d = x.shape
    assert (s, d) == (SEQ, D_MODEL) and ctx.shape == (b, CTX, d) and b <= 8
    depth = ada_w.shape[0]
    assert depth <= 4
    t = b * SL

    cc = jnp.zeros((16, d), F32).at[:b].set(c).at[8].set(c_ctx)
    mod = _ada_table(cc, ada_w, ada_b)
    x2 = jnp.concatenate([x, ctx], axis=1).reshape(t, d)

    lane = np.arange(LANES)
    bd64 = _block_diag(lane // 64)
    lm64 = lane % 64
    rope64 = _rope_tables(64, lm64)
    scale64 = 64 ** -0.5 * LOG2E

    r3 = lambda a: a.reshape(b, SL, a.shape[-1])
    for l in range(depth):
        mod3 = mod[l].reshape(16 * 6, 1, d)
        if l == 0:
            w = na_w_qkv[0].astype(BF16)
            q, k, v = _project(x2, mod3, norm_g[l, 0], w, _qkv_plan(1024, 1024, 1024), (1024, 1024, 1024),
                               bd64, _head64_vec(na_q_norm[0], na_k_norm[0], scale64), name="na_proj")
            attn = _na_attention(r3(q), r3(k), r3(v), _na_bias_table(na_rpb[0]), b)
            w_o = na_w_o[0]
        elif l == 1:
            w = sw_w_qkv[0].astype(BF16)
            q, k, v = _project(x2, mod3, norm_g[l, 0], w, _qkv_plan(1024, 256, 256), (1024, 256, 256),
                               bd64, _head64_vec(sw_q_norm[0], sw_k_norm[0], scale64),
                               rope=rope64, rope_half=32, name="sw_proj")
            attn = _sw_attention(r3(q), r3(k), r3(v), sw_sink[0] * LOG2E, b)
            w_o = sw_w_o[0]
        elif l == 2:
            p = dict(w_dqkv=mla_w_dqkv[0], q_a_norm=mla_q_a_norm[0], kv_a_norm=mla_kv_a_norm[0],
                     w_uq=mla_w_uq[0], w_ukv=mla_w_ukv[0], q_norm=mla_q_norm[0], k_norm=mla_k_norm[0])
            q, k, v = _mla_project(x2, mod3, norm_g[l, 0], p)
            attn = _mla_attention(r3(q), r3(k), r3(v), b)
            w_o = mla_w_o[0]
        else:
            w = diff_w_qkv[0].astype(BF16)
            q, k, v = _project(x2, mod3, norm_g[l, 0], w, _qkv_plan(1024, 1024, 1024), (1024, 1024, 1024),
                               bd64, _head64_vec(diff_q_norm[0], diff_k_norm[0], scale64),
                               rope=rope64, rope_half=32, name="diff_proj")
            lambda_init = 0.8 - 0.6 * math.exp(-0.3 * l)
            attn = _diff_attention(r3(q), r3(k), r3(v), diff_lambda[0], diff_subln[0], lambda_init, b)
            w_o = diff_w_o[0]

        wr = jnp.concatenate([moe_w_router_group[l], moe_w_router_expert[l]], axis=1)
        wr = jnp.pad(wr, ((0, 0), (0, LANES - wr.shape[1])))
        wr_hi = wr.astype(BF16)
        wr_lo = (wr - wr_hi.astype(F32)).astype(BF16)
        br = jnp.concatenate([moe_b_router_group[l], moe_b_router_expert[l]])
        br = jnp.pad(br, (0, LANES - br.shape[0])).reshape(1, LANES)
        x2, h_bf, route = _oproj_router(attn.reshape(t, -1), x2, mod3, w_o.astype(BF16), norm_g[l, 1], wr_hi, wr_lo, br)
        x2 = _moe(x2, h_bf, route, mod3, moe_w_gate_up[l].astype(BF16), moe_w_down[l].astype(BF16))

    return x2.reshape(b, SL, d)
```

```python
import functools
import math

import jax
import jax.numpy as jnp
import numpy as np
from jax import lax
from jax.experimental import pallas as pl
from jax.experimental.pallas import tpu as pltpu

F32 = jnp.float32
BF16 = jnp.bfloat16

D_MODEL = 1024
SEQ = 4096
CTX = 256
SL = SEQ + CTX
GRID_W = 64
GRID_ROWS = SEQ // GRID_W
EPS = 1e-6
ROPE_BASE = 10000.0
NEG = -1e30
LOG2E = 1.4426950408889634

NA_KH, NA_KW = 8, 16
SW_WINDOW = 128
SW_HEADS, SW_KV_HEADS = 16, 4
MLA_Q_RANK, MLA_KV_RANK, MLA_NOPE, MLA_ROPE, MLA_V, MLA_HEADS = 384, 256, 64, 32, 64, 16
DIFF_HEADS = 8
N_GROUPS, EXPERTS_PER_GROUP, N_EXPERTS, D_EXPERT = 4, 8, 32, 256

LANES = 128
MXU_N = 256
TM = 256
TILES_PER_SLAB = SL // TM
MOE_BLK = 256
VMEM_LIMIT = 56 * 1024 * 1024


def _cparams(*sem):
    return pltpu.CompilerParams(dimension_semantics=sem, vmem_limit_bytes=VMEM_LIMIT)


def _lane_iota(shape):
    return lax.broadcasted_iota(jnp.int32, shape, len(shape) - 1)


def _dot(a, b):
    return jnp.dot(a, b, preferred_element_type=F32)


def _dot_nt(a, b):
    return lax.dot_general(a, b, (((1,), (1,)), ((), ())), preferred_element_type=F32)


def _ada_kernel(cc_ref, w_ref, b_ref, o_ref):
    a = cc_ref[...]
    a = a * jax.nn.sigmoid(a)
    hi = a.astype(BF16)
    lo = (a - hi.astype(F32)).astype(BF16)
    w = w_ref[...]
    whi = w.astype(BF16)
    wlo = (w - whi.astype(F32)).astype(BF16)
    o_ref[...] = _dot(hi, whi) + _dot(lo, whi) + _dot(hi, wlo) + b_ref[...]


def _ada_table(cc, ada_w, ada_b):
    depth, d, n = ada_w.shape
    tn = 1024
    return pl.pallas_call(
        _ada_kernel,
        grid=(depth, n // tn),
        in_specs=[pl.BlockSpec((16, d), lambda l, j: (0, 0)),
                  pl.BlockSpec((None, d, tn), lambda l, j: (l, 0, j)),
                  pl.BlockSpec((None, 1, tn), lambda l, j: (l, 0, j))],
        out_specs=pl.BlockSpec((None, 16, tn), lambda l, j: (l, 0, j)),
        out_shape=jax.ShapeDtypeStruct((depth, 16, n), F32),
        compiler_params=_cparams("parallel", "parallel"),
        name="ada_table",
    )(cc, ada_w, ada_b.reshape(depth, 1, n))


def _mod_spec(part):
    def index(i):
        m = jnp.where(i % TILES_PER_SLAB == TILES_PER_SLAB - 1, 8, i // TILES_PER_SLAB)
        return (m * 6 + part, 0, 0)
    return pl.BlockSpec((None, 1, D_MODEL), index)


def _modulated(x, g, sc, sh):
    ms = jnp.mean(x * x, axis=-1, keepdims=True)
    return (x * lax.rsqrt(ms + EPS)) * g * (1.0 + sc) + sh


def _group_norm_rope(y, bd, invcnt, gain, cos=None, sin=None, half=None):
    ss = _dot((y * y).astype(BF16), bd)
    yn = y * lax.rsqrt(ss * invcnt + EPS) * gain
    if cos is not None:
        lane = _lane_iota(yn.shape)
        fwd = pltpu.roll(yn, LANES - half, 1)
        bwd = pltpu.roll(yn, half, 1)
        partner = jnp.where((lane & (2 * half - 1)) < half, fwd, bwd)
        yn = yn * cos + partner * sin
    return yn


def _block_diag(group_of_lane):
    g = np.asarray(group_of_lane)
    return jnp.asarray((g[:, None] == g[None, :]).astype(np.float32), dtype=BF16)


def _rope_tables(rot_dim, lane_map):
    t = jnp.arange(SEQ, dtype=jnp.int32)
    n_freq = rot_dim // 4
    inv_freq = ROPE_BASE ** (-jnp.arange(n_freq, dtype=F32) / n_freq)
    rows = (t // GRID_W).astype(F32)
    cols = (t % GRID_W).astype(F32)
    ang = jnp.concatenate([rows[:, None] * inv_freq, cols[:, None] * inv_freq], axis=-1)
    cos, sin = jnp.cos(ang), jnp.sin(ang)
    lm = np.asarray(lane_map)
    rot = lm >= 0
    idx = np.where(rot, lm % (rot_dim // 2), 0)
    sign = np.where(lm < rot_dim // 2, -1.0, 1.0).astype(np.float32)
    cos_t = jnp.where(rot[None, :], cos[:, idx], 1.0)
    sin_t = jnp.where(rot[None, :], sin[:, idx] * sign[None, :], 0.0)
    ident_c = jnp.ones((CTX, LANES), F32)
    ident_s = jnp.zeros((CTX, LANES), F32)
    return jnp.concatenate([cos_t, ident_c], 0), jnp.concatenate([sin_t, ident_s], 0)


def _proj_kernel(plan, rope_half, x_ref, g_ref, sc_ref, sh_ref, w_ref, bd_ref, vec_ref, *rest):
    if rope_half is not None:
        cos_ref, sin_ref = rest[:2]
        outs = rest[2:]
        cos, sin = cos_ref[...], sin_ref[...]
    else:
        outs = rest
        cos = sin = None
    h = _modulated(x_ref[...], g_ref[...], sc_ref[...], sh_ref[...]).astype(BF16)
    bd = bd_ref[...]
    for c0, ep, out_idx, oc0 in plan:
        acc = _dot(h, w_ref[:, c0:c0 + MXU_N])
        for s in range(MXU_N // LANES):
            y = acc[:, s * LANES:(s + 1) * LANES]
            if ep is not None:
                y = _group_norm_rope(y, bd, vec_ref[ep, 0:1, :], vec_ref[ep, 1:2, :], cos, sin, rope_half)
            lo = oc0 + s * LANES
            outs[out_idx][:, lo:lo + LANES] = y.astype(BF16)


def _project(x2, mod3, g, w_bf, plan, out_widths, bd, vec, rope=None, rope_half=None, name="proj"):
    t, d = x2.shape
    n = w_bf.shape[1]
    n_tiles = t // TM
    in_specs = [pl.BlockSpec((TM, d), lambda i: (i, 0)),
                pl.BlockSpec((1, d), lambda i: (0, 0)),
                _mod_spec(1), _mod_spec(0),
                pl.BlockSpec((d, n), lambda i: (0, 0)),
                pl.BlockSpec(bd.shape, lambda i: (0, 0)),
                pl.BlockSpec(vec.shape, lambda i: (0, 0, 0))]
    args = [x2, g.reshape(1, d), mod3, mod3, w_bf, bd, vec]
    if rope is not None:
        in_specs += [pl.BlockSpec((TM, LANES), lambda i: (i % TILES_PER_SLAB, 0))] * 2
        args += list(rope)
    return pl.pallas_call(
        functools.partial(_proj_kernel, plan, rope_half),
        grid=(n_tiles,),
        in_specs=in_specs,
        out_specs=[pl.BlockSpec((TM, w), lambda i: (i, 0)) for w in out_widths],
        out_shape=[jax.ShapeDtypeStruct((t, w), BF16) for w in out_widths],
        compiler_params=_cparams("parallel"),
        name=name,
    )(*args)


def _qkv_plan(nq, nk, nv):
    plan = []
    for c0 in range(0, nq, MXU_N):
        plan.append((c0, 0, 0, c0))
    for c0 in range(0, nk, MXU_N):
        plan.append((nq + c0, 1, 1, c0))
    for c0 in range(0, nv, MXU_N):
        plan.append((nq + nk + c0, None, 2, c0))
    return tuple(plan)


def _head64_vec(q_norm, k_norm, q_scale):
    inv = jnp.full((LANES,), 1.0 / 64, F32)
    z = jnp.zeros((6, LANES), F32)
    vq = jnp.concatenate([inv[None], (jnp.tile(q_norm, 2) * q_scale)[None], z], 0)
    vk = jnp.concatenate([inv[None], jnp.tile(k_norm, 2)[None], z], 0)
    return jnp.stack([vq, vk], 0)


def _mla_proj_kernel(x_ref, g_ref, sc_ref, sh_ref, wd_ref, wuq_ref, wuk_ref, wuv_ref, bd_ref, vec_ref,
                     an_ref, cos_ref, sin_ref, q_out, k_out, v_out):
    h = _modulated(x_ref[...], g_ref[...], sc_ref[...], sh_ref[...]).astype(BF16)
    cos, sin = cos_ref[...], sin_ref[...]
    half = MLA_ROPE // 2
    cq = _dot(h, wd_ref[:, 0:MLA_Q_RANK])
    ckv = _dot(h, wd_ref[:, MLA_Q_RANK:MLA_Q_RANK + MLA_KV_RANK])
    kr = _dot(h, wd_ref[:, MLA_Q_RANK + MLA_KV_RANK:])
    cq = cq * lax.rsqrt(jnp.mean(cq * cq, axis=-1, keepdims=True) + EPS) * an_ref[0:1, 0:MLA_Q_RANK]
    ckv = ckv * lax.rsqrt(jnp.mean(ckv * ckv, axis=-1, keepdims=True) + EPS) * an_ref[1:2, 0:MLA_KV_RANK]
    cq = cq.astype(BF16)
    ckv = ckv.astype(BF16)
    kr = _group_norm_rope(kr, bd_ref[1], vec_ref[2, 0:1, :], vec_ref[2, 1:2, :], cos[:, LANES:], sin[:, LANES:], half)
    kr = pltpu.roll(kr, MLA_NOPE, 1)
    for hd in range(MLA_HEADS):
        c0 = hd * LANES
        qh = _dot(cq, wuq_ref[:, c0:c0 + LANES])
        qh = _group_norm_rope(qh, bd_ref[0], vec_ref[0, 0:1, :], vec_ref[0, 1:2, :], cos[:, :LANES], sin[:, :LANES], half)
        q_out[:, c0:c0 + LANES] = qh.astype(BF16)
        kh = _dot(ckv, wuk_ref[:, c0:c0 + LANES])
        kh = _group_norm_rope(kh, bd_ref[0], vec_ref[1, 0:1, :], vec_ref[1, 1:2, :]) + kr
        k_out[:, c0:c0 + LANES] = kh.astype(BF16)
    for c0 in range(0, MLA_HEADS * MLA_V, MXU_N):
        v_out[:, c0:c0 + MXU_N] = _dot(ckv, wuv_ref[:, c0:c0 + MXU_N]).astype(BF16)


def _mla_project(x2, mod3, g, p):
    t, d = x2.shape
    hq = MLA_NOPE + MLA_ROPE
    n_tiles = t // TM
    wd = jnp.pad(p["w_dqkv"], ((0, 0), (0, LANES - MLA_ROPE))).astype(BF16)
    wuq = p["w_uq"].reshape(MLA_Q_RANK, MLA_HEADS, hq)
    wuq = jnp.pad(wuq, ((0, 0), (0, 0), (0, LANES - hq))).reshape(MLA_Q_RANK, MLA_HEADS * LANES).astype(BF16)
    wukv = p["w_ukv"].reshape(MLA_KV_RANK, MLA_HEADS, MLA_NOPE + MLA_V)
    wuk = jnp.pad(wukv[:, :, :MLA_NOPE], ((0, 0), (0, 0), (0, LANES - MLA_NOPE)))
    wuk = wuk.reshape(MLA_KV_RANK, MLA_HEADS * LANES).astype(BF16)
    wuv = wukv[:, :, MLA_NOPE:].reshape(MLA_KV_RANK, MLA_HEADS * MLA_V).astype(BF16)
    scale = (MLA_NOPE + MLA_ROPE) ** -0.5 * LOG2E
    lane = np.arange(LANES)
    grp_head = np.where(lane < 64, 0, np.where(lane < 96, 1, 2))
    grp_kr = np.where(lane < 32, 0, 1)
    bd = jnp.stack([_block_diag(grp_head), _block_diag(grp_kr)], 0)
    inv_head = jnp.asarray(np.where(lane < 64, 1 / 64, 1 / 32), F32)
    zpad = jnp.zeros((LANES - hq,), F32)
    gq = jnp.concatenate([p["q_norm"], zpad]) * scale
    gk = jnp.concatenate([p["k_norm"][:MLA_NOPE], jnp.zeros((LANES - MLA_NOPE,), F32)])
    gkr = jnp.concatenate([p["k_norm"][MLA_NOPE:], jnp.zeros((LANES - MLA_ROPE,), F32)])
    inv_kr = jnp.asarray(np.where(lane < 32, 1 / 32, 1 / 96), F32)
    z6 = jnp.zeros((6, LANES), F32)
    vec = jnp.stack([jnp.concatenate([inv_head[None], gq[None], z6], 0),
                     jnp.concatenate([inv_head[None], gk[None], z6], 0),
                     jnp.concatenate([inv_kr[None], gkr[None], z6], 0)], 0)
    an = jnp.zeros((8, MLA_Q_RANK), F32)
    an = an.at[0].set(p["q_a_norm"]).at[1, :MLA_KV_RANK].set(p["kv_a_norm"])
    lm_head = np.where((lane >= 64) & (lane < 96), lane - 64, -1)
    lm_kr = np.where(lane < 32, lane, -1)
    ch, sh_ = _rope_tables(MLA_ROPE, lm_head)
    ck, sk = _rope_tables(MLA_ROPE, lm_kr)
    cos2 = jnp.concatenate([ch, ck], 1)
    sin2 = jnp.concatenate([sh_, sk], 1)

    full = lambda a: pl.BlockSpec(a.shape, lambda i: (0,) * a.ndim)
    rope_spec = pl.BlockSpec((TM, 2 * LANES), lambda i: (i % TILES_PER_SLAB, 0))
    out_w = (MLA_HEADS * LANES, MLA_HEADS * LANES, MLA_HEADS * MLA_V)
    return pl.pallas_call(
        _mla_proj_kernel,
        grid=(n_tiles,),
        in_specs=[pl.BlockSpec((TM, d), lambda i: (i, 0)), pl.BlockSpec((1, d), lambda i: (0, 0)),
                  _mod_spec(1), _mod_spec(0), full(wd), full(wuq), full(wuk), full(wuv), full(bd), full(vec),
                  full(an), rope_spec, rope_spec],
        out_specs=[pl.BlockSpec((TM, w), lambda i: (i, 0)) for w in out_w],
        out_shape=[jax.ShapeDtypeStruct((t, w), BF16) for w in out_w],
        compiler_params=_cparams("parallel"),
        name="mla_proj",
    )(x2, g.reshape(1, d), mod3, mod3, wd, wuq, wuk, wuv, bd, vec, an, cos2, sin2)


def _softmax_pv(s_list, v_list, sink=None):
    m = functools.reduce(jnp.maximum, [jnp.max(s, axis=-1, keepdims=True) for s in s_list])
    if sink is not None:
        m = jnp.maximum(m, sink)
    ps = [jnp.exp2(s - m) for s in s_list]
    l = functools.reduce(lambda a, b: a + b, [jnp.sum(p, axis=-1, keepdims=True) for p in ps])
    if sink is not None:
        l = l + jnp.exp2(sink - m)
    o = functools.reduce(lambda a, b: a + b, [_dot(p.astype(BF16), v) for p, v in zip(ps, v_list)])
    return o * (1.0 / l)


def _split_pair(x, lo):
    zero = jnp.zeros_like(x)
    return jnp.where(lo, x, zero), jnp.where(lo, zero, x)


def _na_kernel(q_ref, k_ref, v_ref, bias_ref, o_ref):
    lo = _lane_iota((1, LANES)) < 64
    kc = k_ref[SEQ:SL, :]
    vc = v_ref[SEQ:SL, :]
    nb = NA_KH * GRID_W

    def pair_attend(q, ks, vs, bias):
        n = q.shape[0]
        qa, qb = _split_pair(q, lo)
        qs = jnp.concatenate([qa, qb], axis=0)
        s_list = [_dot_nt(qs, k) for k in ks]
        if bias is not None:
            s_list[0] = s_list[0] + bias
        o = _softmax_pv(s_list, vs)
        return jnp.where(lo, o[:n], o[n:])

    def row_body(r, carry):
        row0 = jnp.clip(r - NA_KH // 2, 0, GRID_ROWS - NA_KH)
        q = q_ref[pl.ds(pl.multiple_of(r * GRID_W, GRID_W), GRID_W), :]
        ks = pl.multiple_of(row0 * GRID_W, GRID_W)
        kn = k_ref[pl.ds(ks, nb), :]
        vn = v_ref[pl.ds(ks, nb), :]
        o = pair_attend(q, [kn, kc], [vn, vc], bias_ref[r - row0])
        o_ref[pl.ds(pl.multiple_of(r * GRID_W, GRID_W), GRID_W), :] = o.astype(BF16)
        return carry

    lax.fori_loop(0, GRID_ROWS, row_body, 0)
    o_ref[SEQ:SL, :] = pair_attend(q_ref[SEQ:SL, :], [kc], [vc], None).astype(BF16)


def _na_bias_table(rpb):
    heads = rpb.shape[0]
    n_col = 2 * NA_KW - 1
    r2 = rpb.reshape(heads, 2 * NA_KH - 1, n_col) * LOG2E
    padded = jnp.pad(r2, ((0, 0), (0, 0), (GRID_W, GRID_W)))
    base = GRID_W + NA_KW - 1
    toeplitz = jnp.stack([padded[:, :, base - qc:base - qc + GRID_W] for qc in range(GRID_W)], axis=2)
    qc = np.arange(GRID_W)[:, None, None]
    kc = np.arange(GRID_W)[None, None, :]
    col0 = np.clip(qc - NA_KW // 2, 0, GRID_W - NA_KW)
    valid = np.broadcast_to((kc >= col0) & (kc < col0 + NA_KW), (GRID_W, NA_KH, GRID_W))
    valid = valid.reshape(GRID_W, NA_KH * GRID_W)
    classes = []
    for c in range(NA_KH):
        t = toeplitz[:, NA_KH - 1 - c:2 * NA_KH - 1 - c]
        t = t.transpose(0, 2, 1, 3).reshape(heads, GRID_W, NA_KH * GRID_W)
        classes.append(jnp.where(valid[None], t, NEG))
    tab = jnp.stack(classes, 0)
    return tab.reshape(NA_KH, heads // 2, 2 * GRID_W, NA_KH * GRID_W)


def _na_attention(q, k, v, bias, b):
    hp = q.shape[-1] // LANES
    slab = lambda: pl.BlockSpec((None, SL, LANES), lambda i, j: (i, 0, j))
    return pl.pallas_call(
        _na_kernel,
        grid=(b, hp),
        in_specs=[slab(), slab(), slab(),
                  pl.BlockSpec((NA_KH, None, 2 * GRID_W, NA_KH * GRID_W), lambda i, j: (0, j, 0, 0))],
        out_specs=slab(),
        out_shape=jax.ShapeDtypeStruct(q.shape, BF16),
        compiler_params=_cparams("parallel", "parallel"),
        name="na_attention",
    )(q, k, v, bias)


Q_BLK = 128


def _sw_kernel(sink_ref, q_ref, k_ref, v_ref, o_ref):
    g = pl.program_id(1)
    lane = _lane_iota((1, LANES))
    lo = lane < 64
    first_half = (jnp.zeros((1, LANES), jnp.int32) + (g % 2)) == 0
    target = (lane // 64) == (g % 2)
    kc = k_ref[SEQ:SL, :]
    vc = v_ref[SEQ:SL, :]
    span = Q_BLK + 2 * SW_WINDOW
    sinks = [sink_ref[4 * g + t] for t in range(4)]

    def swap(x):
        return jnp.concatenate([x[:, 64:], x[:, :64]], axis=1)

    def stack_heads(q):
        parts = []
        for blk in range(2):
            qb = q[:, blk * LANES:(blk + 1) * LANES]
            qs = swap(qb)
            even = jnp.where(first_half, qb, qs)
            odd = jnp.where(first_half, qs, qb)
            zero = jnp.zeros_like(qb)
            parts += [jnp.where(target, even, zero), jnp.where(target, odd, zero)]
        return jnp.concatenate(parts, axis=0)

    def unstack_heads(o, n):
        blocks = []
        for blk in range(2):
            oe = o[(2 * blk) * n:(2 * blk + 1) * n]
            oo = o[(2 * blk + 1) * n:(2 * blk + 2) * n]
            oe = jnp.where(first_half, oe, pltpu.roll(oe, 64, 1))
            oo = jnp.where(first_half, pltpu.roll(oo, 64, 1), oo)
            blocks.append(jnp.where(lo, oe, oo))
        return jnp.concatenate(blocks, axis=1)

    def sink_col(n):
        row = lax.broadcasted_iota(jnp.int32, (4 * n, 1), 0)
        return jnp.where(row < n, sinks[0], jnp.where(row < 2 * n, sinks[1],
                         jnp.where(row < 3 * n, sinks[2], sinks[3])))

    def block_body(i, carry):
        q0 = pl.multiple_of(i * Q_BLK, Q_BLK)
        start = pl.multiple_of(jnp.clip((i - 1) * Q_BLK, 0, SEQ - span), Q_BLK)
        qs = stack_heads(q_ref[pl.ds(q0, Q_BLK), :])
        kw = k_ref[pl.ds(start, span), :]
        vw = v_ref[pl.ds(start, span), :]
        s_loc = _dot_nt(qs, kw)
        qpos = q0 + (lax.broadcasted_iota(jnp.int32, (4 * Q_BLK, 1), 0) & (Q_BLK - 1))
        kpos = start + _lane_iota((1, span))
        s_loc = jnp.where(jnp.abs(kpos - qpos) <= SW_WINDOW, s_loc, NEG)
        s_ctx = _dot_nt(qs, kc)
        o = _softmax_pv([s_loc, s_ctx], [vw, vc], sink_col(Q_BLK))
        o_ref[pl.ds(q0, Q_BLK), :] = unstack_heads(o, Q_BLK).astype(BF16)
        return carry

    lax.fori_loop(0, SEQ // Q_BLK, block_body, 0)
    qs = stack_heads(q_ref[SEQ:SL, :])
    o = _softmax_pv([_dot_nt(qs, kc)], [vc], sink_col(CTX))
    o_ref[SEQ:SL, :] = unstack_heads(o, CTX).astype(BF16)


def _sw_attention(q, k, v, sink2, b):
    nq = 4 * 64
    slabq = lambda: pl.BlockSpec((None, SL, nq), lambda i, g: (i, 0, g))
    slabk = lambda: pl.BlockSpec((None, SL, LANES), lambda i, g: (i, 0, g // 2))
    return pl.pallas_call(
        _sw_kernel,
        grid=(b, SW_KV_HEADS),
        in_specs=[pl.BlockSpec(memory_space=pltpu.SMEM), slabq(), slabk(), slabk()],
        out_specs=slabq(),
        out_shape=jax.ShapeDtypeStruct(q.shape, BF16),
        compiler_params=_cparams("parallel", "parallel"),
        name="sw_attention",
    )(sink2, q, k, v)


DENSE_TQ = 512
DENSE_TK = 1024


def _online_step(s, v, m_ref, l_ref, acc_ref):
    chunks = s.shape[1] // LANES
    m_prev = m_ref[...]
    m_new = jnp.maximum(m_prev, jnp.max(s, axis=-1, keepdims=True))
    alpha = jnp.exp2(m_prev - m_new)
    p = jnp.exp2(s - jnp.concatenate([m_new] * chunks, axis=1))
    psum = functools.reduce(lambda a, b: a + b, [p[:, c * LANES:(c + 1) * LANES] for c in range(chunks)])
    l_ref[...] = alpha * l_ref[...] + psum
    acc_ref[...] = alpha * acc_ref[...] + _dot(p.astype(BF16), v)
    m_ref[...] = m_new


def _dense_streams(streams, v_ref, n_lat, vc):
    views = []
    for qs, _, (m_ref, l_ref, acc_ref) in streams:
        rows = qs.shape[0]
        m_ref[0:rows, :] = jnp.full((rows, LANES), NEG, F32)
        l_ref[0:rows, :] = jnp.zeros((rows, LANES), F32)
        acc_ref[0:rows, :] = jnp.zeros((rows, LANES), F32)
        views.append((m_ref.at[0:rows, :], l_ref.at[0:rows, :], acc_ref.at[0:rows, :]))

    for c in range(n_lat // DENSE_TK):
        v = v_ref[c * DENSE_TK:(c + 1) * DENSE_TK, :]
        for (qs, k_ref, _), view in zip(streams, views):
            _online_step(_dot_nt(qs, k_ref[c * DENSE_TK:(c + 1) * DENSE_TK, :]), v, *view)
    outs = []
    for (qs, k_ref, _), (mr, lr, ar) in zip(streams, views):
        _online_step(_dot_nt(qs, k_ref[SEQ:SL, :]), vc, mr, lr, ar)
        outs.append(ar[...] * (1.0 / jnp.sum(lr[...], axis=-1, keepdims=True)))
    return outs


def _diff_kernel(lambda_init, lam_ref, subln_ref, q_ref, k_ref, v_ref, o_ref, m_ref, l_ref, acc_ref):
    lo = _lane_iota((1, LANES)) < 64
    lam = lam_ref[...]
    lam_full = (jnp.exp(jnp.sum(lam[0:1] * lam[1:2], axis=-1, keepdims=True))
                - jnp.exp(jnp.sum(lam[2:3] * lam[3:4], axis=-1, keepdims=True)) + lambda_init)
    vc = v_ref[SEQ:SL, :]
    gain = subln_ref[...] * (1.0 - lambda_init)

    def q_body(i, carry):
        q0 = pl.multiple_of(i * DENSE_TQ, DENSE_TQ)
        qa, qb = _split_pair(q_ref[pl.ds(q0, DENSE_TQ), :], lo)
        qs = jnp.concatenate([qa, qb], axis=0)
        o, = _dense_streams([(qs, k_ref, (m_ref, l_ref, acc_ref))], v_ref, SEQ, vc)
        od = o[:DENSE_TQ] - lam_full * o[DENSE_TQ:]
        od = od * lax.rsqrt(jnp.mean(od * od, axis=-1, keepdims=True) + EPS) * gain
        o_ref[pl.ds(q0, DENSE_TQ), :] = od.astype(BF16)
        return carry

    lax.fori_loop(0, SEQ // DENSE_TQ, q_body, 0)
    o_ref[SEQ:SL, :] = jnp.zeros((CTX, LANES), BF16)


def _diff_attention(q, k, v, lam, subln, lambda_init, b):
    slab = lambda: pl.BlockSpec((None, SL, LANES), lambda i, j: (i, 0, j))
    rows = 2 * DENSE_TQ
    return pl.pallas_call(
        functools.partial(_diff_kernel, lambda_init),
        grid=(b, DIFF_HEADS),
        in_specs=[pl.BlockSpec(lam.shape, lambda i, j: (0, 0)),
                  pl.BlockSpec((1, LANES), lambda i, j: (0, 0)),
                  slab(), slab(), slab()],
        out_specs=slab(),
        out_shape=jax.ShapeDtypeStruct(v.shape, BF16),
        scratch_shapes=[pltpu.VMEM((rows, LANES), F32)] * 3,
        compiler_params=_cparams("parallel", "parallel"),
        name="diff_attention",
    )(lam, subln.reshape(1, LANES), q, k, v)


def _mla_kernel(q_ref, k_ref, v_ref, o_ref, *scratch):
    lo = _lane_iota((1, LANES)) < 64
    vc = v_ref[SEQ:SL, :]

    def two_heads(q, n_lat):
        streams = []
        for hd in range(2):
            sl = slice(hd * LANES, (hd + 1) * LANES)
            streams.append((q[:, sl], k_ref.at[:, sl], scratch[3 * hd:3 * hd + 3]))
        outs = _dense_streams(streams, v_ref, n_lat, vc)
        return jnp.where(lo, outs[0], outs[1])

    def q_body(i, carry):
        q0 = pl.multiple_of(i * DENSE_TQ, DENSE_TQ)
        o = two_heads(q_ref[pl.ds(q0, DENSE_TQ), :], SEQ)
        o_ref[pl.ds(q0, DENSE_TQ), :] = o.astype(BF16)
        return carry

    lax.fori_loop(0, SEQ // DENSE_TQ, q_body, 0)
    o_ref[SEQ:SL, :] = two_heads(q_ref[SEQ:SL, :], 0).astype(BF16)


def _mla_attention(q, k, v, b):
    slabq = lambda: pl.BlockSpec((None, SL, 2 * LANES), lambda i, j: (i, 0, j))
    slabv = lambda: pl.BlockSpec((None, SL, LANES), lambda i, j: (i, 0, j))
    return pl.pallas_call(
        _mla_kernel,
        grid=(b, MLA_HEADS // 2),
        in_specs=[slabq(), slabq(), slabv()],
        out_specs=slabv(),
        out_shape=jax.ShapeDtypeStruct(v.shape, BF16),
        scratch_shapes=[pltpu.VMEM((DENSE_TQ, LANES), F32)] * 6,
        compiler_params=_cparams("parallel", "parallel"),
        name="mla_attention",
    )(q, k, v)


def _oproj_kernel(a_ref, x_ref, wo_ref, g1_ref, ng_ref, sc_ref, sh_ref, wr_hi_ref, wr_lo_ref, br_ref,
                  xo_ref, h_ref, route_ref, cnt_ref):
    y = _dot(a_ref[...], wo_ref[...])
    x = x_ref[...] + g1_ref[...] * y
    xo_ref[...] = x
    h = _modulated(x, ng_ref[...], sc_ref[...], sh_ref[...])
    hi = h.astype(BF16)
    h_ref[...] = hi
    lo = (h - hi.astype(F32)).astype(BF16)
    z = _dot(hi, wr_hi_ref[...]) + _dot(lo, wr_hi_ref[...]) + _dot(hi, wr_lo_ref[...]) + br_ref[...]
    lane = _lane_iota(z.shape)
    lane_f = lane.astype(F32)
    big = jnp.float32(1e9)
    is_g = lane < N_GROUPS
    zg = jnp.where(is_g, z, NEG)
    gmax = jnp.max(zg, axis=-1, keepdims=True)
    gsum = jnp.sum(jnp.where(is_g, jnp.exp(zg - gmax), 0.0), axis=-1, keepdims=True)
    g_w = 1.0 / gsum
    g_idx = jnp.min(jnp.where(is_g & (zg == gmax), lane_f, big), axis=-1, keepdims=True)
    e_lo = N_GROUPS + EXPERTS_PER_GROUP * g_idx
    is_e = (lane_f >= e_lo) & (lane_f < e_lo + EXPERTS_PER_GROUP)
    ze = jnp.where(is_e, z, NEG)
    z1 = jnp.max(ze, axis=-1, keepdims=True)
    i1 = jnp.min(jnp.where(is_e & (ze == z1), lane_f, big), axis=-1, keepdims=True)
    ze2 = jnp.where(lane_f == i1, NEG, ze)
    z2 = jnp.max(ze2, axis=-1, keepdims=True)
    i2 = jnp.min(jnp.where(is_e & (ze2 == z2) & (lane_f != i1), lane_f, big), axis=-1, keepdims=True)
    r = jnp.exp(z2 - z1)
    gate1 = g_w / (1.0 + r)
    gate2 = g_w * r / (1.0 + r)
    e1 = i1 - N_GROUPS
    e2 = i2 - N_GROUPS
    route = jnp.where(lane == 0, gate1, jnp.where(lane == 1, gate2,
                      jnp.where(lane == 2, e1, jnp.where(lane == 3, e2, 0.0))))
    route_ref[...] = route

    @pl.when(pl.program_id(0) == 0)
    def _():
        cnt_ref[...] = jnp.zeros_like(cnt_ref)

    hits = jnp.where(lane_f == e1, 1.0, 0.0) + jnp.where(lane_f == e2, 1.0, 0.0)
    cnt_ref[...] += jnp.sum(hits, axis=0, keepdims=True)


def _route_pos_kernel(route_ref, base_ref, pos_ref, carry_ref):
    @pl.when(pl.program_id(0) == 0)
    def _():
        carry_ref[...] = base_ref[...]

    route = route_ref[...]
    lane_f = _lane_iota(route.shape).astype(F32)
    oh0 = jnp.where(lane_f == route[:, 2:3], 1.0, 0.0)
    oh1 = jnp.where(lane_f == route[:, 3:4], 1.0, 0.0)
    r = lax.broadcasted_iota(jnp.int32, (TM, TM), 0)
    c = lax.broadcasted_iota(jnp.int32, (TM, TM), 1)
    tri = jnp.where(c <= r, 1.0, 0.0).astype(BF16)
    carry = carry_ref[...]
    cnt0 = jnp.sum(oh0, axis=0, keepdims=True)
    pre0 = _dot(tri, oh0.astype(BF16)) - 1.0 + carry
    pre1 = _dot(tri, oh1.astype(BF16)) - 1.0 + carry + cnt0
    pos0 = jnp.sum(oh0 * pre0, axis=-1, keepdims=True)
    pos1 = jnp.sum(oh1 * pre1, axis=-1, keepdims=True)
    lane = _lane_iota(route.shape)
    pos_ref[...] = jnp.where(lane == 0, pos0, jnp.where(lane == 1, pos1, 0.0))
    carry_ref[...] = carry + cnt0 + jnp.sum(oh1, axis=0, keepdims=True)


def _route_pos(route, base):
    t = route.shape[0]
    return pl.pallas_call(
        _route_pos_kernel,
        grid=(t // TM,),
        in_specs=[pl.BlockSpec((TM, LANES), lambda i: (i, 0)), pl.BlockSpec((1, LANES), lambda i: (0, 0))],
        out_specs=pl.BlockSpec((TM, LANES), lambda i: (i, 0)),
        out_shape=jax.ShapeDtypeStruct((t, LANES), F32),
        scratch_shapes=[pltpu.VMEM((1, LANES), F32)],
        compiler_params=_cparams("arbitrary"),
        name="route_pos",
    )(route, base)


def _oproj_router(attn2, x2, mod3, wo_bf, ng, wr_hi, wr_lo, br):
    t, d = x2.shape
    n_in = attn2.shape[1]
    full = lambda a: pl.BlockSpec(a.shape, lambda i: (0,) * a.ndim)
    row = lambda w: pl.BlockSpec((TM, w), lambda i: (i, 0))
    return pl.pallas_call(
        _oproj_kernel,
        grid=(t // TM,),
        in_specs=[row(n_in), row(d), full(wo_bf), _mod_spec(2), pl.BlockSpec((1, d), lambda i: (0, 0)),
                  _mod_spec(4), _mod_spec(3), full(wr_hi), full(wr_lo), full(br)],
        out_specs=[row(d), row(d), row(LANES), pl.BlockSpec((1, LANES), lambda i: (0, 0))],
        out_shape=[jax.ShapeDtypeStruct((t, d), F32), jax.ShapeDtypeStruct((t, d), BF16),
                   jax.ShapeDtypeStruct((t, LANES), F32), jax.ShapeDtypeStruct((1, LANES), F32)],
        compiler_params=_cparams("arbitrary"),
        name="oproj_router",
    )(attn2, x2, wo_bf, mod3, ng.reshape(1, d), mod3, mod3, wr_hi, wr_lo, br)


def _expert_kernel(be_ref, nu_ref, x_ref, wgu_ref, wd_ref, o_ref, wgu_bf, wd_bf):
    i = pl.program_id(0)
    new_expert = (i == 0) | (be_ref[i] != be_ref[jnp.maximum(i - 1, 0)])

    @pl.when(new_expert)
    def _():
        wgu_bf[...] = wgu_ref[...].astype(BF16)
        wd_bf[...] = wd_ref[...].astype(BF16)

    @pl.when(i < nu_ref[0])
    def _():
        gu = _dot(x_ref[...], wgu_bf[...])
        gte, up = gu[:, :D_EXPERT], gu[:, D_EXPERT:]
        a = (gte * jax.nn.sigmoid(gte) * up).astype(BF16)
        o_ref[...] = _dot(a, wd_bf[...]).astype(BF16)

    @pl.when(i >= nu_ref[0])
    def _():
        o_ref[...] = jnp.zeros_like(o_ref)


def _experts(xs, block_expert, n_used, wgu, wd):
    n_rows, d = xs.shape
    n_blocks = n_rows // MOE_BLK
    return pl.pallas_call(
        _expert_kernel,
        grid_spec=pltpu.PrefetchScalarGridSpec(
            num_scalar_prefetch=2,
            grid=(n_blocks,),
            in_specs=[pl.BlockSpec((MOE_BLK, d), lambda i, be, nu: (i, 0)),
                      pl.BlockSpec((None, d, 2 * D_EXPERT), lambda i, be, nu: (be[i], 0, 0)),
                      pl.BlockSpec((None, D_EXPERT, d), lambda i, be, nu: (be[i], 0, 0))],
            out_specs=pl.BlockSpec((MOE_BLK, d), lambda i, be, nu: (i, 0)),
            scratch_shapes=[pltpu.VMEM((d, 2 * D_EXPERT), BF16), pltpu.VMEM((D_EXPERT, d), BF16)]),
        out_shape=jax.ShapeDtypeStruct((n_rows, d), BF16),
        compiler_params=_cparams("arbitrary"),
        name="experts",
    )(block_expert, n_used, xs, wgu, wd)


def _combine_kernel(x_ref, y0_ref, y1_ref, route_ref, g2_ref, o_ref):
    route = route_ref[...]
    y = route[:, 0:1] * y0_ref[...].astype(F32) + route[:, 1:2] * y1_ref[...].astype(F32)
    o_ref[...] = x_ref[...] + g2_ref[...] * y


def _combine(x2, y0, y1, route, mod3):
    t, d = x2.shape
    row = lambda w: pl.BlockSpec((TM, w), lambda i: (i, 0))
    return pl.pallas_call(
        _combine_kernel,
        grid=(t // TM,),
        in_specs=[row(d), row(d), row(d), row(LANES), _mod_spec(5)],
        out_specs=row(d),
        out_shape=jax.ShapeDtypeStruct((t, d), F32),
        compiler_params=_cparams("parallel"),
        name="moe_combine",
    )(x2, y0, y1, route, mod3)


def _moe_dispatch(route, counts_f, t):
    counts = counts_f[0, :N_EXPERTS].astype(jnp.int32)
    padded = (counts + MOE_BLK - 1) // MOE_BLK * MOE_BLK
    seg_end = jnp.cumsum(padded)
    seg_start = seg_end - padded
    base = jnp.zeros((1, LANES), F32).at[0, :N_EXPERTS].set(seg_start.astype(F32))
    pos_f = _route_pos(route, base)
    pos = pos_f[:, :2].T.astype(jnp.int32)
    n_rows = -(-(2 * t + N_EXPERTS * (MOE_BLK - 1)) // MOE_BLK) * MOE_BLK
    n_blocks = n_rows // MOE_BLK
    token = jnp.tile(jnp.arange(t, dtype=jnp.int32), 2)
    row_token = jnp.zeros((n_rows,), jnp.int32).at[pos.reshape(-1)].set(
        token, mode="promise_in_bounds", unique_indices=True)
    block_expert = jnp.minimum(
        jnp.searchsorted(seg_end, jnp.arange(n_blocks, dtype=jnp.int32) * MOE_BLK, side="right"),
        N_EXPERTS - 1).astype(jnp.int32)
    n_used = (seg_end[-1] // MOE_BLK).astype(jnp.int32).reshape(1)
    return row_token, pos, block_expert, n_used


def _take_rows(a, idx):
    return a.at[idx].get(mode="promise_in_bounds")


def _moe(x2, h_bf, route, counts_f, mod3, wgu, wd):
    t = x2.shape[0]
    row_token, pos, block_expert, n_used = _moe_dispatch(route, counts_f, t)
    xs = _take_rows(h_bf, row_token)
    ys = _experts(xs, block_expert, n_used, wgu, wd)
    y0 = _take_rows(ys, pos[0])
    y1 = _take_rows(ys, pos[1])
    return _combine(x2, y0, y1, route, mod3)


def kernel(x, c, ctx, c_ctx, ada_w, ada_b, norm_g, moe_w_router_group, moe_b_router_group, moe_w_router_expert, moe_b_router_expert, moe_w_gate_up, moe_w_down, na_w_qkv, na_w_o, na_q_norm, na_k_norm, na_rpb, sw_w_qkv, sw_w_o, sw_q_norm, sw_k_norm, sw_sink, mla_w_dqkv, mla_q_a_norm, mla_kv_a_norm, mla_w_uq, mla_w_ukv, mla_q_norm, mla_k_norm, mla_w_o, diff_w_qkv, diff_q_norm, diff_k_norm, diff_lambda, diff_subln, diff_w_o):
    slab = _stack(x, c, ctx, c_ctx, ada_w, ada_b, norm_g, moe_w_router_group, moe_b_router_group, moe_w_router_expert, moe_b_router_expert, moe_w_gate_up, moe_w_down, na_w_qkv, na_w_o, na_q_norm, na_k_norm, na_rpb, sw_w_qkv, sw_w_o, sw_q_norm, sw_k_norm, sw_sink, mla_w_dqkv, mla_q_a_norm, mla_kv_a_norm, mla_w_uq, mla_w_ukv, mla_q_norm, mla_k_norm, mla_w_o, diff_w_qkv, diff_q_norm, diff_k_norm, diff_lambda, diff_subln, diff_w_o)
    return slab[:, :SEQ]


def _stack(x, c, ctx, c_ctx, ada_w, ada_b, norm_g, moe_w_router_group, moe_b_router_group, moe_w_router_expert, moe_b_router_expert, moe_w_gate_up, moe_w_down, na_w_qkv, na_w_o, na_q_norm, na_k_norm, na_rpb, sw_w_qkv, sw_w_o, sw_q_norm, sw_k_norm, sw_sink, mla_w_dqkv, mla_q_a_norm, mla_kv_a_norm, mla_w_uq, mla_w_ukv, mla_q_norm, mla_k_norm, mla_w_o, diff_w_qkv, diff_q_norm, diff_k_norm, diff_lambda, diff_subln, diff_w_o):
    b, s, d = x.shape
    assert (s, d) == (SEQ, D_MODEL) and ctx.shape == (b, CTX, d) and b <= 8
    depth = ada_w.shape[0]
    assert depth <= 4
    t = b * SL

    cc = jnp.zeros((16, d), F32).at[:b].set(c).at[8].set(c_ctx)
    mod = _ada_table(cc, ada_w, ada_b)
    x2 = jnp.concatenate([x, ctx], axis=1).reshape(t, d)

    lane = np.arange(LANES)
    bd64 = _block_diag(lane // 64)
    lm64 = lane % 64
    rope64 = _rope_tables(64, lm64)
    scale64 = 64 ** -0.5 * LOG2E

    r3 = lambda a: a.reshape(b, SL, a.shape[-1])
    for l in range(depth):
        mod3 = mod[l].reshape(16 * 6, 1, d)
        if l == 0:
            w = na_w_qkv[0].astype(BF16)
            q, k, v = _project(x2, mod3, norm_g[l, 0], w, _qkv_plan(1024, 1024, 1024), (1024, 1024, 1024),
                               bd64, _head64_vec(na_q_norm[0], na_k_norm[0], scale64), name="na_proj")
            attn = _na_attention(r3(q), r3(k), r3(v), _na_bias_table(na_rpb[0]), b)
            w_o = na_w_o[0]
        elif l == 1:
            w = sw_w_qkv[0].astype(BF16)
            q, k, v = _project(x2, mod3, norm_g[l, 0], w, _qkv_plan(1024, 256, 256), (1024, 256, 256),
                               bd64, _head64_vec(sw_q_norm[0], sw_k_norm[0], scale64),
                               rope=rope64, rope_half=32, name="sw_proj")
            attn = _sw_attention(r3(q), r3(k), r3(v), sw_sink[0] * LOG2E, b)
            w_o = sw_w_o[0]
        elif l == 2:
            p = dict(w_dqkv=mla_w_dqkv[0], q_a_norm=mla_q_a_norm[0], kv_a_norm=mla_kv_a_norm[0],
                     w_uq=mla_w_uq[0], w_ukv=mla_w_ukv[0], q_norm=mla_q_norm[0], k_norm=mla_k_norm[0])
            q, k, v = _mla_project(x2, mod3, norm_g[l, 0], p)
            attn = _mla_attention(r3(q), r3(k), r3(v), b)
            w_o = mla_w_o[0]
        else:
            w = diff_w_qkv[0].astype(BF16)
            q, k, v = _project(x2, mod3, norm_g[l, 0], w, _qkv_plan(1024, 1024, 1024), (1024, 1024, 1024),
                               bd64, _head64_vec(diff_q_norm[0], diff_k_norm[0], scale64),
                               rope=rope64, rope_half=32, name="diff_proj")
            lambda_init = 0.8 - 0.6 * math.exp(-0.3 * l)
            attn = _diff_attention(r3(q), r3(k), r3(v), diff_lambda[0], diff_subln[0], lambda_init, b)
            w_o = diff_w_o[0]

        wr = jnp.concatenate([moe_w_router_group[l], moe_w_router_expert[l]], axis=1)
        wr = jnp.pad(wr, ((0, 0), (0, LANES - wr.shape[1])))
        wr_hi = wr.astype(BF16)
        wr_lo = (wr - wr_hi.astype(F32)).astype(BF16)
        br = jnp.concatenate([moe_b_router_group[l], moe_b_router_expert[l]])
        br = jnp.pad(br, (0, LANES - br.shape[0])).reshape(1, LANES)
        x2, h_bf, route, counts_f = _oproj_router(attn.reshape(t, -1), x2, mod3, w_o.astype(BF16), norm_g[l, 1],
                                                  wr_hi, wr_lo, br)
        x2 = _moe(x2, h_bf, route, counts_f, mod3, moe_w_gate_up[l], moe_w_down[l])

    return x2.reshape(b, SL, d)
```

```python
import functools
import math

import jax
import jax.numpy as jnp
import numpy as np
from jax import lax
from jax.experimental import pallas as pl
from jax.experimental.pallas import tpu as pltpu

F32 = jnp.float32
BF16 = jnp.bfloat16

D_MODEL = 1024
SEQ = 4096
CTX = 256
SL = SEQ + CTX
GRID_W = 64
GRID_ROWS = SEQ // GRID_W
EPS = 1e-6
ROPE_BASE = 10000.0
NEG = -1e30
LOG2E = 1.4426950408889634

NA_KH, NA_KW = 8, 16
SW_WINDOW = 128
SW_HEADS, SW_KV_HEADS = 16, 4
MLA_Q_RANK, MLA_KV_RANK, MLA_NOPE, MLA_ROPE, MLA_V, MLA_HEADS = 384, 256, 64, 32, 64, 16
DIFF_HEADS = 8
N_GROUPS, EXPERTS_PER_GROUP, N_EXPERTS, D_EXPERT = 4, 8, 32, 256

LANES = 128
MXU_N = 256
TM = 256
TILES_PER_SLAB = SL // TM
MOE_BLK = 256
VMEM_LIMIT = 56 * 1024 * 1024


def _cparams(*sem):
    return pltpu.CompilerParams(dimension_semantics=sem, vmem_limit_bytes=VMEM_LIMIT)


def _lane_iota(shape):
    return lax.broadcasted_iota(jnp.int32, shape, len(shape) - 1)


def _dot(a, b):
    return jnp.dot(a, b, preferred_element_type=F32)


def _dot_nt(a, b):
    return lax.dot_general(a, b, (((1,), (1,)), ((), ())), preferred_element_type=F32)


def _ada_kernel(cc_ref, w_ref, b_ref, o_ref):
    a = cc_ref[...]
    a = a * jax.nn.sigmoid(a)
    hi = a.astype(BF16)
    lo = (a - hi.astype(F32)).astype(BF16)
    w = w_ref[...]
    whi = w.astype(BF16)
    wlo = (w - whi.astype(F32)).astype(BF16)
    o_ref[...] = _dot(hi, whi) + _dot(lo, whi) + _dot(hi, wlo) + b_ref[...]


def _ada_table(cc, ada_w, ada_b):
    depth, d, n = ada_w.shape
    tn = 1024
    return pl.pallas_call(
        _ada_kernel,
        grid=(depth, n // tn),
        in_specs=[pl.BlockSpec((16, d), lambda l, j: (0, 0)),
                  pl.BlockSpec((None, d, tn), lambda l, j: (l, 0, j)),
                  pl.BlockSpec((None, 1, tn), lambda l, j: (l, 0, j))],
        out_specs=pl.BlockSpec((None, 16, tn), lambda l, j: (l, 0, j)),
        out_shape=jax.ShapeDtypeStruct((depth, 16, n), F32),
        compiler_params=_cparams("parallel", "parallel"),
        name="ada_table",
    )(cc, ada_w, ada_b.reshape(depth, 1, n))


def _mod_spec(part):
    def index(i):
        m = jnp.where(i % TILES_PER_SLAB == TILES_PER_SLAB - 1, 8, i // TILES_PER_SLAB)
        return (m * 6 + part, 0, 0)
    return pl.BlockSpec((None, 1, D_MODEL), index)


def _modulated(x, g, sc, sh):
    ms = jnp.mean(x * x, axis=-1, keepdims=True)
    return (x * lax.rsqrt(ms + EPS)) * g * (1.0 + sc) + sh


def _group_norm_rope(y, bd, invcnt, gain, cos=None, sin=None, half=None):
    ss = _dot((y * y).astype(BF16), bd)
    yn = y * lax.rsqrt(ss * invcnt + EPS) * gain
    if cos is not None:
        lane = _lane_iota(yn.shape)
        fwd = pltpu.roll(yn, LANES - half, 1)
        bwd = pltpu.roll(yn, half, 1)
        partner = jnp.where((lane & (2 * half - 1)) < half, fwd, bwd)
        yn = yn * cos + partner * sin
    return yn


def _block_diag(group_of_lane):
    g = np.asarray(group_of_lane)
    return jnp.asarray((g[:, None] == g[None, :]).astype(np.float32), dtype=BF16)


def _rope_tables(rot_dim, lane_map):
    t = jnp.arange(SEQ, dtype=jnp.int32)
    n_freq = rot_dim // 4
    inv_freq = ROPE_BASE ** (-jnp.arange(n_freq, dtype=F32) / n_freq)
    rows = (t // GRID_W).astype(F32)
    cols = (t % GRID_W).astype(F32)
    ang = jnp.concatenate([rows[:, None] * inv_freq, cols[:, None] * inv_freq], axis=-1)
    cos, sin = jnp.cos(ang), jnp.sin(ang)
    lm = np.asarray(lane_map)
    rot = lm >= 0
    idx = np.where(rot, lm % (rot_dim // 2), 0)
    sign = np.where(lm < rot_dim // 2, -1.0, 1.0).astype(np.float32)
    cos_t = jnp.where(rot[None, :], cos[:, idx], 1.0)
    sin_t = jnp.where(rot[None, :], sin[:, idx] * sign[None, :], 0.0)
    ident_c = jnp.ones((CTX, LANES), F32)
    ident_s = jnp.zeros((CTX, LANES), F32)
    return jnp.concatenate([cos_t, ident_c], 0), jnp.concatenate([sin_t, ident_s], 0)


def _proj_kernel(plan, rope_half, x_ref, g_ref, sc_ref, sh_ref, w_ref, bd_ref, vec_ref, *rest):
    if rope_half is not None:
        cos_ref, sin_ref = rest[:2]
        outs = rest[2:]
        cos, sin = cos_ref[...], sin_ref[...]
    else:
        outs = rest
        cos = sin = None
    h = _modulated(x_ref[...], g_ref[...], sc_ref[...], sh_ref[...]).astype(BF16)
    bd = bd_ref[...]
    for c0, ep, out_idx, oc0 in plan:
        acc = _dot(h, w_ref[:, c0:c0 + MXU_N])
        for s in range(MXU_N // LANES):
            y = acc[:, s * LANES:(s + 1) * LANES]
            if ep is not None:
                y = _group_norm_rope(y, bd, vec_ref[ep, 0:1, :], vec_ref[ep, 1:2, :], cos, sin, rope_half)
            lo = oc0 + s * LANES
            outs[out_idx][:, lo:lo + LANES] = y.astype(BF16)


def _project(x2, mod3, g, w_bf, plan, out_widths, bd, vec, rope=None, rope_half=None, name="proj"):
    t, d = x2.shape
    n = w_bf.shape[1]
    n_tiles = t // TM
    in_specs = [pl.BlockSpec((TM, d), lambda i: (i, 0)),
                pl.BlockSpec((1, d), lambda i: (0, 0)),
                _mod_spec(1), _mod_spec(0),
                pl.BlockSpec((d, n), lambda i: (0, 0)),
                pl.BlockSpec(bd.shape, lambda i: (0, 0)),
                pl.BlockSpec(vec.shape, lambda i: (0, 0, 0))]
    args = [x2, g.reshape(1, d), mod3, mod3, w_bf, bd, vec]
    if rope is not None:
        in_specs += [pl.BlockSpec((TM, LANES), lambda i: (i % TILES_PER_SLAB, 0))] * 2
        args += list(rope)
    return pl.pallas_call(
        functools.partial(_proj_kernel, plan, rope_half),
        grid=(n_tiles,),
        in_specs=in_specs,
        out_specs=[pl.BlockSpec((TM, w), lambda i: (i, 0)) for w in out_widths],
        out_shape=[jax.ShapeDtypeStruct((t, w), BF16) for w in out_widths],
        compiler_params=_cparams("parallel"),
        name=name,
    )(*args)


def _qkv_plan(nq, nk, nv):
    plan = []
    for c0 in range(0, nq, MXU_N):
        plan.append((c0, 0, 0, c0))
    for c0 in range(0, nk, MXU_N):
        plan.append((nq + c0, 1, 1, c0))
    for c0 in range(0, nv, MXU_N):
        plan.append((nq + nk + c0, None, 2, c0))
    return tuple(plan)


def _head64_vec(q_norm, k_norm, q_scale):
    inv = jnp.full((LANES,), 1.0 / 64, F32)
    z = jnp.zeros((6, LANES), F32)
    vq = jnp.concatenate([inv[None], (jnp.tile(q_norm, 2) * q_scale)[None], z], 0)
    vk = jnp.concatenate([inv[None], jnp.tile(k_norm, 2)[None], z], 0)
    return jnp.stack([vq, vk], 0)


def _mla_proj_kernel(x_ref, g_ref, sc_ref, sh_ref, wd_ref, wuq_ref, wuk_ref, wuv_ref, bd_ref, vec_ref,
                     an_ref, cos_ref, sin_ref, q_out, k_out, v_out):
    h = _modulated(x_ref[...], g_ref[...], sc_ref[...], sh_ref[...]).astype(BF16)
    cos, sin = cos_ref[...], sin_ref[...]
    half = MLA_ROPE // 2
    cq = _dot(h, wd_ref[:, 0:MLA_Q_RANK])
    ckv = _dot(h, wd_ref[:, MLA_Q_RANK:MLA_Q_RANK + MLA_KV_RANK])
    kr = _dot(h, wd_ref[:, MLA_Q_RANK + MLA_KV_RANK:])
    cq = cq * lax.rsqrt(jnp.mean(cq * cq, axis=-1, keepdims=True) + EPS) * an_ref[0:1, 0:MLA_Q_RANK]
    ckv = ckv * lax.rsqrt(jnp.mean(ckv * ckv, axis=-1, keepdims=True) + EPS) * an_ref[1:2, 0:MLA_KV_RANK]
    cq = cq.astype(BF16)
    ckv = ckv.astype(BF16)
    kr = _group_norm_rope(kr, bd_ref[1], vec_ref[2, 0:1, :], vec_ref[2, 1:2, :], cos[:, LANES:], sin[:, LANES:], half)
    kr = pltpu.roll(kr, MLA_NOPE, 1)
    for hd in range(MLA_HEADS):
        c0 = hd * LANES
        qh = _dot(cq, wuq_ref[:, c0:c0 + LANES])
        qh = _group_norm_rope(qh, bd_ref[0], vec_ref[0, 0:1, :], vec_ref[0, 1:2, :], cos[:, :LANES], sin[:, :LANES], half)
        q_out[:, c0:c0 + LANES] = qh.astype(BF16)
        kh = _dot(ckv, wuk_ref[:, c0:c0 + LANES])
        kh = _group_norm_rope(kh, bd_ref[0], vec_ref[1, 0:1, :], vec_ref[1, 1:2, :]) + kr
        k_out[:, c0:c0 + LANES] = kh.astype(BF16)
    for c0 in range(0, MLA_HEADS * MLA_V, MXU_N):
        v_out[:, c0:c0 + MXU_N] = _dot(ckv, wuv_ref[:, c0:c0 + MXU_N]).astype(BF16)


def _mla_project(x2, mod3, g, p):
    t, d = x2.shape
    hq = MLA_NOPE + MLA_ROPE
    n_tiles = t // TM
    wd = jnp.pad(p["w_dqkv"], ((0, 0), (0, LANES - MLA_ROPE))).astype(BF16)
    wuq = p["w_uq"].reshape(MLA_Q_RANK, MLA_HEADS, hq)
    wuq = jnp.pad(wuq, ((0, 0), (0, 0), (0, LANES - hq))).reshape(MLA_Q_RANK, MLA_HEADS * LANES).astype(BF16)
    wukv = p["w_ukv"].reshape(MLA_KV_RANK, MLA_HEADS, MLA_NOPE + MLA_V)
    wuk = jnp.pad(wukv[:, :, :MLA_NOPE], ((0, 0), (0, 0), (0, LANES - MLA_NOPE)))
    wuk = wuk.reshape(MLA_KV_RANK, MLA_HEADS * LANES).astype(BF16)
    wuv = wukv[:, :, MLA_NOPE:].reshape(MLA_KV_RANK, MLA_HEADS * MLA_V).astype(BF16)
    scale = (MLA_NOPE + MLA_ROPE) ** -0.5 * LOG2E
    lane = np.arange(LANES)
    grp_head = np.where(lane < 64, 0, np.where(lane < 96, 1, 2))
    grp_kr = np.where(lane < 32, 0, 1)
    bd = jnp.stack([_block_diag(grp_head), _block_diag(grp_kr)], 0)
    inv_head = jnp.asarray(np.where(lane < 64, 1 / 64, 1 / 32), F32)
    zpad = jnp.zeros((LANES - hq,), F32)
    gq = jnp.concatenate([p["q_norm"], zpad]) * scale
    gk = jnp.concatenate([p["k_norm"][:MLA_NOPE], jnp.zeros((LANES - MLA_NOPE,), F32)])
    gkr = jnp.concatenate([p["k_norm"][MLA_NOPE:], jnp.zeros((LANES - MLA_ROPE,), F32)])
    inv_kr = jnp.asarray(np.where(lane < 32, 1 / 32, 1 / 96), F32)
    z6 = jnp.zeros((6, LANES), F32)
    vec = jnp.stack([jnp.concatenate([inv_head[None], gq[None], z6], 0),
                     jnp.concatenate([inv_head[None], gk[None], z6], 0),
                     jnp.concatenate([inv_kr[None], gkr[None], z6], 0)], 0)
    an = jnp.zeros((8, MLA_Q_RANK), F32)
    an = an.at[0].set(p["q_a_norm"]).at[1, :MLA_KV_RANK].set(p["kv_a_norm"])
    lm_head = np.where((lane >= 64) & (lane < 96), lane - 64, -1)
    lm_kr = np.where(lane < 32, lane, -1)
    ch, sh_ = _rope_tables(MLA_ROPE, lm_head)
    ck, sk = _rope_tables(MLA_ROPE, lm_kr)
    cos2 = jnp.concatenate([ch, ck], 1)
    sin2 = jnp.concatenate([sh_, sk], 1)

    full = lambda a: pl.BlockSpec(a.shape, lambda i: (0,) * a.ndim)
    rope_spec = pl.BlockSpec((TM, 2 * LANES), lambda i: (i % TILES_PER_SLAB, 0))
    out_w = (MLA_HEADS * LANES, MLA_HEADS * LANES, MLA_HEADS * MLA_V)
    return pl.pallas_call(
        _mla_proj_kernel,
        grid=(n_tiles,),
        in_specs=[pl.BlockSpec((TM, d), lambda i: (i, 0)), pl.BlockSpec((1, d), lambda i: (0, 0)),
                  _mod_spec(1), _mod_spec(0), full(wd), full(wuq), full(wuk), full(wuv), full(bd), full(vec),
                  full(an), rope_spec, rope_spec],
        out_specs=[pl.BlockSpec((TM, w), lambda i: (i, 0)) for w in out_w],
        out_shape=[jax.ShapeDtypeStruct((t, w), BF16) for w in out_w],
        compiler_params=_cparams("parallel"),
        name="mla_proj",
    )(x2, g.reshape(1, d), mod3, mod3, wd, wuq, wuk, wuv, bd, vec, an, cos2, sin2)


def _softmax_pv(s_list, v_list, sink=None):
    m = functools.reduce(jnp.maximum, [jnp.max(s, axis=-1, keepdims=True) for s in s_list])
    if sink is not None:
        m = jnp.maximum(m, sink)
    ps = [jnp.exp2(s - m) for s in s_list]
    l = functools.reduce(lambda a, b: a + b, [jnp.sum(p, axis=-1, keepdims=True) for p in ps])
    if sink is not None:
        l = l + jnp.exp2(sink - m)
    o = functools.reduce(lambda a, b: a + b, [_dot(p.astype(BF16), v) for p, v in zip(ps, v_list)])
    return o * (1.0 / l)


def _split_pair(x, lo):
    zero = jnp.zeros_like(x)
    return jnp.where(lo, x, zero), jnp.where(lo, zero, x)


def _na_kernel(q_ref, k_ref, v_ref, bias_ref, o_ref):
    lo = _lane_iota((1, LANES)) < 64
    kc = k_ref[SEQ:SL, :]
    vc = v_ref[SEQ:SL, :]
    nb = NA_KH * GRID_W

    def pair_attend(q, ks, vs, bias):
        n = q.shape[0]
        qa, qb = _split_pair(q, lo)
        qs = jnp.concatenate([qa, qb], axis=0)
        s_list = [_dot_nt(qs, k) for k in ks]
        if bias is not None:
            s_list[0] = s_list[0] + bias
        o = _softmax_pv(s_list, vs)
        return jnp.where(lo, o[:n], o[n:])

    def row_body(r, carry):
        row0 = jnp.clip(r - NA_KH // 2, 0, GRID_ROWS - NA_KH)
        q = q_ref[pl.ds(pl.multiple_of(r * GRID_W, GRID_W), GRID_W), :]
        ks = pl.multiple_of(row0 * GRID_W, GRID_W)
        kn = k_ref[pl.ds(ks, nb), :]
        vn = v_ref[pl.ds(ks, nb), :]
        o = pair_attend(q, [kn, kc], [vn, vc], bias_ref[r - row0])
        o_ref[pl.ds(pl.multiple_of(r * GRID_W, GRID_W), GRID_W), :] = o.astype(BF16)
        return carry

    lax.fori_loop(0, GRID_ROWS, row_body, 0, unroll=8)
    o_ref[SEQ:SL, :] = pair_attend(q_ref[SEQ:SL, :], [kc], [vc], None).astype(BF16)


def _na_bias_table(rpb):
    heads = rpb.shape[0]
    n_col = 2 * NA_KW - 1
    r2 = rpb.reshape(heads, 2 * NA_KH - 1, n_col) * LOG2E
    padded = jnp.pad(r2, ((0, 0), (0, 0), (GRID_W, GRID_W)))
    base = GRID_W + NA_KW - 1
    toeplitz = jnp.stack([padded[:, :, base - qc:base - qc + GRID_W] for qc in range(GRID_W)], axis=2)
    qc = np.arange(GRID_W)[:, None, None]
    kc = np.arange(GRID_W)[None, None, :]
    col0 = np.clip(qc - NA_KW // 2, 0, GRID_W - NA_KW)
    valid = np.broadcast_to((kc >= col0) & (kc < col0 + NA_KW), (GRID_W, NA_KH, GRID_W))
    valid = valid.reshape(GRID_W, NA_KH * GRID_W)
    classes = []
    for c in range(NA_KH):
        t = toeplitz[:, NA_KH - 1 - c:2 * NA_KH - 1 - c]
        t = t.transpose(0, 2, 1, 3).reshape(heads, GRID_W, NA_KH * GRID_W)
        classes.append(jnp.where(valid[None], t, NEG))
    tab = jnp.stack(classes, 0)
    return tab.reshape(NA_KH, heads // 2, 2 * GRID_W, NA_KH * GRID_W)


def _na_attention(q, k, v, bias, b):
    hp = q.shape[-1] // LANES
    slab = lambda: pl.BlockSpec((None, SL, LANES), lambda i, j: (i, 0, j))
    return pl.pallas_call(
        _na_kernel,
        grid=(b, hp),
        in_specs=[slab(), slab(), slab(),
                  pl.BlockSpec((NA_KH, None, 2 * GRID_W, NA_KH * GRID_W), lambda i, j: (0, j, 0, 0))],
        out_specs=slab(),
        out_shape=jax.ShapeDtypeStruct(q.shape, BF16),
        compiler_params=_cparams("parallel", "parallel"),
        name="na_attention",
    )(q, k, v, bias)


Q_BLK = 128


def _sw_kernel(sink_ref, q_ref, k_ref, v_ref, o_ref):
    g = pl.program_id(1)
    lane = _lane_iota((1, LANES))
    lo = lane < 64
    first_half = (jnp.zeros((1, LANES), jnp.int32) + (g % 2)) == 0
    target = (lane // 64) == (g % 2)
    kc = k_ref[SEQ:SL, :]
    vc = v_ref[SEQ:SL, :]
    span = Q_BLK + 2 * SW_WINDOW
    sinks = [sink_ref[4 * g + t] for t in range(4)]

    def swap(x):
        return jnp.concatenate([x[:, 64:], x[:, :64]], axis=1)

    def stack_heads(q):
        parts = []
        for blk in range(2):
            qb = q[:, blk * LANES:(blk + 1) * LANES]
            qs = swap(qb)
            even = jnp.where(first_half, qb, qs)
            odd = jnp.where(first_half, qs, qb)
            zero = jnp.zeros_like(qb)
            parts += [jnp.where(target, even, zero), jnp.where(target, odd, zero)]
        return jnp.concatenate(parts, axis=0)

    def unstack_heads(o, n):
        blocks = []
        for blk in range(2):
            oe = o[(2 * blk) * n:(2 * blk + 1) * n]
            oo = o[(2 * blk + 1) * n:(2 * blk + 2) * n]
            oe = jnp.where(first_half, oe, pltpu.roll(oe, 64, 1))
            oo = jnp.where(first_half, pltpu.roll(oo, 64, 1), oo)
            blocks.append(jnp.where(lo, oe, oo))
        return jnp.concatenate(blocks, axis=1)

    def sink_col(n):
        row = lax.broadcasted_iota(jnp.int32, (4 * n, 1), 0)
        return jnp.where(row < n, sinks[0], jnp.where(row < 2 * n, sinks[1],
                         jnp.where(row < 3 * n, sinks[2], sinks[3])))

    def block_body(i, carry):
        q0 = pl.multiple_of(i * Q_BLK, Q_BLK)
        start = pl.multiple_of(jnp.clip((i - 1) * Q_BLK, 0, SEQ - span), Q_BLK)
        qs = stack_heads(q_ref[pl.ds(q0, Q_BLK), :])
        kw = k_ref[pl.ds(start, span), :]
        vw = v_ref[pl.ds(start, span), :]
        s_loc = _dot_nt(qs, kw)
        qpos = q0 + (lax.broadcasted_iota(jnp.int32, (4 * Q_BLK, 1), 0) & (Q_BLK - 1))
        kpos = start + _lane_iota((1, span))
        s_loc = jnp.where(jnp.abs(kpos - qpos) <= SW_WINDOW, s_loc, NEG)
        s_ctx = _dot_nt(qs, kc)
        o = _softmax_pv([s_loc, s_ctx], [vw, vc], sink_col(Q_BLK))
        o_ref[pl.ds(q0, Q_BLK), :] = unstack_heads(o, Q_BLK).astype(BF16)
        return carry

    lax.fori_loop(0, SEQ // Q_BLK, block_body, 0, unroll=4)
    qs = stack_heads(q_ref[SEQ:SL, :])
    o = _softmax_pv([_dot_nt(qs, kc)], [vc], sink_col(CTX))
    o_ref[SEQ:SL, :] = unstack_heads(o, CTX).astype(BF16)


def _sw_attention(q, k, v, sink2, b):
    nq = 4 * 64
    slabq = lambda: pl.BlockSpec((None, SL, nq), lambda i, g: (i, 0, g))
    slabk = lambda: pl.BlockSpec((None, SL, LANES), lambda i, g: (i, 0, g // 2))
    return pl.pallas_call(
        _sw_kernel,
        grid=(b, SW_KV_HEADS),
        in_specs=[pl.BlockSpec(memory_space=pltpu.SMEM), slabq(), slabk(), slabk()],
        out_specs=slabq(),
        out_shape=jax.ShapeDtypeStruct(q.shape, BF16),
        compiler_params=_cparams("parallel", "parallel"),
        name="sw_attention",
    )(sink2, q, k, v)


DENSE_TQ = 512
DENSE_TK = 1024


def _online_step(s, v, m_ref, l_ref, acc_ref):
    chunks = s.shape[1] // LANES
    m_prev = m_ref[...]
    m_new = jnp.maximum(m_prev, jnp.max(s, axis=-1, keepdims=True))
    alpha = jnp.exp2(m_prev - m_new)
    p = jnp.exp2(s - jnp.concatenate([m_new] * chunks, axis=1))
    psum = functools.reduce(lambda a, b: a + b, [p[:, c * LANES:(c + 1) * LANES] for c in range(chunks)])
    l_ref[...] = alpha * l_ref[...] + psum
    acc_ref[...] = alpha * acc_ref[...] + _dot(p.astype(BF16), v)
    m_ref[...] = m_new


def _dense_streams(streams, v_ref, n_lat, vc):
    views = []
    for qs, _, (m_ref, l_ref, acc_ref) in streams:
        rows = qs.shape[0]
        m_ref[0:rows, :] = jnp.full((rows, LANES), NEG, F32)
        l_ref[0:rows, :] = jnp.zeros((rows, LANES), F32)
        acc_ref[0:rows, :] = jnp.zeros((rows, LANES), F32)
        views.append((m_ref.at[0:rows, :], l_ref.at[0:rows, :], acc_ref.at[0:rows, :]))

    for c in range(n_lat // DENSE_TK):
        v = v_ref[c * DENSE_TK:(c + 1) * DENSE_TK, :]
        for (qs, k_ref, _), view in zip(streams, views):
            _online_step(_dot_nt(qs, k_ref[c * DENSE_TK:(c + 1) * DENSE_TK, :]), v, *view)
    outs = []
    for (qs, k_ref, _), (mr, lr, ar) in zip(streams, views):
        _online_step(_dot_nt(qs, k_ref[SEQ:SL, :]), vc, mr, lr, ar)
        outs.append(ar[...] * (1.0 / jnp.sum(lr[...], axis=-1, keepdims=True)))
    return outs


def _diff_kernel(lambda_init, lam_ref, subln_ref, q_ref, k_ref, v_ref, o_ref, m_ref, l_ref, acc_ref):
    lo = _lane_iota((1, LANES)) < 64
    lam = lam_ref[...]
    lam_full = (jnp.exp(jnp.sum(lam[0:1] * lam[1:2], axis=-1, keepdims=True))
                - jnp.exp(jnp.sum(lam[2:3] * lam[3:4], axis=-1, keepdims=True)) + lambda_init)
    vc = v_ref[SEQ:SL, :]
    gain = subln_ref[...] * (1.0 - lambda_init)

    def q_body(i, carry):
        q0 = pl.multiple_of(i * DENSE_TQ, DENSE_TQ)
        qa, qb = _split_pair(q_ref[pl.ds(q0, DENSE_TQ), :], lo)
        qs = jnp.concatenate([qa, qb], axis=0)
        o, = _dense_streams([(qs, k_ref, (m_ref, l_ref, acc_ref))], v_ref, SEQ, vc)
        od = o[:DENSE_TQ] - lam_full * o[DENSE_TQ:]
        od = od * lax.rsqrt(jnp.mean(od * od, axis=-1, keepdims=True) + EPS) * gain
        o_ref[pl.ds(q0, DENSE_TQ), :] = od.astype(BF16)
        return carry

    lax.fori_loop(0, SEQ // DENSE_TQ, q_body, 0)
    o_ref[SEQ:SL, :] = jnp.zeros((CTX, LANES), BF16)


def _diff_attention(q, k, v, lam, subln, lambda_init, b):
    slab = lambda: pl.BlockSpec((None, SL, LANES), lambda i, j: (i, 0, j))
    rows = 2 * DENSE_TQ
    return pl.pallas_call(
        functools.partial(_diff_kernel, lambda_init),
        grid=(b, DIFF_HEADS),
        in_specs=[pl.BlockSpec(lam.shape, lambda i, j: (0, 0)),
                  pl.BlockSpec((1, LANES), lambda i, j: (0, 0)),
                  slab(), slab(), slab()],
        out_specs=slab(),
        out_shape=jax.ShapeDtypeStruct(v.shape, BF16),
        scratch_shapes=[pltpu.VMEM((rows, LANES), F32)] * 3,
        compiler_params=_cparams("parallel", "parallel"),
        name="diff_attention",
    )(lam, subln.reshape(1, LANES), q, k, v)


def _mla_kernel(q_ref, k_ref, v_ref, o_ref, *scratch):
    lo = _lane_iota((1, LANES)) < 64
    vc = v_ref[SEQ:SL, :]

    def two_heads(q, n_lat):
        streams = []
        for hd in range(2):
            sl = slice(hd * LANES, (hd + 1) * LANES)
            streams.append((q[:, sl], k_ref.at[:, sl], scratch[3 * hd:3 * hd + 3]))
        outs = _dense_streams(streams, v_ref, n_lat, vc)
        return jnp.where(lo, outs[0], outs[1])

    def q_body(i, carry):
        q0 = pl.multiple_of(i * DENSE_TQ, DENSE_TQ)
        o = two_heads(q_ref[pl.ds(q0, DENSE_TQ), :], SEQ)
        o_ref[pl.ds(q0, DENSE_TQ), :] = o.astype(BF16)
        return carry

    lax.fori_loop(0, SEQ // DENSE_TQ, q_body, 0)
    o_ref[SEQ:SL, :] = two_heads(q_ref[SEQ:SL, :], 0).astype(BF16)


def _mla_attention(q, k, v, b):
    slabq = lambda: pl.BlockSpec((None, SL, 2 * LANES), lambda i, j: (i, 0, j))
    slabv = lambda: pl.BlockSpec((None, SL, LANES), lambda i, j: (i, 0, j))
    return pl.pallas_call(
        _mla_kernel,
        grid=(b, MLA_HEADS // 2),
        in_specs=[slabq(), slabq(), slabv()],
        out_specs=slabv(),
        out_shape=jax.ShapeDtypeStruct(v.shape, BF16),
        scratch_shapes=[pltpu.VMEM((DENSE_TQ, LANES), F32)] * 6,
        compiler_params=_cparams("parallel", "parallel"),
        name="mla_attention",
    )(q, k, v)


def _oproj_kernel(a_ref, x_ref, wo_ref, g1_ref, ng_ref, sc_ref, sh_ref, wr_hi_ref, wr_lo_ref, br_ref,
                  xo_ref, h_ref, route_ref, cnt_ref):
    y = _dot(a_ref[...], wo_ref[...])
    x = x_ref[...] + g1_ref[...] * y
    xo_ref[...] = x
    h = _modulated(x, ng_ref[...], sc_ref[...], sh_ref[...])
    hi = h.astype(BF16)
    h_ref[...] = hi
    lo = (h - hi.astype(F32)).astype(BF16)
    z = _dot(hi, wr_hi_ref[...]) + _dot(lo, wr_hi_ref[...]) + _dot(hi, wr_lo_ref[...]) + br_ref[...]
    lane = _lane_iota(z.shape)
    lane_f = lane.astype(F32)
    big = jnp.float32(1e9)
    is_g = lane < N_GROUPS
    zg = jnp.where(is_g, z, NEG)
    gmax = jnp.max(zg, axis=-1, keepdims=True)
    gsum = jnp.sum(jnp.where(is_g, jnp.exp(zg - gmax), 0.0), axis=-1, keepdims=True)
    g_w = 1.0 / gsum
    g_idx = jnp.min(jnp.where(is_g & (zg == gmax), lane_f, big), axis=-1, keepdims=True)
    e_lo = N_GROUPS + EXPERTS_PER_GROUP * g_idx
    is_e = (lane_f >= e_lo) & (lane_f < e_lo + EXPERTS_PER_GROUP)
    ze = jnp.where(is_e, z, NEG)
    z1 = jnp.max(ze, axis=-1, keepdims=True)
    i1 = jnp.min(jnp.where(is_e & (ze == z1), lane_f, big), axis=-1, keepdims=True)
    ze2 = jnp.where(lane_f == i1, NEG, ze)
    z2 = jnp.max(ze2, axis=-1, keepdims=True)
    i2 = jnp.min(jnp.where(is_e & (ze2 == z2) & (lane_f != i1), lane_f, big), axis=-1, keepdims=True)
    r = jnp.exp(z2 - z1)
    gate1 = g_w / (1.0 + r)
    gate2 = g_w * r / (1.0 + r)
    e1 = i1 - N_GROUPS
    e2 = i2 - N_GROUPS
    route = jnp.where(lane == 0, gate1, jnp.where(lane == 1, gate2,
                      jnp.where(lane == 2, e1, jnp.where(lane == 3, e2, 0.0))))
    route_ref[...] = route

    @pl.when(pl.program_id(0) == 0)
    def _():
        cnt_ref[...] = jnp.zeros_like(cnt_ref)

    hits = jnp.where(lane_f == e1, 1.0, 0.0) + jnp.where(lane_f == e2, 1.0, 0.0)
    cnt_ref[...] += jnp.sum(hits, axis=0, keepdims=True)


def _route_pos_kernel(route_ref, base_ref, pos_ref, carry_ref):
    @pl.when(pl.program_id(0) == 0)
    def _():
        carry_ref[...] = base_ref[...]

    route = route_ref[...]
    lane_f = _lane_iota(route.shape).astype(F32)
    oh0 = jnp.where(lane_f == route[:, 2:3], 1.0, 0.0)
    oh1 = jnp.where(lane_f == route[:, 3:4], 1.0, 0.0)
    r = lax.broadcasted_iota(jnp.int32, (TM, TM), 0)
    c = lax.broadcasted_iota(jnp.int32, (TM, TM), 1)
    tri = jnp.where(c <= r, 1.0, 0.0).astype(BF16)
    carry = carry_ref[...]
    cnt0 = jnp.sum(oh0, axis=0, keepdims=True)
    pre0 = _dot(tri, oh0.astype(BF16)) - 1.0 + carry
    pre1 = _dot(tri, oh1.astype(BF16)) - 1.0 + carry + cnt0
    pos0 = jnp.sum(oh0 * pre0, axis=-1, keepdims=True)
    pos1 = jnp.sum(oh1 * pre1, axis=-1, keepdims=True)
    lane = _lane_iota(route.shape)
    pos_ref[...] = jnp.where(lane == 0, pos0, jnp.where(lane == 1, pos1, 0.0))
    carry_ref[...] = carry + cnt0 + jnp.sum(oh1, axis=0, keepdims=True)


def _route_pos(route, base):
    t = route.shape[0]
    return pl.pallas_call(
        _route_pos_kernel,
        grid=(t // TM,),
        in_specs=[pl.BlockSpec((TM, LANES), lambda i: (i, 0)), pl.BlockSpec((1, LANES), lambda i: (0, 0))],
        out_specs=pl.BlockSpec((TM, LANES), lambda i: (i, 0)),
        out_shape=jax.ShapeDtypeStruct((t, LANES), F32),
        scratch_shapes=[pltpu.VMEM((1, LANES), F32)],
        compiler_params=_cparams("arbitrary"),
        name="route_pos",
    )(route, base)


def _oproj_router(attn2, x2, mod3, wo_bf, ng, wr_hi, wr_lo, br):
    t, d = x2.shape
    n_in = attn2.shape[1]
    full = lambda a: pl.BlockSpec(a.shape, lambda i: (0,) * a.ndim)
    row = lambda w: pl.BlockSpec((TM, w), lambda i: (i, 0))
    return pl.pallas_call(
        _oproj_kernel,
        grid=(t // TM,),
        in_specs=[row(n_in), row(d), full(wo_bf), _mod_spec(2), pl.BlockSpec((1, d), lambda i: (0, 0)),
                  _mod_spec(4), _mod_spec(3), full(wr_hi), full(wr_lo), full(br)],
        out_specs=[row(d), row(d), row(LANES), pl.BlockSpec((1, LANES), lambda i: (0, 0))],
        out_shape=[jax.ShapeDtypeStruct((t, d), F32), jax.ShapeDtypeStruct((t, d), BF16),
                   jax.ShapeDtypeStruct((t, LANES), F32), jax.ShapeDtypeStruct((1, LANES), F32)],
        compiler_params=_cparams("arbitrary"),
        name="oproj_router",
    )(attn2, x2, wo_bf, mod3, ng.reshape(1, d), mod3, mod3, wr_hi, wr_lo, br)


def _expert_kernel(be_ref, nu_ref, x_ref, wgu_ref, wd_ref, o_ref, wgu_bf, wd_bf):
    i = pl.program_id(0)
    new_expert = (i == 0) | (be_ref[i] != be_ref[jnp.maximum(i - 1, 0)])

    @pl.when(new_expert)
    def _():
        wgu_bf[...] = wgu_ref[...].astype(BF16)
        wd_bf[...] = wd_ref[...].astype(BF16)

    @pl.when(i < nu_ref[0])
    def _():
        gu = _dot(x_ref[...], wgu_bf[...])
        gte, up = gu[:, :D_EXPERT], gu[:, D_EXPERT:]
        a = (gte * jax.nn.sigmoid(gte) * up).astype(BF16)
        o_ref[...] = _dot(a, wd_bf[...]).astype(BF16)

    @pl.when(i >= nu_ref[0])
    def _():
        o_ref[...] = jnp.zeros_like(o_ref)


def _experts(xs, block_expert, n_used, wgu, wd, layer):
    n_rows, d = xs.shape
    n_blocks = n_rows // MOE_BLK
    return pl.pallas_call(
        _expert_kernel,
        grid_spec=pltpu.PrefetchScalarGridSpec(
            num_scalar_prefetch=2,
            grid=(n_blocks,),
            in_specs=[pl.BlockSpec((MOE_BLK, d), lambda i, be, nu: (i, 0)),
                      pl.BlockSpec((None, None, d, 2 * D_EXPERT), lambda i, be, nu: (layer, be[i], 0, 0)),
                      pl.BlockSpec((None, None, D_EXPERT, d), lambda i, be, nu: (layer, be[i], 0, 0))],
            out_specs=pl.BlockSpec((MOE_BLK, d), lambda i, be, nu: (i, 0)),
            scratch_shapes=[pltpu.VMEM((d, 2 * D_EXPERT), BF16), pltpu.VMEM((D_EXPERT, d), BF16)]),
        out_shape=jax.ShapeDtypeStruct((n_rows, d), BF16),
        compiler_params=_cparams("arbitrary"),
        name="experts",
    )(block_expert, n_used, xs, wgu, wd)


def _combine_kernel(x_ref, y0_ref, y1_ref, route_ref, g2_ref, o_ref):
    route = route_ref[...]
    y = route[:, 0:1] * y0_ref[...].astype(F32) + route[:, 1:2] * y1_ref[...].astype(F32)
    o_ref[...] = x_ref[...] + g2_ref[...] * y


def _combine(x2, y0, y1, route, mod3):
    t, d = x2.shape
    row = lambda w: pl.BlockSpec((TM, w), lambda i: (i, 0))
    return pl.pallas_call(
        _combine_kernel,
        grid=(t // TM,),
        in_specs=[row(d), row(d), row(d), row(LANES), _mod_spec(5)],
        out_specs=row(d),
        out_shape=jax.ShapeDtypeStruct((t, d), F32),
        compiler_params=_cparams("parallel"),
        name="moe_combine",
    )(x2, y0, y1, route, mod3)


def _moe_dispatch(route, counts_f, t):
    counts = counts_f[0, :N_EXPERTS].astype(jnp.int32)
    padded = (counts + MOE_BLK - 1) // MOE_BLK * MOE_BLK
    seg_end = jnp.cumsum(padded)
    seg_start = seg_end - padded
    base = jnp.zeros((1, LANES), F32).at[0, :N_EXPERTS].set(seg_start.astype(F32))
    pos_f = _route_pos(route, base)
    pos = pos_f[:, :2].T.astype(jnp.int32)
    n_rows = -(-(2 * t + N_EXPERTS * (MOE_BLK - 1)) // MOE_BLK) * MOE_BLK
    n_blocks = n_rows // MOE_BLK
    token = jnp.tile(jnp.arange(t, dtype=jnp.int32), 2)
    row_token = jnp.zeros((n_rows,), jnp.int32).at[pos.reshape(-1)].set(
        token, mode="promise_in_bounds", unique_indices=True)
    block_start = jnp.arange(n_blocks, dtype=jnp.int32) * MOE_BLK
    block_expert = jnp.minimum(jnp.sum((seg_end[None, :] <= block_start[:, None]).astype(jnp.int32), axis=1),
                               N_EXPERTS - 1)
    n_used = (seg_end[-1] // MOE_BLK).astype(jnp.int32).reshape(1)
    return row_token, pos, block_expert, n_used


def _take_rows(a, idx):
    return a.at[idx].get(mode="promise_in_bounds")


def _moe(x2, h_bf, route, counts_f, mod3, wgu, wd, layer):
    t = x2.shape[0]
    row_token, pos, block_expert, n_used = _moe_dispatch(route, counts_f, t)
    xs = _take_rows(h_bf, row_token)
    ys = _experts(xs, block_expert, n_used, wgu, wd, layer)
    y0 = _take_rows(ys, pos[0])
    y1 = _take_rows(ys, pos[1])
    return _combine(x2, y0, y1, route, mod3)


def kernel(x, c, ctx, c_ctx, ada_w, ada_b, norm_g, moe_w_router_group, moe_b_router_group, moe_w_router_expert, moe_b_router_expert, moe_w_gate_up, moe_w_down, na_w_qkv, na_w_o, na_q_norm, na_k_norm, na_rpb, sw_w_qkv, sw_w_o, sw_q_norm, sw_k_norm, sw_sink, mla_w_dqkv, mla_q_a_norm, mla_kv_a_norm, mla_w_uq, mla_w_ukv, mla_q_norm, mla_k_norm, mla_w_o, diff_w_qkv, diff_q_norm, diff_k_norm, diff_lambda, diff_subln, diff_w_o):
    slab = _stack(x, c, ctx, c_ctx, ada_w, ada_b, norm_g, moe_w_router_group, moe_b_router_group, moe_w_router_expert, moe_b_router_expert, moe_w_gate_up, moe_w_down, na_w_qkv, na_w_o, na_q_norm, na_k_norm, na_rpb, sw_w_qkv, sw_w_o, sw_q_norm, sw_k_norm, sw_sink, mla_w_dqkv, mla_q_a_norm, mla_kv_a_norm, mla_w_uq, mla_w_ukv, mla_q_norm, mla_k_norm, mla_w_o, diff_w_qkv, diff_q_norm, diff_k_norm, diff_lambda, diff_subln, diff_w_o)
    return slab[:, :SEQ]


def _stack(x, c, ctx, c_ctx, ada_w, ada_b, norm_g, moe_w_router_group, moe_b_router_group, moe_w_router_expert, moe_b_router_expert, moe_w_gate_up, moe_w_down, na_w_qkv, na_w_o, na_q_norm, na_k_norm, na_rpb, sw_w_qkv, sw_w_o, sw_q_norm, sw_k_norm, sw_sink, mla_w_dqkv, mla_q_a_norm, mla_kv_a_norm, mla_w_uq, mla_w_ukv, mla_q_norm, mla_k_norm, mla_w_o, diff_w_qkv, diff_q_norm, diff_k_norm, diff_lambda, diff_subln, diff_w_o):
    b, s, d = x.shape
    assert (s, d) == (SEQ, D_MODEL) and ctx.shape == (b, CTX, d) and b <= 8
    depth = ada_w.shape[0]
    assert depth <= 4
    t = b * SL

    cc = jnp.zeros((16, d), F32).at[:b].set(c).at[8].set(c_ctx)
    mod = _ada_table(cc, ada_w, ada_b)
    x2 = jnp.concatenate([x, ctx], axis=1).reshape(t, d)

    lane = np.arange(LANES)
    bd64 = _block_diag(lane // 64)
    lm64 = lane % 64
    rope64 = _rope_tables(64, lm64)
    scale64 = 64 ** -0.5 * LOG2E

    r3 = lambda a: a.reshape(b, SL, a.shape[-1])
    for l in range(depth):
        mod3 = mod[l].reshape(16 * 6, 1, d)
        if l == 0:
            w = na_w_qkv[0].astype(BF16)
            q, k, v = _project(x2, mod3, norm_g[l, 0], w, _qkv_plan(1024, 1024, 1024), (1024, 1024, 1024),
                               bd64, _head64_vec(na_q_norm[0], na_k_norm[0], scale64), name="na_proj")
            attn = _na_attention(r3(q), r3(k), r3(v), _na_bias_table(na_rpb[0]), b)
            w_o = na_w_o[0]
        elif l == 1:
            w = sw_w_qkv[0].astype(BF16)
            q, k, v = _project(x2, mod3, norm_g[l, 0], w, _qkv_plan(1024, 256, 256), (1024, 256, 256),
                               bd64, _head64_vec(sw_q_norm[0], sw_k_norm[0], scale64),
                               rope=rope64, rope_half=32, name="sw_proj")
            attn = _sw_attention(r3(q), r3(k), r3(v), sw_sink[0] * LOG2E, b)
            w_o = sw_w_o[0]
        elif l == 2:
            p = dict(w_dqkv=mla_w_dqkv[0], q_a_norm=mla_q_a_norm[0], kv_a_norm=mla_kv_a_norm[0],
                     w_uq=mla_w_uq[0], w_ukv=mla_w_ukv[0], q_norm=mla_q_norm[0], k_norm=mla_k_norm[0])
            q, k, v = _mla_project(x2, mod3, norm_g[l, 0], p)
            attn = _mla_attention(r3(q), r3(k), r3(v), b)
            w_o = mla_w_o[0]
        else:
            w = diff_w_qkv[0].astype(BF16)
            q, k, v = _project(x2, mod3, norm_g[l, 0], w, _qkv_plan(1024, 1024, 1024), (1024, 1024, 1024),
                               bd64, _head64_vec(diff_q_norm[0], diff_k_norm[0], scale64),
                               rope=rope64, rope_half=32, name="diff_proj")
            lambda_init = 0.8 - 0.6 * math.exp(-0.3 * l)
            attn = _diff_attention(r3(q), r3(k), r3(v), diff_lambda[0], diff_subln[0], lambda_init, b)
            w_o = diff_w_o[0]

        wr = jnp.concatenate([moe_w_router_group[l], moe_w_router_expert[l]], axis=1)
        wr = jnp.pad(wr, ((0, 0), (0, LANES - wr.shape[1])))
        wr_hi = wr.astype(BF16)
        wr_lo = (wr - wr_hi.astype(F32)).astype(BF16)
        br = jnp.concatenate([moe_b_router_group[l], moe_b_router_expert[l]])
        br = jnp.pad(br, (0, LANES - br.shape[0])).reshape(1, LANES)
        x2, h_bf, route, counts_f = _oproj_router(attn.reshape(t, -1), x2, mod3, w_o.astype(BF16), norm_g[l, 1],
                                                  wr_hi, wr_lo, br)
        x2 = _moe(x2, h_bf, route, counts_f, mod3, moe_w_gate_up, moe_w_down, l)

    return x2.reshape(b, SL, d)
```

```python
import functools
import math

import jax
import jax.numpy as jnp
import numpy as np
from jax import lax
from jax.experimental import pallas as pl
from jax.experimental.pallas import tpu as pltpu

F32 = jnp.float32
BF16 = jnp.bfloat16

D_MODEL = 1024
SEQ = 4096
CTX = 256
SL = SEQ + CTX
GRID_W = 64
GRID_ROWS = SEQ // GRID_W
EPS = 1e-6
ROPE_BASE = 10000.0
NEG = -1e30
LOG2E = 1.4426950408889634

NA_KH, NA_KW = 8, 16
SW_WINDOW = 128
SW_HEADS, SW_KV_HEADS = 16, 4
MLA_Q_RANK, MLA_KV_RANK, MLA_NOPE, MLA_ROPE, MLA_V, MLA_HEADS = 384, 256, 64, 32, 64, 16
DIFF_HEADS = 8
N_GROUPS, EXPERTS_PER_GROUP, N_EXPERTS, D_EXPERT = 4, 8, 32, 256

LANES = 128
MXU_N = 256
TM = 256
TILES_PER_SLAB = SL // TM
MOE_BLK = 256
VMEM_LIMIT = 56 * 1024 * 1024


def _cparams(*sem):
    return pltpu.CompilerParams(dimension_semantics=sem, vmem_limit_bytes=VMEM_LIMIT)


def _lane_iota(shape):
    return lax.broadcasted_iota(jnp.int32, shape, len(shape) - 1)


def _dot(a, b):
    return jnp.dot(a, b, preferred_element_type=F32)


def _dot_nt(a, b):
    return lax.dot_general(a, b, (((1,), (1,)), ((), ())), preferred_element_type=F32)


def _ada_kernel(cc_ref, w_ref, b_ref, o_ref):
    a = cc_ref[...]
    a = a * jax.nn.sigmoid(a)
    hi = a.astype(BF16)
    lo = (a - hi.astype(F32)).astype(BF16)
    w = w_ref[...]
    whi = w.astype(BF16)
    wlo = (w - whi.astype(F32)).astype(BF16)
    o_ref[...] = _dot(hi, whi) + _dot(lo, whi) + _dot(hi, wlo) + b_ref[...]


def _ada_table(cc, ada_w, ada_b):
    depth, d, n = ada_w.shape
    tn = 1024
    return pl.pallas_call(
        _ada_kernel,
        grid=(depth, n // tn),
        in_specs=[pl.BlockSpec((16, d), lambda l, j: (0, 0)),
                  pl.BlockSpec((None, d, tn), lambda l, j: (l, 0, j)),
                  pl.BlockSpec((None, 1, tn), lambda l, j: (l, 0, j))],
        out_specs=pl.BlockSpec((None, 16, tn), lambda l, j: (l, 0, j)),
        out_shape=jax.ShapeDtypeStruct((depth, 16, n), F32),
        compiler_params=_cparams("parallel", "parallel"),
        name="ada_table",
    )(cc, ada_w, ada_b.reshape(depth, 1, n))


def _mod_spec(part):
    def index(i):
        m = jnp.where(i % TILES_PER_SLAB == TILES_PER_SLAB - 1, 8, i // TILES_PER_SLAB)
        return (m * 6 + part, 0, 0)
    return pl.BlockSpec((None, 1, D_MODEL), index)


def _modulated(x, g, sc, sh):
    ms = jnp.mean(x * x, axis=-1, keepdims=True)
    return (x * lax.rsqrt(ms + EPS)) * g * (1.0 + sc) + sh


def _group_norm_rope(y, bd, invcnt, gain, cos=None, sin=None, half=None):
    ss = _dot((y * y).astype(BF16), bd)
    yn = y * lax.rsqrt(ss * invcnt + EPS) * gain
    if cos is not None:
        lane = _lane_iota(yn.shape)
        fwd = pltpu.roll(yn, LANES - half, 1)
        bwd = pltpu.roll(yn, half, 1)
        partner = jnp.where((lane & (2 * half - 1)) < half, fwd, bwd)
        yn = yn * cos + partner * sin
    return yn


def _block_diag(group_of_lane):
    g = np.asarray(group_of_lane)
    return jnp.asarray((g[:, None] == g[None, :]).astype(np.float32), dtype=BF16)


def _rope_tables(rot_dim, lane_map):
    t = jnp.arange(SEQ, dtype=jnp.int32)
    n_freq = rot_dim // 4
    inv_freq = ROPE_BASE ** (-jnp.arange(n_freq, dtype=F32) / n_freq)
    rows = (t // GRID_W).astype(F32)
    cols = (t % GRID_W).astype(F32)
    ang = jnp.concatenate([rows[:, None] * inv_freq, cols[:, None] * inv_freq], axis=-1)
    cos, sin = jnp.cos(ang), jnp.sin(ang)
    lm = np.asarray(lane_map)
    rot = lm >= 0
    idx = np.where(rot, lm % (rot_dim // 2), 0)
    sign = np.where(lm < rot_dim // 2, -1.0, 1.0).astype(np.float32)
    cos_t = jnp.where(rot[None, :], cos[:, idx], 1.0)
    sin_t = jnp.where(rot[None, :], sin[:, idx] * sign[None, :], 0.0)
    ident_c = jnp.ones((CTX, LANES), F32)
    ident_s = jnp.zeros((CTX, LANES), F32)
    return jnp.concatenate([cos_t, ident_c], 0), jnp.concatenate([sin_t, ident_s], 0)


def _proj_kernel(plan, rope_half, x_ref, g_ref, sc_ref, sh_ref, w_ref, bd_ref, vec_ref, *rest):
    if rope_half is not None:
        cos_ref, sin_ref = rest[:2]
        outs = rest[2:]
        cos, sin = cos_ref[...], sin_ref[...]
    else:
        outs = rest
        cos = sin = None
    h = _modulated(x_ref[...], g_ref[...], sc_ref[...], sh_ref[...]).astype(BF16)
    bd = bd_ref[...]
    for c0, ep, out_idx, oc0 in plan:
        acc = _dot(h, w_ref[:, c0:c0 + MXU_N])
        for s in range(MXU_N // LANES):
            y = acc[:, s * LANES:(s + 1) * LANES]
            if ep is not None:
                y = _group_norm_rope(y, bd, vec_ref[ep, 0:1, :], vec_ref[ep, 1:2, :], cos, sin, rope_half)
            lo = oc0 + s * LANES
            outs[out_idx][:, lo:lo + LANES] = y.astype(BF16)


def _project(x2, mod3, g, w_bf, plan, out_widths, bd, vec, rope=None, rope_half=None, name="proj"):
    t, d = x2.shape
    n = w_bf.shape[1]
    n_tiles = t // TM
    in_specs = [pl.BlockSpec((TM, d), lambda i: (i, 0)),
                pl.BlockSpec((1, d), lambda i: (0, 0)),
                _mod_spec(1), _mod_spec(0),
                pl.BlockSpec((d, n), lambda i: (0, 0)),
                pl.BlockSpec(bd.shape, lambda i: (0, 0)),
                pl.BlockSpec(vec.shape, lambda i: (0, 0, 0))]
    args = [x2, g.reshape(1, d), mod3, mod3, w_bf, bd, vec]
    if rope is not None:
        in_specs += [pl.BlockSpec((TM, LANES), lambda i: (i % TILES_PER_SLAB, 0))] * 2
        args += list(rope)
    return pl.pallas_call(
        functools.partial(_proj_kernel, plan, rope_half),
        grid=(n_tiles,),
        in_specs=in_specs,
        out_specs=[pl.BlockSpec((TM, w), lambda i: (i, 0)) for w in out_widths],
        out_shape=[jax.ShapeDtypeStruct((t, w), BF16) for w in out_widths],
        compiler_params=_cparams("parallel"),
        name=name,
    )(*args)


def _qkv_plan(nq, nk, nv):
    plan = []
    for c0 in range(0, nq, MXU_N):
        plan.append((c0, 0, 0, c0))
    for c0 in range(0, nk, MXU_N):
        plan.append((nq + c0, 1, 1, c0))
    for c0 in range(0, nv, MXU_N):
        plan.append((nq + nk + c0, None, 2, c0))
    return tuple(plan)


def _head64_vec(q_norm, k_norm, q_scale):
    inv = jnp.full((LANES,), 1.0 / 64, F32)
    z = jnp.zeros((6, LANES), F32)
    vq = jnp.concatenate([inv[None], (jnp.tile(q_norm, 2) * q_scale)[None], z], 0)
    vk = jnp.concatenate([inv[None], jnp.tile(k_norm, 2)[None], z], 0)
    return jnp.stack([vq, vk], 0)


def _mla_proj_kernel(x_ref, g_ref, sc_ref, sh_ref, wd_ref, wuq_ref, wuk_ref, wuv_ref, bd_ref, vec_ref,
                     an_ref, cos_ref, sin_ref, q_out, k_out, v_out):
    h = _modulated(x_ref[...], g_ref[...], sc_ref[...], sh_ref[...]).astype(BF16)
    cos, sin = cos_ref[...], sin_ref[...]
    half = MLA_ROPE // 2
    cq = _dot(h, wd_ref[:, 0:MLA_Q_RANK])
    ckv = _dot(h, wd_ref[:, MLA_Q_RANK:MLA_Q_RANK + MLA_KV_RANK])
    kr = _dot(h, wd_ref[:, MLA_Q_RANK + MLA_KV_RANK:])
    cq = cq * lax.rsqrt(jnp.mean(cq * cq, axis=-1, keepdims=True) + EPS) * an_ref[0:1, 0:MLA_Q_RANK]
    ckv = ckv * lax.rsqrt(jnp.mean(ckv * ckv, axis=-1, keepdims=True) + EPS) * an_ref[1:2, 0:MLA_KV_RANK]
    cq = cq.astype(BF16)
    ckv = ckv.astype(BF16)
    kr = _group_norm_rope(kr, bd_ref[1], vec_ref[2, 0:1, :], vec_ref[2, 1:2, :], cos[:, LANES:], sin[:, LANES:], half)
    kr = pltpu.roll(kr, MLA_NOPE, 1)
    for hd in range(MLA_HEADS):
        c0 = hd * LANES
        qh = _dot(cq, wuq_ref[:, c0:c0 + LANES])
        qh = _group_norm_rope(qh, bd_ref[0], vec_ref[0, 0:1, :], vec_ref[0, 1:2, :], cos[:, :LANES], sin[:, :LANES], half)
        q_out[:, c0:c0 + LANES] = qh.astype(BF16)
        kh = _dot(ckv, wuk_ref[:, c0:c0 + LANES])
        kh = _group_norm_rope(kh, bd_ref[0], vec_ref[1, 0:1, :], vec_ref[1, 1:2, :]) + kr
        k_out[:, c0:c0 + LANES] = kh.astype(BF16)
    for c0 in range(0, MLA_HEADS * MLA_V, MXU_N):
        v_out[:, c0:c0 + MXU_N] = _dot(ckv, wuv_ref[:, c0:c0 + MXU_N]).astype(BF16)


def _mla_project(x2, mod3, g, p):
    t, d = x2.shape
    hq = MLA_NOPE + MLA_ROPE
    n_tiles = t // TM
    wd = jnp.pad(p["w_dqkv"], ((0, 0), (0, LANES - MLA_ROPE))).astype(BF16)
    wuq = p["w_uq"].reshape(MLA_Q_RANK, MLA_HEADS, hq)
    wuq = jnp.pad(wuq, ((0, 0), (0, 0), (0, LANES - hq))).reshape(MLA_Q_RANK, MLA_HEADS * LANES).astype(BF16)
    wukv = p["w_ukv"].reshape(MLA_KV_RANK, MLA_HEADS, MLA_NOPE + MLA_V)
    wuk = jnp.pad(wukv[:, :, :MLA_NOPE], ((0, 0), (0, 0), (0, LANES - MLA_NOPE)))
    wuk = wuk.reshape(MLA_KV_RANK, MLA_HEADS * LANES).astype(BF16)
    wuv = wukv[:, :, MLA_NOPE:].reshape(MLA_KV_RANK, MLA_HEADS * MLA_V).astype(BF16)
    scale = (MLA_NOPE + MLA_ROPE) ** -0.5 * LOG2E
    lane = np.arange(LANES)
    grp_head = np.where(lane < 64, 0, np.where(lane < 96, 1, 2))
    grp_kr = np.where(lane < 32, 0, 1)
    bd = jnp.stack([_block_diag(grp_head), _block_diag(grp_kr)], 0)
    inv_head = jnp.asarray(np.where(lane < 64, 1 / 64, 1 / 32), F32)
    zpad = jnp.zeros((LANES - hq,), F32)
    gq = jnp.concatenate([p["q_norm"], zpad]) * scale
    gk = jnp.concatenate([p["k_norm"][:MLA_NOPE], jnp.zeros((LANES - MLA_NOPE,), F32)])
    gkr = jnp.concatenate([p["k_norm"][MLA_NOPE:], jnp.zeros((LANES - MLA_ROPE,), F32)])
    inv_kr = jnp.asarray(np.where(lane < 32, 1 / 32, 1 / 96), F32)
    z6 = jnp.zeros((6, LANES), F32)
    vec = jnp.stack([jnp.concatenate([inv_head[None], gq[None], z6], 0),
                     jnp.concatenate([inv_head[None], gk[None], z6], 0),
                     jnp.concatenate([inv_kr[None], gkr[None], z6], 0)], 0)
    an = jnp.zeros((8, MLA_Q_RANK), F32)
    an = an.at[0].set(p["q_a_norm"]).at[1, :MLA_KV_RANK].set(p["kv_a_norm"])
    lm_head = np.where((lane >= 64) & (lane < 96), lane - 64, -1)
    lm_kr = np.where(lane < 32, lane, -1)
    ch, sh_ = _rope_tables(MLA_ROPE, lm_head)
    ck, sk = _rope_tables(MLA_ROPE, lm_kr)
    cos2 = jnp.concatenate([ch, ck], 1)
    sin2 = jnp.concatenate([sh_, sk], 1)

    full = lambda a: pl.BlockSpec(a.shape, lambda i: (0,) * a.ndim)
    rope_spec = pl.BlockSpec((TM, 2 * LANES), lambda i: (i % TILES_PER_SLAB, 0))
    out_w = (MLA_HEADS * LANES, MLA_HEADS * LANES, MLA_HEADS * MLA_V)
    return pl.pallas_call(
        _mla_proj_kernel,
        grid=(n_tiles,),
        in_specs=[pl.BlockSpec((TM, d), lambda i: (i, 0)), pl.BlockSpec((1, d), lambda i: (0, 0)),
                  _mod_spec(1), _mod_spec(0), full(wd), full(wuq), full(wuk), full(wuv), full(bd), full(vec),
                  full(an), rope_spec, rope_spec],
        out_specs=[pl.BlockSpec((TM, w), lambda i: (i, 0)) for w in out_w],
        out_shape=[jax.ShapeDtypeStruct((t, w), BF16) for w in out_w],
        compiler_params=_cparams("parallel"),
        name="mla_proj",
    )(x2, g.reshape(1, d), mod3, mod3, wd, wuq, wuk, wuv, bd, vec, an, cos2, sin2)


def _softmax_pv(s_list, v_list, sink=None):
    m = functools.reduce(jnp.maximum, [jnp.max(s, axis=-1, keepdims=True) for s in s_list])
    if sink is not None:
        m = jnp.maximum(m, sink)
    ps = [jnp.exp2(s - m) for s in s_list]
    l = functools.reduce(lambda a, b: a + b, [jnp.sum(p, axis=-1, keepdims=True) for p in ps])
    if sink is not None:
        l = l + jnp.exp2(sink - m)
    o = functools.reduce(lambda a, b: a + b, [_dot(p.astype(BF16), v) for p, v in zip(ps, v_list)])
    return o * (1.0 / l)


def _split_pair(x, lo):
    zero = jnp.zeros_like(x)
    return jnp.where(lo, x, zero), jnp.where(lo, zero, x)


def _na_kernel(q_ref, k_ref, v_ref, bias_ref, o_ref):
    lo = _lane_iota((1, LANES)) < 64
    kc = k_ref[SEQ:SL, :]
    vc = v_ref[SEQ:SL, :]
    nb = NA_KH * GRID_W

    def pair_attend(q, ks, vs, bias):
        n = q.shape[0]
        qa, qb = _split_pair(q, lo)
        qs = jnp.concatenate([qa, qb], axis=0)
        s_list = [_dot_nt(qs, k) for k in ks]
        if bias is not None:
            s_list[0] = s_list[0] + bias
        o = _softmax_pv(s_list, vs)
        return jnp.where(lo, o[:n], o[n:])

    def row_body(r, carry):
        row0 = jnp.clip(r - NA_KH // 2, 0, GRID_ROWS - NA_KH)
        q = q_ref[pl.ds(pl.multiple_of(r * GRID_W, GRID_W), GRID_W), :]
        ks = pl.multiple_of(row0 * GRID_W, GRID_W)
        kn = k_ref[pl.ds(ks, nb), :]
        vn = v_ref[pl.ds(ks, nb), :]
        o = pair_attend(q, [kn, kc], [vn, vc], bias_ref[r - row0])
        o_ref[pl.ds(pl.multiple_of(r * GRID_W, GRID_W), GRID_W), :] = o.astype(BF16)
        return carry

    lax.fori_loop(0, GRID_ROWS, row_body, 0, unroll=8)
    o_ref[SEQ:SL, :] = pair_attend(q_ref[SEQ:SL, :], [kc], [vc], None).astype(BF16)


def _na_bias_table(rpb):
    heads = rpb.shape[0]
    n_col = 2 * NA_KW - 1
    r2 = rpb.reshape(heads, 2 * NA_KH - 1, n_col) * LOG2E
    padded = jnp.pad(r2, ((0, 0), (0, 0), (GRID_W, GRID_W)))
    base = GRID_W + NA_KW - 1
    toeplitz = jnp.stack([padded[:, :, base - qc:base - qc + GRID_W] for qc in range(GRID_W)], axis=2)
    qc = np.arange(GRID_W)[:, None, None]
    kc = np.arange(GRID_W)[None, None, :]
    col0 = np.clip(qc - NA_KW // 2, 0, GRID_W - NA_KW)
    valid = np.broadcast_to((kc >= col0) & (kc < col0 + NA_KW), (GRID_W, NA_KH, GRID_W))
    valid = valid.reshape(GRID_W, NA_KH * GRID_W)
    classes = []
    for c in range(NA_KH):
        t = toeplitz[:, NA_KH - 1 - c:2 * NA_KH - 1 - c]
        t = t.transpose(0, 2, 1, 3).reshape(heads, GRID_W, NA_KH * GRID_W)
        classes.append(jnp.where(valid[None], t, NEG))
    tab = jnp.stack(classes, 0)
    return tab.reshape(NA_KH, heads // 2, 2 * GRID_W, NA_KH * GRID_W)


def _na_attention(q, k, v, bias, b):
    hp = q.shape[-1] // LANES
    slab = lambda: pl.BlockSpec((None, SL, LANES), lambda i, j: (i, 0, j))
    return pl.pallas_call(
        _na_kernel,
        grid=(b, hp),
        in_specs=[slab(), slab(), slab(),
                  pl.BlockSpec((NA_KH, None, 2 * GRID_W, NA_KH * GRID_W), lambda i, j: (0, j, 0, 0))],
        out_specs=slab(),
        out_shape=jax.ShapeDtypeStruct(q.shape, BF16),
        compiler_params=_cparams("parallel", "parallel"),
        name="na_attention",
    )(q, k, v, bias)


Q_BLK = 128


def _sw_kernel(sink_ref, q_ref, k_ref, v_ref, o_ref):
    g = pl.program_id(1)
    lane = _lane_iota((1, LANES))
    lo = lane < 64
    first_half = (jnp.zeros((1, LANES), jnp.int32) + (g % 2)) == 0
    target = (lane // 64) == (g % 2)
    kc = k_ref[SEQ:SL, :]
    vc = v_ref[SEQ:SL, :]
    span = Q_BLK + 2 * SW_WINDOW
    sinks = [sink_ref[4 * g + t] for t in range(4)]

    def swap(x):
        return jnp.concatenate([x[:, 64:], x[:, :64]], axis=1)

    def stack_heads(q):
        parts = []
        for blk in range(2):
            qb = q[:, blk * LANES:(blk + 1) * LANES]
            qs = swap(qb)
            even = jnp.where(first_half, qb, qs)
            odd = jnp.where(first_half, qs, qb)
            zero = jnp.zeros_like(qb)
            parts += [jnp.where(target, even, zero), jnp.where(target, odd, zero)]
        return jnp.concatenate(parts, axis=0)

    def unstack_heads(o, n):
        blocks = []
        for blk in range(2):
            oe = o[(2 * blk) * n:(2 * blk + 1) * n]
            oo = o[(2 * blk + 1) * n:(2 * blk + 2) * n]
            oe = jnp.where(first_half, oe, pltpu.roll(oe, 64, 1))
            oo = jnp.where(first_half, pltpu.roll(oo, 64, 1), oo)
            blocks.append(jnp.where(lo, oe, oo))
        return jnp.concatenate(blocks, axis=1)

    def sink_col(n):
        row = lax.broadcasted_iota(jnp.int32, (4 * n, 1), 0)
        return jnp.where(row < n, sinks[0], jnp.where(row < 2 * n, sinks[1],
                         jnp.where(row < 3 * n, sinks[2], sinks[3])))

    def block_body(i, carry):
        q0 = pl.multiple_of(i * Q_BLK, Q_BLK)
        start = pl.multiple_of(jnp.clip((i - 1) * Q_BLK, 0, SEQ - span), Q_BLK)
        qs = stack_heads(q_ref[pl.ds(q0, Q_BLK), :])
        kw = k_ref[pl.ds(start, span), :]
        vw = v_ref[pl.ds(start, span), :]
        s_loc = _dot_nt(qs, kw)
        qpos = q0 + (lax.broadcasted_iota(jnp.int32, (4 * Q_BLK, 1), 0) & (Q_BLK - 1))
        kpos = start + _lane_iota((1, span))
        s_loc = jnp.where(jnp.abs(kpos - qpos) <= SW_WINDOW, s_loc, NEG)
        s_ctx = _dot_nt(qs, kc)
        o = _softmax_pv([s_loc, s_ctx], [vw, vc], sink_col(Q_BLK))
        o_ref[pl.ds(q0, Q_BLK), :] = unstack_heads(o, Q_BLK).astype(BF16)
        return carry

    lax.fori_loop(0, SEQ // Q_BLK, block_body, 0, unroll=4)
    qs = stack_heads(q_ref[SEQ:SL, :])
    o = _softmax_pv([_dot_nt(qs, kc)], [vc], sink_col(CTX))
    o_ref[SEQ:SL, :] = unstack_heads(o, CTX).astype(BF16)


def _sw_attention(q, k, v, sink2, b):
    nq = 4 * 64
    slabq = lambda: pl.BlockSpec((None, SL, nq), lambda i, g: (i, 0, g))
    slabk = lambda: pl.BlockSpec((None, SL, LANES), lambda i, g: (i, 0, g // 2))
    return pl.pallas_call(
        _sw_kernel,
        grid=(b, SW_KV_HEADS),
        in_specs=[pl.BlockSpec(memory_space=pltpu.SMEM), slabq(), slabk(), slabk()],
        out_specs=slabq(),
        out_shape=jax.ShapeDtypeStruct(q.shape, BF16),
        compiler_params=_cparams("parallel", "parallel"),
        name="sw_attention",
    )(sink2, q, k, v)


DENSE_TQ = 512
DENSE_TK = 1024


def _online_step(s, v, m_ref, l_ref, acc_ref):
    chunks = s.shape[1] // LANES
    m_prev = m_ref[...]
    m_new = jnp.maximum(m_prev, jnp.max(s, axis=-1, keepdims=True))
    alpha = jnp.exp2(m_prev - m_new)
    p = jnp.exp2(s - jnp.concatenate([m_new] * chunks, axis=1))
    psum = functools.reduce(lambda a, b: a + b, [p[:, c * LANES:(c + 1) * LANES] for c in range(chunks)])
    l_ref[...] = alpha * l_ref[...] + psum
    acc_ref[...] = alpha * acc_ref[...] + _dot(p.astype(BF16), v)
    m_ref[...] = m_new


def _dense_streams(streams, v_ref, n_lat, vc):
    views = []
    for qs, _, (m_ref, l_ref, acc_ref) in streams:
        rows = qs.shape[0]
        m_ref[0:rows, :] = jnp.full((rows, LANES), NEG, F32)
        l_ref[0:rows, :] = jnp.zeros((rows, LANES), F32)
        acc_ref[0:rows, :] = jnp.zeros((rows, LANES), F32)
        views.append((m_ref.at[0:rows, :], l_ref.at[0:rows, :], acc_ref.at[0:rows, :]))

    for c in range(n_lat // DENSE_TK):
        v = v_ref[c * DENSE_TK:(c + 1) * DENSE_TK, :]
        for (qs, k_ref, _), view in zip(streams, views):
            _online_step(_dot_nt(qs, k_ref[c * DENSE_TK:(c + 1) * DENSE_TK, :]), v, *view)
    outs = []
    for (qs, k_ref, _), (mr, lr, ar) in zip(streams, views):
        _online_step(_dot_nt(qs, k_ref[SEQ:SL, :]), vc, mr, lr, ar)
        outs.append(ar[...] * (1.0 / jnp.sum(lr[...], axis=-1, keepdims=True)))
    return outs


def _diff_kernel(lambda_init, lam_ref, subln_ref, q_ref, k_ref, v_ref, o_ref, m_ref, l_ref, acc_ref):
    lo = _lane_iota((1, LANES)) < 64
    lam = lam_ref[...]
    lam_full = (jnp.exp(jnp.sum(lam[0:1] * lam[1:2], axis=-1, keepdims=True))
                - jnp.exp(jnp.sum(lam[2:3] * lam[3:4], axis=-1, keepdims=True)) + lambda_init)
    vc = v_ref[SEQ:SL, :]
    gain = subln_ref[...] * (1.0 - lambda_init)

    def q_body(i, carry):
        q0 = pl.multiple_of(i * DENSE_TQ, DENSE_TQ)
        qa, qb = _split_pair(q_ref[pl.ds(q0, DENSE_TQ), :], lo)
        qs = jnp.concatenate([qa, qb], axis=0)
        o, = _dense_streams([(qs, k_ref, (m_ref, l_ref, acc_ref))], v_ref, SEQ, vc)
        od = o[:DENSE_TQ] - lam_full * o[DENSE_TQ:]
        od = od * lax.rsqrt(jnp.mean(od * od, axis=-1, keepdims=True) + EPS) * gain
        o_ref[pl.ds(q0, DENSE_TQ), :] = od.astype(BF16)
        return carry

    lax.fori_loop(0, SEQ // DENSE_TQ, q_body, 0)
    o_ref[SEQ:SL, :] = jnp.zeros((CTX, LANES), BF16)


def _diff_attention(q, k, v, lam, subln, lambda_init, b):
    slab = lambda: pl.BlockSpec((None, SL, LANES), lambda i, j: (i, 0, j))
    rows = 2 * DENSE_TQ
    return pl.pallas_call(
        functools.partial(_diff_kernel, lambda_init),
        grid=(b, DIFF_HEADS),
        in_specs=[pl.BlockSpec(lam.shape, lambda i, j: (0, 0)),
                  pl.BlockSpec((1, LANES), lambda i, j: (0, 0)),
                  slab(), slab(), slab()],
        out_specs=slab(),
        out_shape=jax.ShapeDtypeStruct(v.shape, BF16),
        scratch_shapes=[pltpu.VMEM((rows, LANES), F32)] * 3,
        compiler_params=_cparams("parallel", "parallel"),
        name="diff_attention",
    )(lam, subln.reshape(1, LANES), q, k, v)


def _mla_kernel(q_ref, k_ref, v_ref, o_ref, *scratch):
    lo = _lane_iota((1, LANES)) < 64
    vc = v_ref[SEQ:SL, :]

    def two_heads(q, n_lat):
        streams = []
        for hd in range(2):
            sl = slice(hd * LANES, (hd + 1) * LANES)
            streams.append((q[:, sl], k_ref.at[:, sl], scratch[3 * hd:3 * hd + 3]))
        outs = _dense_streams(streams, v_ref, n_lat, vc)
        return jnp.where(lo, outs[0], outs[1])

    def q_body(i, carry):
        q0 = pl.multiple_of(i * DENSE_TQ, DENSE_TQ)
        o = two_heads(q_ref[pl.ds(q0, DENSE_TQ), :], SEQ)
        o_ref[pl.ds(q0, DENSE_TQ), :] = o.astype(BF16)
        return carry

    lax.fori_loop(0, SEQ // DENSE_TQ, q_body, 0)
    o_ref[SEQ:SL, :] = two_heads(q_ref[SEQ:SL, :], 0).astype(BF16)


def _mla_attention(q, k, v, b):
    slabq = lambda: pl.BlockSpec((None, SL, 2 * LANES), lambda i, j: (i, 0, j))
    slabv = lambda: pl.BlockSpec((None, SL, LANES), lambda i, j: (i, 0, j))
    return pl.pallas_call(
        _mla_kernel,
        grid=(b, MLA_HEADS // 2),
        in_specs=[slabq(), slabq(), slabv()],
        out_specs=slabv(),
        out_shape=jax.ShapeDtypeStruct(v.shape, BF16),
        scratch_shapes=[pltpu.VMEM((DENSE_TQ, LANES), F32)] * 6,
        compiler_params=_cparams("parallel", "parallel"),
        name="mla_attention",
    )(q, k, v)


def _oproj_kernel(a_ref, x_ref, wo_ref, g1_ref, ng_ref, sc_ref, sh_ref, wr_hi_ref, wr_lo_ref, br_ref,
                  xo_ref, h_ref, route_ref, cnt_ref):
    y = _dot(a_ref[...], wo_ref[...])
    x = x_ref[...] + g1_ref[...] * y
    xo_ref[...] = x
    h = _modulated(x, ng_ref[...], sc_ref[...], sh_ref[...])
    hi = h.astype(BF16)
    h_ref[...] = hi
    lo = (h - hi.astype(F32)).astype(BF16)
    z = _dot(hi, wr_hi_ref[...]) + _dot(lo, wr_hi_ref[...]) + _dot(hi, wr_lo_ref[...]) + br_ref[...]
    lane = _lane_iota(z.shape)
    lane_f = lane.astype(F32)
    big = jnp.float32(1e9)
    is_g = lane < N_GROUPS
    zg = jnp.where(is_g, z, NEG)
    gmax = jnp.max(zg, axis=-1, keepdims=True)
    gsum = jnp.sum(jnp.where(is_g, jnp.exp(zg - gmax), 0.0), axis=-1, keepdims=True)
    g_w = 1.0 / gsum
    g_idx = jnp.min(jnp.where(is_g & (zg == gmax), lane_f, big), axis=-1, keepdims=True)
    e_lo = N_GROUPS + EXPERTS_PER_GROUP * g_idx
    is_e = (lane_f >= e_lo) & (lane_f < e_lo + EXPERTS_PER_GROUP)
    ze = jnp.where(is_e, z, NEG)
    z1 = jnp.max(ze, axis=-1, keepdims=True)
    i1 = jnp.min(jnp.where(is_e & (ze == z1), lane_f, big), axis=-1, keepdims=True)
    ze2 = jnp.where(lane_f == i1, NEG, ze)
    z2 = jnp.max(ze2, axis=-1, keepdims=True)
    i2 = jnp.min(jnp.where(is_e & (ze2 == z2) & (lane_f != i1), lane_f, big), axis=-1, keepdims=True)
    r = jnp.exp(z2 - z1)
    gate1 = g_w / (1.0 + r)
    gate2 = g_w * r / (1.0 + r)
    e1 = i1 - N_GROUPS
    e2 = i2 - N_GROUPS
    route = jnp.where(lane == 0, gate1, jnp.where(lane == 1, gate2,
                      jnp.where(lane == 2, e1, jnp.where(lane == 3, e2, 0.0))))
    route_ref[...] = route

    @pl.when(pl.program_id(0) == 0)
    def _():
        cnt_ref[...] = jnp.zeros_like(cnt_ref)

    hits = jnp.where(lane_f == e1, 1.0, 0.0) + jnp.where(lane_f == e2, 1.0, 0.0)
    cnt_ref[...] += jnp.sum(hits, axis=0, keepdims=True)


def _route_pos_kernel(route_ref, base_ref, pos_ref, carry_ref):
    @pl.when(pl.program_id(0) == 0)
    def _():
        carry_ref[...] = base_ref[...]

    route = route_ref[...]
    lane_f = _lane_iota(route.shape).astype(F32)
    oh0 = jnp.where(lane_f == route[:, 2:3], 1.0, 0.0)
    oh1 = jnp.where(lane_f == route[:, 3:4], 1.0, 0.0)
    r = lax.broadcasted_iota(jnp.int32, (TM, TM), 0)
    c = lax.broadcasted_iota(jnp.int32, (TM, TM), 1)
    tri = jnp.where(c <= r, 1.0, 0.0).astype(BF16)
    carry = carry_ref[...]
    cnt0 = jnp.sum(oh0, axis=0, keepdims=True)
    pre0 = _dot(tri, oh0.astype(BF16)) - 1.0 + carry
    pre1 = _dot(tri, oh1.astype(BF16)) - 1.0 + carry + cnt0
    pos0 = jnp.sum(oh0 * pre0, axis=-1, keepdims=True)
    pos1 = jnp.sum(oh1 * pre1, axis=-1, keepdims=True)
    lane = _lane_iota(route.shape)
    pos_ref[...] = jnp.where(lane == 0, pos0, jnp.where(lane == 1, pos1, 0.0))
    carry_ref[...] = carry + cnt0 + jnp.sum(oh1, axis=0, keepdims=True)


def _route_pos(route, base):
    t = route.shape[0]
    return pl.pallas_call(
        _route_pos_kernel,
        grid=(t // TM,),
        in_specs=[pl.BlockSpec((TM, LANES), lambda i: (i, 0)), pl.BlockSpec((1, LANES), lambda i: (0, 0))],
        out_specs=pl.BlockSpec((TM, LANES), lambda i: (i, 0)),
        out_shape=jax.ShapeDtypeStruct((t, LANES), F32),
        scratch_shapes=[pltpu.VMEM((1, LANES), F32)],
        compiler_params=_cparams("arbitrary"),
        name="route_pos",
    )(route, base)


def _oproj_router(attn2, x2, mod3, wo_bf, ng, wr_hi, wr_lo, br):
    t, d = x2.shape
    n_in = attn2.shape[1]
    full = lambda a: pl.BlockSpec(a.shape, lambda i: (0,) * a.ndim)
    row = lambda w: pl.BlockSpec((TM, w), lambda i: (i, 0))
    return pl.pallas_call(
        _oproj_kernel,
        grid=(t // TM,),
        in_specs=[row(n_in), row(d), full(wo_bf), _mod_spec(2), pl.BlockSpec((1, d), lambda i: (0, 0)),
                  _mod_spec(4), _mod_spec(3), full(wr_hi), full(wr_lo), full(br)],
        out_specs=[row(d), row(d), row(LANES), pl.BlockSpec((1, LANES), lambda i: (0, 0))],
        out_shape=[jax.ShapeDtypeStruct((t, d), F32), jax.ShapeDtypeStruct((t, d), BF16),
                   jax.ShapeDtypeStruct((t, LANES), F32), jax.ShapeDtypeStruct((1, LANES), F32)],
        compiler_params=_cparams("arbitrary"),
        name="oproj_router",
    )(attn2, x2, wo_bf, mod3, ng.reshape(1, d), mod3, mod3, wr_hi, wr_lo, br)


def _expert_kernel(be_ref, nu_ref, x_ref, wgu_ref, wd_ref, o_ref, wgu_bf, wd_bf):
    i = pl.program_id(0)
    new_expert = (i == 0) | (be_ref[i] != be_ref[jnp.maximum(i - 1, 0)])

    @pl.when(new_expert)
    def _():
        wgu_bf[...] = wgu_ref[...].astype(BF16)
        wd_bf[...] = wd_ref[...].astype(BF16)

    @pl.when(i < nu_ref[0])
    def _():
        gu = _dot(x_ref[...], wgu_bf[...])
        gte, up = gu[:, :D_EXPERT], gu[:, D_EXPERT:]
        a = (gte * jax.nn.sigmoid(gte) * up).astype(BF16)
        o_ref[...] = _dot(a, wd_bf[...]).astype(BF16)

    @pl.when(i >= nu_ref[0])
    def _():
        o_ref[...] = jnp.zeros_like(o_ref)


def _experts(xs, block_expert, n_used, wgu, wd, layer):
    n_rows, d = xs.shape
    n_blocks = n_rows // MOE_BLK
    return pl.pallas_call(
        _expert_kernel,
        grid_spec=pltpu.PrefetchScalarGridSpec(
            num_scalar_prefetch=2,
            grid=(n_blocks,),
            in_specs=[pl.BlockSpec((MOE_BLK, d), lambda i, be, nu: (i, 0)),
                      pl.BlockSpec((None, None, d, 2 * D_EXPERT), lambda i, be, nu: (layer, be[i], 0, 0)),
                      pl.BlockSpec((None, None, D_EXPERT, d), lambda i, be, nu: (layer, be[i], 0, 0))],
            out_specs=pl.BlockSpec((MOE_BLK, d), lambda i, be, nu: (i, 0)),
            scratch_shapes=[pltpu.VMEM((d, 2 * D_EXPERT), BF16), pltpu.VMEM((D_EXPERT, d), BF16)]),
        out_shape=jax.ShapeDtypeStruct((n_rows, d), BF16),
        compiler_params=_cparams("arbitrary"),
        name="experts",
    )(block_expert, n_used, xs, wgu, wd)


def _combine_kernel(x_ref, y0_ref, y1_ref, route_ref, g2_ref, o_ref):
    route = route_ref[...]
    y = route[:, 0:1] * y0_ref[...].astype(F32) + route[:, 1:2] * y1_ref[...].astype(F32)
    o_ref[...] = x_ref[...] + g2_ref[...] * y


def _combine(x2, y0, y1, route, mod3):
    t, d = x2.shape
    row = lambda w: pl.BlockSpec((TM, w), lambda i: (i, 0))
    return pl.pallas_call(
        _combine_kernel,
        grid=(t // TM,),
        in_specs=[row(d), row(d), row(d), row(LANES), _mod_spec(5)],
        out_specs=row(d),
        out_shape=jax.ShapeDtypeStruct((t, d), F32),
        compiler_params=_cparams("parallel"),
        name="moe_combine",
    )(x2, y0, y1, route, mod3)


def _moe_dispatch(route, counts_f, t):
    counts = counts_f[0, :N_EXPERTS].astype(jnp.int32)
    padded = (counts + MOE_BLK - 1) // MOE_BLK * MOE_BLK
    seg_end = jnp.cumsum(padded)
    seg_start = seg_end - padded
    base = jnp.zeros((1, LANES), F32).at[0, :N_EXPERTS].set(seg_start.astype(F32))
    pos_f = _route_pos(route, base)
    pos = pos_f[:, :2].T.astype(jnp.int32)
    n_rows = -(-(2 * t + N_EXPERTS * (MOE_BLK - 1)) // MOE_BLK) * MOE_BLK
    n_blocks = n_rows // MOE_BLK
    token = jnp.tile(jnp.arange(t, dtype=jnp.int32), 2)
    row_token = jnp.zeros((n_rows,), jnp.int32).at[pos.reshape(-1)].set(
        token, mode="promise_in_bounds", unique_indices=True)
    block_start = jnp.arange(n_blocks, dtype=jnp.int32) * MOE_BLK
    block_expert = jnp.minimum(jnp.sum((seg_end[None, :] <= block_start[:, None]).astype(jnp.int32), axis=1),
                               N_EXPERTS - 1)
    n_used = (seg_end[-1] // MOE_BLK).astype(jnp.int32).reshape(1)
    return row_token, pos, block_expert, n_used


def _take_rows(a, idx):
    return a.at[idx].get(mode="promise_in_bounds")


def _moe(x2, h_bf, route, counts_f, mod3, wgu, wd, layer):
    t = x2.shape[0]
    row_token, pos, block_expert, n_used = _moe_dispatch(route, counts_f, t)
    xs = _take_rows(h_bf, row_token)
    ys = _experts(xs, block_expert, n_used, wgu, wd, layer)
    y0 = _take_rows(ys, pos[0])
    y1 = _take_rows(ys, pos[1])
    return _combine(x2, y0, y1, route, mod3)


def kernel(x, c, ctx, c_ctx, ada_w, ada_b, norm_g, moe_w_router_group, moe_b_router_group, moe_w_router_expert, moe_b_router_expert, moe_w_gate_up, moe_w_down, na_w_qkv, na_w_o, na_q_norm, na_k_norm, na_rpb, sw_w_qkv, sw_w_o, sw_q_norm, sw_k_norm, sw_sink, mla_w_dqkv, mla_q_a_norm, mla_kv_a_norm, mla_w_uq, mla_w_ukv, mla_q_norm, mla_k_norm, mla_w_o, diff_w_qkv, diff_q_norm, diff_k_norm, diff_lambda, diff_subln, diff_w_o):
    b = x.shape[0]
    parts = (slice(0, b // 2), slice(b // 2, b)) if b >= 2 else (slice(0, b),)
    outs = []
    for sl in parts:
        slab = _stack(x[sl], c[sl], ctx[sl], c_ctx, ada_w, ada_b, norm_g, moe_w_router_group, moe_b_router_group, moe_w_router_expert, moe_b_router_expert, moe_w_gate_up, moe_w_down, na_w_qkv, na_w_o, na_q_norm, na_k_norm, na_rpb, sw_w_qkv, sw_w_o, sw_q_norm, sw_k_norm, sw_sink, mla_w_dqkv, mla_q_a_norm, mla_kv_a_norm, mla_w_uq, mla_w_ukv, mla_q_norm, mla_k_norm, mla_w_o, diff_w_qkv, diff_q_norm, diff_k_norm, diff_lambda, diff_subln, diff_w_o)
        outs.append(slab[:, :SEQ])
    return jnp.concatenate(outs, axis=0)


def _stack(x, c, ctx, c_ctx, ada_w, ada_b, norm_g, moe_w_router_group, moe_b_router_group, moe_w_router_expert, moe_b_router_expert, moe_w_gate_up, moe_w_down, na_w_qkv, na_w_o, na_q_norm, na_k_norm, na_rpb, sw_w_qkv, sw_w_o, sw_q_norm, sw_k_norm, sw_sink, mla_w_dqkv, mla_q_a_norm, mla_kv_a_norm, mla_w_uq, mla_w_ukv, mla_q_norm, mla_k_norm, mla_w_o, diff_w_qkv, diff_q_norm, diff_k_norm, diff_lambda, diff_subln, diff_w_o):
    b, s, d = x.shape
    assert (s, d) == (SEQ, D_MODEL) and ctx.shape == (b, CTX, d) and b <= 8
    depth = ada_w.shape[0]
    assert depth <= 4
    t = b * SL

    cc = jnp.zeros((16, d), F32).at[:b].set(c).at[8].set(c_ctx)
    mod = _ada_table(cc, ada_w, ada_b)
    x2 = jnp.concatenate([x, ctx], axis=1).reshape(t, d)

    lane = np.arange(LANES)
    bd64 = _block_diag(lane // 64)
    lm64 = lane % 64
    rope64 = _rope_tables(64, lm64)
    scale64 = 64 ** -0.5 * LOG2E

    r3 = lambda a: a.reshape(b, SL, a.shape[-1])
    for l in range(depth):
        mod3 = mod[l].reshape(16 * 6, 1, d)
        if l == 0:
            w = na_w_qkv[0].astype(BF16)
            q, k, v = _project(x2, mod3, norm_g[l, 0], w, _qkv_plan(1024, 1024, 1024), (1024, 1024, 1024),
                               bd64, _head64_vec(na_q_norm[0], na_k_norm[0], scale64), name="na_proj")
            attn = _na_attention(r3(q), r3(k), r3(v), _na_bias_table(na_rpb[0]), b)
            w_o = na_w_o[0]
        elif l == 1:
            w = sw_w_qkv[0].astype(BF16)
            q, k, v = _project(x2, mod3, norm_g[l, 0], w, _qkv_plan(1024, 256, 256), (1024, 256, 256),
                               bd64, _head64_vec(sw_q_norm[0], sw_k_norm[0], scale64),
                               rope=rope64, rope_half=32, name="sw_proj")
            attn = _sw_attention(r3(q), r3(k), r3(v), sw_sink[0] * LOG2E, b)
            w_o = sw_w_o[0]
        elif l == 2:
            p = dict(w_dqkv=mla_w_dqkv[0], q_a_norm=mla_q_a_norm[0], kv_a_norm=mla_kv_a_norm[0],
                     w_uq=mla_w_uq[0], w_ukv=mla_w_ukv[0], q_norm=mla_q_norm[0], k_norm=mla_k_norm[0])
            q, k, v = _mla_project(x2, mod3, norm_g[l, 0], p)
            attn = _mla_attention(r3(q), r3(k), r3(v), b)
            w_o = mla_w_o[0]
        else:
            w = diff_w_qkv[0].astype(BF16)
            q, k, v = _project(x2, mod3, norm_g[l, 0], w, _qkv_plan(1024, 1024, 1024), (1024, 1024, 1024),
                               bd64, _head64_vec(diff_q_norm[0], diff_k_norm[0], scale64),
                               rope=rope64, rope_half=32, name="diff_proj")
            lambda_init = 0.8 - 0.6 * math.exp(-0.3 * l)
            attn = _diff_attention(r3(q), r3(k), r3(v), diff_lambda[0], diff_subln[0], lambda_init, b)
            w_o = diff_w_o[0]

        wr = jnp.concatenate([moe_w_router_group[l], moe_w_router_expert[l]], axis=1)
        wr = jnp.pad(wr, ((0, 0), (0, LANES - wr.shape[1])))
        wr_hi = wr.astype(BF16)
        wr_lo = (wr - wr_hi.astype(F32)).astype(BF16)
        br = jnp.concatenate([moe_b_router_group[l], moe_b_router_expert[l]])
        br = jnp.pad(br, (0, LANES - br.shape[0])).reshape(1, LANES)
        x2, h_bf, route, counts_f = _oproj_router(attn.reshape(t, -1), x2, mod3, w_o.astype(BF16), norm_g[l, 1],
                                                  wr_hi, wr_lo, br)
        x2 = _moe(x2, h_bf, route, counts_f, mod3, moe_w_gate_up, moe_w_down, l)

    return x2.reshape(b, SL, d)
```

```python
import functools
import math

import jax
import jax.numpy as jnp
import numpy as np
from jax import lax
from jax.experimental import pallas as pl
from jax.experimental.pallas import tpu as pltpu

F32 = jnp.float32
BF16 = jnp.bfloat16

D_MODEL = 1024
SEQ = 4096
CTX = 256
SL = SEQ + CTX
GRID_W = 64
GRID_ROWS = SEQ // GRID_W
EPS = 1e-6
ROPE_BASE = 10000.0
NEG = -1e30
LOG2E = 1.4426950408889634

NA_KH, NA_KW = 8, 16
SW_WINDOW = 128
SW_HEADS, SW_KV_HEADS = 16, 4
MLA_Q_RANK, MLA_KV_RANK, MLA_NOPE, MLA_ROPE, MLA_V, MLA_HEADS = 384, 256, 64, 32, 64, 16
DIFF_HEADS = 8
N_GROUPS, EXPERTS_PER_GROUP, N_EXPERTS, D_EXPERT = 4, 8, 32, 256

LANES = 128
MXU_N = 256
TM = 256
TILES_PER_SLAB = SL // TM
MOE_BLK = 256
VMEM_LIMIT = 56 * 1024 * 1024


def _cparams(*sem):
    return pltpu.CompilerParams(dimension_semantics=sem, vmem_limit_bytes=VMEM_LIMIT)


def _lane_iota(shape):
    return lax.broadcasted_iota(jnp.int32, shape, len(shape) - 1)


def _dot(a, b):
    return jnp.dot(a, b, preferred_element_type=F32)


def _dot_nt(a, b):
    return lax.dot_general(a, b, (((1,), (1,)), ((), ())), preferred_element_type=F32)


def _ada_kernel(cc_ref, w_ref, b_ref, o_ref):
    a = cc_ref[...]
    a = a * jax.nn.sigmoid(a)
    hi = a.astype(BF16)
    lo = (a - hi.astype(F32)).astype(BF16)
    w = w_ref[...]
    whi = w.astype(BF16)
    wlo = (w - whi.astype(F32)).astype(BF16)
    o_ref[...] = _dot(hi, whi) + _dot(lo, whi) + _dot(hi, wlo) + b_ref[...]


def _ada_table(cc, ada_w, ada_b):
    depth, d, n = ada_w.shape
    tn = 1024
    return pl.pallas_call(
        _ada_kernel,
        grid=(depth, n // tn),
        in_specs=[pl.BlockSpec((16, d), lambda l, j: (0, 0)),
                  pl.BlockSpec((None, d, tn), lambda l, j: (l, 0, j)),
                  pl.BlockSpec((None, 1, tn), lambda l, j: (l, 0, j))],
        out_specs=pl.BlockSpec((None, 16, tn), lambda l, j: (l, 0, j)),
        out_shape=jax.ShapeDtypeStruct((depth, 16, n), F32),
        compiler_params=_cparams("parallel", "parallel"),
        name="ada_table",
    )(cc, ada_w, ada_b.reshape(depth, 1, n))


def _mod_spec(part):
    def index(i):
        m = jnp.where(i % TILES_PER_SLAB == TILES_PER_SLAB - 1, 8, i // TILES_PER_SLAB)
        return (m * 6 + part, 0, 0)
    return pl.BlockSpec((None, 1, D_MODEL), index)


def _modulated(x, g, sc, sh):
    ms = jnp.mean(x * x, axis=-1, keepdims=True)
    return (x * lax.rsqrt(ms + EPS)) * g * (1.0 + sc) + sh


def _group_norm_rope(y, bd, invcnt, gain, cos=None, sin=None, half=None):
    ss = _dot((y * y).astype(BF16), bd)
    yn = y * lax.rsqrt(ss * invcnt + EPS) * gain
    if cos is not None:
        lane = _lane_iota(yn.shape)
        fwd = pltpu.roll(yn, LANES - half, 1)
        bwd = pltpu.roll(yn, half, 1)
        partner = jnp.where((lane & (2 * half - 1)) < half, fwd, bwd)
        yn = yn * cos + partner * sin
    return yn


def _block_diag(group_of_lane):
    g = np.asarray(group_of_lane)
    return jnp.asarray((g[:, None] == g[None, :]).astype(np.float32), dtype=BF16)


def _rope_tables(rot_dim, lane_map):
    t = jnp.arange(SEQ, dtype=jnp.int32)
    n_freq = rot_dim // 4
    inv_freq = ROPE_BASE ** (-jnp.arange(n_freq, dtype=F32) / n_freq)
    rows = (t // GRID_W).astype(F32)
    cols = (t % GRID_W).astype(F32)
    ang = jnp.concatenate([rows[:, None] * inv_freq, cols[:, None] * inv_freq], axis=-1)
    cos, sin = jnp.cos(ang), jnp.sin(ang)
    lm = np.asarray(lane_map)
    rot = lm >= 0
    idx = np.where(rot, lm % (rot_dim // 2), 0)
    sign = np.where(lm < rot_dim // 2, -1.0, 1.0).astype(np.float32)
    cos_t = jnp.where(rot[None, :], cos[:, idx], 1.0)
    sin_t = jnp.where(rot[None, :], sin[:, idx] * sign[None, :], 0.0)
    ident_c = jnp.ones((CTX, LANES), F32)
    ident_s = jnp.zeros((CTX, LANES), F32)
    return jnp.concatenate([cos_t, ident_c], 0), jnp.concatenate([sin_t, ident_s], 0)


def _residual_with_moe(x_ref, y0_ref, y1_ref, route_ref, g2_ref):
    route = route_ref[...]
    y = route[:, 0:1] * y0_ref[...].astype(F32) + route[:, 1:2] * y1_ref[...].astype(F32)
    return x_ref[...] + g2_ref[...] * y


def _proj_kernel(plan, rope_half, has_pending, x_ref, g_ref, sc_ref, sh_ref, w_ref, bd_ref, vec_ref, *rest):
    if has_pending:
        x = _residual_with_moe(x_ref, *rest[:4])
        rest = rest[4:]
        rest[-1][...] = x
        rest = rest[:-1]
    else:
        x = x_ref[...]
    if rope_half is not None:
        cos_ref, sin_ref = rest[:2]
        outs = rest[2:]
        cos, sin = cos_ref[...], sin_ref[...]
    else:
        outs = rest
        cos = sin = None
    h = _modulated(x, g_ref[...], sc_ref[...], sh_ref[...]).astype(BF16)
    bd = bd_ref[...]
    for c0, ep, out_idx, oc0 in plan:
        acc = _dot(h, w_ref[:, c0:c0 + MXU_N])
        for s in range(MXU_N // LANES):
            y = acc[:, s * LANES:(s + 1) * LANES]
            if ep is not None:
                y = _group_norm_rope(y, bd, vec_ref[ep, 0:1, :], vec_ref[ep, 1:2, :], cos, sin, rope_half)
            lo = oc0 + s * LANES
            outs[out_idx][:, lo:lo + LANES] = y.astype(BF16)


def _pending_specs(pending, d):
    row = lambda w: pl.BlockSpec((TM, w), lambda i: (i, 0))
    return [row(d), row(d), row(LANES), _mod_spec(5)], list(pending)


def _project(x2, mod3, g, w_bf, plan, out_widths, bd, vec, rope=None, rope_half=None, pending=None, name="proj"):
    t, d = x2.shape
    n = w_bf.shape[1]
    n_tiles = t // TM
    in_specs = [pl.BlockSpec((TM, d), lambda i: (i, 0)),
                pl.BlockSpec((1, d), lambda i: (0, 0)),
                _mod_spec(1), _mod_spec(0),
                pl.BlockSpec((d, n), lambda i: (0, 0)),
                pl.BlockSpec(bd.shape, lambda i: (0, 0)),
                pl.BlockSpec(vec.shape, lambda i: (0, 0, 0))]
    args = [x2, g.reshape(1, d), mod3, mod3, w_bf, bd, vec]
    out_specs = [pl.BlockSpec((TM, w), lambda i: (i, 0)) for w in out_widths]
    out_shape = [jax.ShapeDtypeStruct((t, w), BF16) for w in out_widths]
    if pending is not None:
        specs, pargs = _pending_specs(pending, d)
        in_specs += specs
        args += pargs
        out_specs.append(pl.BlockSpec((TM, d), lambda i: (i, 0)))
        out_shape.append(jax.ShapeDtypeStruct((t, d), F32))
    if rope is not None:
        in_specs += [pl.BlockSpec((TM, LANES), lambda i: (i % TILES_PER_SLAB, 0))] * 2
        args += list(rope)
    return pl.pallas_call(
        functools.partial(_proj_kernel, plan, rope_half, pending is not None),
        grid=(n_tiles,),
        in_specs=in_specs,
        out_specs=out_specs,
        out_shape=out_shape,
        compiler_params=_cparams("parallel"),
        name=name,
    )(*args)


def _qkv_plan(nq, nk, nv):
    plan = []
    for c0 in range(0, nq, MXU_N):
        plan.append((c0, 0, 0, c0))
    for c0 in range(0, nk, MXU_N):
        plan.append((nq + c0, 1, 1, c0))
    for c0 in range(0, nv, MXU_N):
        plan.append((nq + nk + c0, None, 2, c0))
    return tuple(plan)


def _head64_vec(q_norm, k_norm, q_scale):
    inv = jnp.full((LANES,), 1.0 / 64, F32)
    z = jnp.zeros((6, LANES), F32)
    vq = jnp.concatenate([inv[None], (jnp.tile(q_norm, 2) * q_scale)[None], z], 0)
    vk = jnp.concatenate([inv[None], jnp.tile(k_norm, 2)[None], z], 0)
    return jnp.stack([vq, vk], 0)


def _mla_proj_kernel(x_ref, g_ref, sc_ref, sh_ref, wd_ref, wuq_ref, wuk_ref, wuv_ref, bd_ref, vec_ref,
                     an_ref, cos_ref, sin_ref, y0_ref, y1_ref, route_ref, g2_ref, q_out, k_out, v_out, x_out):
    x = _residual_with_moe(x_ref, y0_ref, y1_ref, route_ref, g2_ref)
    x_out[...] = x
    h = _modulated(x, g_ref[...], sc_ref[...], sh_ref[...]).astype(BF16)
    cos, sin = cos_ref[...], sin_ref[...]
    half = MLA_ROPE // 2
    cq = _dot(h, wd_ref[:, 0:MLA_Q_RANK])
    ckv = _dot(h, wd_ref[:, MLA_Q_RANK:MLA_Q_RANK + MLA_KV_RANK])
    kr = _dot(h, wd_ref[:, MLA_Q_RANK + MLA_KV_RANK:])
    cq = cq * lax.rsqrt(jnp.mean(cq * cq, axis=-1, keepdims=True) + EPS) * an_ref[0:1, 0:MLA_Q_RANK]
    ckv = ckv * lax.rsqrt(jnp.mean(ckv * ckv, axis=-1, keepdims=True) + EPS) * an_ref[1:2, 0:MLA_KV_RANK]
    cq = cq.astype(BF16)
    ckv = ckv.astype(BF16)
    kr = _group_norm_rope(kr, bd_ref[1], vec_ref[2, 0:1, :], vec_ref[2, 1:2, :], cos[:, LANES:], sin[:, LANES:], half)
    kr = pltpu.roll(kr, MLA_NOPE, 1)
    for hd in range(MLA_HEADS):
        c0 = hd * LANES
        qh = _dot(cq, wuq_ref[:, c0:c0 + LANES])
        qh = _group_norm_rope(qh, bd_ref[0], vec_ref[0, 0:1, :], vec_ref[0, 1:2, :], cos[:, :LANES], sin[:, :LANES], half)
        q_out[:, c0:c0 + LANES] = qh.astype(BF16)
        kh = _dot(ckv, wuk_ref[:, c0:c0 + LANES])
        kh = _group_norm_rope(kh, bd_ref[0], vec_ref[1, 0:1, :], vec_ref[1, 1:2, :]) + kr
        k_out[:, c0:c0 + LANES] = kh.astype(BF16)
    for c0 in range(0, MLA_HEADS * MLA_V, MXU_N):
        v_out[:, c0:c0 + MXU_N] = _dot(ckv, wuv_ref[:, c0:c0 + MXU_N]).astype(BF16)


def _mla_project(x2, mod3, g, p, pending):
    t, d = x2.shape
    hq = MLA_NOPE + MLA_ROPE
    n_tiles = t // TM
    wd = jnp.pad(p["w_dqkv"], ((0, 0), (0, LANES - MLA_ROPE))).astype(BF16)
    wuq = p["w_uq"].reshape(MLA_Q_RANK, MLA_HEADS, hq)
    wuq = jnp.pad(wuq, ((0, 0), (0, 0), (0, LANES - hq))).reshape(MLA_Q_RANK, MLA_HEADS * LANES).astype(BF16)
    wukv = p["w_ukv"].reshape(MLA_KV_RANK, MLA_HEADS, MLA_NOPE + MLA_V)
    wuk = jnp.pad(wukv[:, :, :MLA_NOPE], ((0, 0), (0, 0), (0, LANES - MLA_NOPE)))
    wuk = wuk.reshape(MLA_KV_RANK, MLA_HEADS * LANES).astype(BF16)
    wuv = wukv[:, :, MLA_NOPE:].reshape(MLA_KV_RANK, MLA_HEADS * MLA_V).astype(BF16)
    scale = (MLA_NOPE + MLA_ROPE) ** -0.5 * LOG2E
    lane = np.arange(LANES)
    grp_head = np.where(lane < 64, 0, np.where(lane < 96, 1, 2))
    grp_kr = np.where(lane < 32, 0, 1)
    bd = jnp.stack([_block_diag(grp_head), _block_diag(grp_kr)], 0)
    inv_head = jnp.asarray(np.where(lane < 64, 1 / 64, 1 / 32), F32)
    zpad = jnp.zeros((LANES - hq,), F32)
    gq = jnp.concatenate([p["q_norm"], zpad]) * scale
    gk = jnp.concatenate([p["k_norm"][:MLA_NOPE], jnp.zeros((LANES - MLA_NOPE,), F32)])
    gkr = jnp.concatenate([p["k_norm"][MLA_NOPE:], jnp.zeros((LANES - MLA_ROPE,), F32)])
    inv_kr = jnp.asarray(np.where(lane < 32, 1 / 32, 1 / 96), F32)
    z6 = jnp.zeros((6, LANES), F32)
    vec = jnp.stack([jnp.concatenate([inv_head[None], gq[None], z6], 0),
                     jnp.concatenate([inv_head[None], gk[None], z6], 0),
                     jnp.concatenate([inv_kr[None], gkr[None], z6], 0)], 0)
    an = jnp.zeros((8, MLA_Q_RANK), F32)
    an = an.at[0].set(p["q_a_norm"]).at[1, :MLA_KV_RANK].set(p["kv_a_norm"])
    lm_head = np.where((lane >= 64) & (lane < 96), lane - 64, -1)
    lm_kr = np.where(lane < 32, lane, -1)
    ch, sh_ = _rope_tables(MLA_ROPE, lm_head)
    ck, sk = _rope_tables(MLA_ROPE, lm_kr)
    cos2 = jnp.concatenate([ch, ck], 1)
    sin2 = jnp.concatenate([sh_, sk], 1)

    full = lambda a: pl.BlockSpec(a.shape, lambda i: (0,) * a.ndim)
    rope_spec = pl.BlockSpec((TM, 2 * LANES), lambda i: (i % TILES_PER_SLAB, 0))
    out_w = (MLA_HEADS * LANES, MLA_HEADS * LANES, MLA_HEADS * MLA_V)
    pend_specs, pend_args = _pending_specs(pending, d)
    return pl.pallas_call(
        _mla_proj_kernel,
        grid=(n_tiles,),
        in_specs=[pl.BlockSpec((TM, d), lambda i: (i, 0)), pl.BlockSpec((1, d), lambda i: (0, 0)),
                  _mod_spec(1), _mod_spec(0), full(wd), full(wuq), full(wuk), full(wuv), full(bd), full(vec),
                  full(an), rope_spec, rope_spec] + pend_specs,
        out_specs=[pl.BlockSpec((TM, w), lambda i: (i, 0)) for w in out_w + (d,)],
        out_shape=[jax.ShapeDtypeStruct((t, w), BF16) for w in out_w] + [jax.ShapeDtypeStruct((t, d), F32)],
        compiler_params=_cparams("parallel"),
        name="mla_proj",
    )(x2, g.reshape(1, d), mod3, mod3, wd, wuq, wuk, wuv, bd, vec, an, cos2, sin2, *pend_args)


def _softmax_pv(s_list, v_list, sink=None):
    m = functools.reduce(jnp.maximum, [jnp.max(s, axis=-1, keepdims=True) for s in s_list])
    if sink is not None:
        m = jnp.maximum(m, sink)
    ps = [jnp.exp2(s - m) for s in s_list]
    l = functools.reduce(lambda a, b: a + b, [jnp.sum(p, axis=-1, keepdims=True) for p in ps])
    if sink is not None:
        l = l + jnp.exp2(sink - m)
    o = functools.reduce(lambda a, b: a + b, [_dot(p.astype(BF16), v) for p, v in zip(ps, v_list)])
    return o * (1.0 / l)


def _split_pair(x, lo):
    zero = jnp.zeros_like(x)
    return jnp.where(lo, x, zero), jnp.where(lo, zero, x)


def _na_kernel(q_ref, k_ref, v_ref, bias_ref, o_ref):
    lo = _lane_iota((1, LANES)) < 64
    kc = k_ref[SEQ:SL, :]
    vc = v_ref[SEQ:SL, :]
    nb = NA_KH * GRID_W

    def pair_attend(q, ks, vs, bias):
        n = q.shape[0]
        qa, qb = _split_pair(q, lo)
        qs = jnp.concatenate([qa, qb], axis=0)
        s_list = [_dot_nt(qs, k) for k in ks]
        if bias is not None:
            s_list[0] = s_list[0] + bias
        o = _softmax_pv(s_list, vs)
        return jnp.where(lo, o[:n], o[n:])

    def row_body(r, carry):
        row0 = jnp.clip(r - NA_KH // 2, 0, GRID_ROWS - NA_KH)
        q = q_ref[pl.ds(pl.multiple_of(r * GRID_W, GRID_W), GRID_W), :]
        ks = pl.multiple_of(row0 * GRID_W, GRID_W)
        kn = k_ref[pl.ds(ks, nb), :]
        vn = v_ref[pl.ds(ks, nb), :]
        o = pair_attend(q, [kn, kc], [vn, vc], bias_ref[r - row0])
        o_ref[pl.ds(pl.multiple_of(r * GRID_W, GRID_W), GRID_W), :] = o.astype(BF16)
        return carry

    lax.fori_loop(0, GRID_ROWS, row_body, 0, unroll=8)
    o_ref[SEQ:SL, :] = pair_attend(q_ref[SEQ:SL, :], [kc], [vc], None).astype(BF16)


def _na_bias_table(rpb):
    heads = rpb.shape[0]
    n_col = 2 * NA_KW - 1
    r2 = rpb.reshape(heads, 2 * NA_KH - 1, n_col) * LOG2E
    padded = jnp.pad(r2, ((0, 0), (0, 0), (GRID_W, GRID_W)))
    base = GRID_W + NA_KW - 1
    toeplitz = jnp.stack([padded[:, :, base - qc:base - qc + GRID_W] for qc in range(GRID_W)], axis=2)
    qc = np.arange(GRID_W)[:, None, None]
    kc = np.arange(GRID_W)[None, None, :]
    col0 = np.clip(qc - NA_KW // 2, 0, GRID_W - NA_KW)
    valid = np.broadcast_to((kc >= col0) & (kc < col0 + NA_KW), (GRID_W, NA_KH, GRID_W))
    valid = valid.reshape(GRID_W, NA_KH * GRID_W)
    classes = []
    for c in range(NA_KH):
        t = toeplitz[:, NA_KH - 1 - c:2 * NA_KH - 1 - c]
        t = t.transpose(0, 2, 1, 3).reshape(heads, GRID_W, NA_KH * GRID_W)
        classes.append(jnp.where(valid[None], t, NEG))
    tab = jnp.stack(classes, 0)
    return tab.reshape(NA_KH, heads // 2, 2 * GRID_W, NA_KH * GRID_W)


def _na_attention(q, k, v, bias, b):
    hp = q.shape[-1] // LANES
    slab = lambda: pl.BlockSpec((None, SL, LANES), lambda i, j: (i, 0, j))
    return pl.pallas_call(
        _na_kernel,
        grid=(b, hp),
        in_specs=[slab(), slab(), slab(),
                  pl.BlockSpec((NA_KH, None, 2 * GRID_W, NA_KH * GRID_W), lambda i, j: (0, j, 0, 0))],
        out_specs=slab(),
        out_shape=jax.ShapeDtypeStruct(q.shape, BF16),
        compiler_params=_cparams("parallel", "parallel"),
        name="na_attention",
    )(q, k, v, bias)


Q_BLK = 128


def _sw_kernel(sink_ref, q_ref, k_ref, v_ref, o_ref):
    g = pl.program_id(1)
    lane = _lane_iota((1, LANES))
    lo = lane < 64
    first_half = (jnp.zeros((1, LANES), jnp.int32) + (g % 2)) == 0
    target = (lane // 64) == (g % 2)
    kc = k_ref[SEQ:SL, :]
    vc = v_ref[SEQ:SL, :]
    span = Q_BLK + 2 * SW_WINDOW
    sinks = [sink_ref[4 * g + t] for t in range(4)]

    def swap(x):
        return jnp.concatenate([x[:, 64:], x[:, :64]], axis=1)

    def stack_heads(q):
        parts = []
        for blk in range(2):
            qb = q[:, blk * LANES:(blk + 1) * LANES]
            qs = swap(qb)
            even = jnp.where(first_half, qb, qs)
            odd = jnp.where(first_half, qs, qb)
            zero = jnp.zeros_like(qb)
            parts += [jnp.where(target, even, zero), jnp.where(target, odd, zero)]
        return jnp.concatenate(parts, axis=0)

    def unstack_heads(o, n):
        blocks = []
        for blk in range(2):
            oe = o[(2 * blk) * n:(2 * blk + 1) * n]
            oo = o[(2 * blk + 1) * n:(2 * blk + 2) * n]
            oe = jnp.where(first_half, oe, pltpu.roll(oe, 64, 1))
            oo = jnp.where(first_half, pltpu.roll(oo, 64, 1), oo)
            blocks.append(jnp.where(lo, oe, oo))
        return jnp.concatenate(blocks, axis=1)

    def sink_col(n):
        row = lax.broadcasted_iota(jnp.int32, (4 * n, 1), 0)
        return jnp.where(row < n, sinks[0], jnp.where(row < 2 * n, sinks[1],
                         jnp.where(row < 3 * n, sinks[2], sinks[3])))

    def block_body(i, carry):
        q0 = pl.multiple_of(i * Q_BLK, Q_BLK)
        start = pl.multiple_of(jnp.clip((i - 1) * Q_BLK, 0, SEQ - span), Q_BLK)
        qs = stack_heads(q_ref[pl.ds(q0, Q_BLK), :])
        kw = k_ref[pl.ds(start, span), :]
        vw = v_ref[pl.ds(start, span), :]
        s_loc = _dot_nt(qs, kw)
        qpos = q0 + (lax.broadcasted_iota(jnp.int32, (4 * Q_BLK, 1), 0) & (Q_BLK - 1))
        kpos = start + _lane_iota((1, span))
        s_loc = jnp.where(jnp.abs(kpos - qpos) <= SW_WINDOW, s_loc, NEG)
        s_ctx = _dot_nt(qs, kc)
        o = _softmax_pv([s_loc, s_ctx], [vw, vc], sink_col(Q_BLK))
        o_ref[pl.ds(q0, Q_BLK), :] = unstack_heads(o, Q_BLK).astype(BF16)
        return carry

    lax.fori_loop(0, SEQ // Q_BLK, block_body, 0, unroll=4)
    qs = stack_heads(q_ref[SEQ:SL, :])
    o = _softmax_pv([_dot_nt(qs, kc)], [vc], sink_col(CTX))
    o_ref[SEQ:SL, :] = unstack_heads(o, CTX).astype(BF16)


def _sw_attention(q, k, v, sink2, b):
    nq = 4 * 64
    slabq = lambda: pl.BlockSpec((None, SL, nq), lambda i, g: (i, 0, g))
    slabk = lambda: pl.BlockSpec((None, SL, LANES), lambda i, g: (i, 0, g // 2))
    return pl.pallas_call(
        _sw_kernel,
        grid=(b, SW_KV_HEADS),
        in_specs=[pl.BlockSpec(memory_space=pltpu.SMEM), slabq(), slabk(), slabk()],
        out_specs=slabq(),
        out_shape=jax.ShapeDtypeStruct(q.shape, BF16),
        compiler_params=_cparams("parallel", "parallel"),
        name="sw_attention",
    )(sink2, q, k, v)


DENSE_TQ = 512
DENSE_TK = 1024


def _online_step(s, v, m_ref, l_ref, acc_ref):
    chunks = s.shape[1] // LANES
    m_prev = m_ref[...]
    m_new = jnp.maximum(m_prev, jnp.max(s, axis=-1, keepdims=True))
    alpha = jnp.exp2(m_prev - m_new)
    p = jnp.exp2(s - jnp.concatenate([m_new] * chunks, axis=1))
    psum = functools.reduce(lambda a, b: a + b, [p[:, c * LANES:(c + 1) * LANES] for c in range(chunks)])
    l_ref[...] = alpha * l_ref[...] + psum
    acc_ref[...] = alpha * acc_ref[...] + _dot(p.astype(BF16), v)
    m_ref[...] = m_new


def _dense_streams(streams, v_ref, n_lat, vc):
    views = []
    for qs, _, (m_ref, l_ref, acc_ref) in streams:
        rows = qs.shape[0]
        m_ref[0:rows, :] = jnp.full((rows, LANES), NEG, F32)
        l_ref[0:rows, :] = jnp.zeros((rows, LANES), F32)
        acc_ref[0:rows, :] = jnp.zeros((rows, LANES), F32)
        views.append((m_ref.at[0:rows, :], l_ref.at[0:rows, :], acc_ref.at[0:rows, :]))

    for c in range(n_lat // DENSE_TK):
        v = v_ref[c * DENSE_TK:(c + 1) * DENSE_TK, :]
        for (qs, k_ref, _), view in zip(streams, views):
            _online_step(_dot_nt(qs, k_ref[c * DENSE_TK:(c + 1) * DENSE_TK, :]), v, *view)
    outs = []
    for (qs, k_ref, _), (mr, lr, ar) in zip(streams, views):
        _online_step(_dot_nt(qs, k_ref[SEQ:SL, :]), vc, mr, lr, ar)
        outs.append(ar[...] * (1.0 / jnp.sum(lr[...], axis=-1, keepdims=True)))
    return outs


def _diff_kernel(lambda_init, lam_ref, subln_ref, q_ref, k_ref, v_ref, o_ref, m_ref, l_ref, acc_ref):
    lo = _lane_iota((1, LANES)) < 64
    lam = lam_ref[...]
    lam_full = (jnp.exp(jnp.sum(lam[0:1] * lam[1:2], axis=-1, keepdims=True))
                - jnp.exp(jnp.sum(lam[2:3] * lam[3:4], axis=-1, keepdims=True)) + lambda_init)
    vc = v_ref[SEQ:SL, :]
    gain = subln_ref[...] * (1.0 - lambda_init)

    def q_body(i, carry):
        q0 = pl.multiple_of(i * DENSE_TQ, DENSE_TQ)
        qa, qb = _split_pair(q_ref[pl.ds(q0, DENSE_TQ), :], lo)
        qs = jnp.concatenate([qa, qb], axis=0)
        o, = _dense_streams([(qs, k_ref, (m_ref, l_ref, acc_ref))], v_ref, SEQ, vc)
        od = o[:DENSE_TQ] - lam_full * o[DENSE_TQ:]
        od = od * lax.rsqrt(jnp.mean(od * od, axis=-1, keepdims=True) + EPS) * gain
        o_ref[pl.ds(q0, DENSE_TQ), :] = od.astype(BF16)
        return carry

    lax.fori_loop(0, SEQ // DENSE_TQ, q_body, 0)
    o_ref[SEQ:SL, :] = jnp.zeros((CTX, LANES), BF16)


def _diff_attention(q, k, v, lam, subln, lambda_init, b):
    slab = lambda: pl.BlockSpec((None, SL, LANES), lambda i, j: (i, 0, j))
    rows = 2 * DENSE_TQ
    return pl.pallas_call(
        functools.partial(_diff_kernel, lambda_init),
        grid=(b, DIFF_HEADS),
        in_specs=[pl.BlockSpec(lam.shape, lambda i, j: (0, 0)),
                  pl.BlockSpec((1, LANES), lambda i, j: (0, 0)),
                  slab(), slab(), slab()],
        out_specs=slab(),
        out_shape=jax.ShapeDtypeStruct(v.shape, BF16),
        scratch_shapes=[pltpu.VMEM((rows, LANES), F32)] * 3,
        compiler_params=_cparams("parallel", "parallel"),
        name="diff_attention",
    )(lam, subln.reshape(1, LANES), q, k, v)


def _mla_kernel(q_ref, k_ref, v_ref, o_ref, *scratch):
    lo = _lane_iota((1, LANES)) < 64
    vc = v_ref[SEQ:SL, :]

    def two_heads(q, n_lat):
        streams = []
        for hd in range(2):
            sl = slice(hd * LANES, (hd + 1) * LANES)
            streams.append((q[:, sl], k_ref.at[:, sl], scratch[3 * hd:3 * hd + 3]))
        outs = _dense_streams(streams, v_ref, n_lat, vc)
        return jnp.where(lo, outs[0], outs[1])

    def q_body(i, carry):
        q0 = pl.multiple_of(i * DENSE_TQ, DENSE_TQ)
        o = two_heads(q_ref[pl.ds(q0, DENSE_TQ), :], SEQ)
        o_ref[pl.ds(q0, DENSE_TQ), :] = o.astype(BF16)
        return carry

    lax.fori_loop(0, SEQ // DENSE_TQ, q_body, 0)
    o_ref[SEQ:SL, :] = two_heads(q_ref[SEQ:SL, :], 0).astype(BF16)


def _mla_attention(q, k, v, b):
    slabq = lambda: pl.BlockSpec((None, SL, 2 * LANES), lambda i, j: (i, 0, j))
    slabv = lambda: pl.BlockSpec((None, SL, LANES), lambda i, j: (i, 0, j))
    return pl.pallas_call(
        _mla_kernel,
        grid=(b, MLA_HEADS // 2),
        in_specs=[slabq(), slabq(), slabv()],
        out_specs=slabv(),
        out_shape=jax.ShapeDtypeStruct(v.shape, BF16),
        scratch_shapes=[pltpu.VMEM((DENSE_TQ, LANES), F32)] * 6,
        compiler_params=_cparams("parallel", "parallel"),
        name="mla_attention",
    )(q, k, v)


def _oproj_kernel(a_ref, x_ref, wo_ref, g1_ref, ng_ref, sc_ref, sh_ref, wr_hi_ref, wr_lo_ref, br_ref,
                  xo_ref, h_ref, route_ref, cnt_ref):
    y = _dot(a_ref[...], wo_ref[...])
    x = x_ref[...] + g1_ref[...] * y
    xo_ref[...] = x
    h = _modulated(x, ng_ref[...], sc_ref[...], sh_ref[...])
    hi = h.astype(BF16)
    h_ref[...] = hi
    lo = (h - hi.astype(F32)).astype(BF16)
    z = _dot(hi, wr_hi_ref[...]) + _dot(lo, wr_hi_ref[...]) + _dot(hi, wr_lo_ref[...]) + br_ref[...]
    lane = _lane_iota(z.shape)
    lane_f = lane.astype(F32)
    big = jnp.float32(1e9)
    is_g = lane < N_GROUPS
    zg = jnp.where(is_g, z, NEG)
    gmax = jnp.max(zg, axis=-1, keepdims=True)
    gsum = jnp.sum(jnp.where(is_g, jnp.exp(zg - gmax), 0.0), axis=-1, keepdims=True)
    g_w = 1.0 / gsum
    g_idx = jnp.min(jnp.where(is_g & (zg == gmax), lane_f, big), axis=-1, keepdims=True)
    e_lo = N_GROUPS + EXPERTS_PER_GROUP * g_idx
    is_e = (lane_f >= e_lo) & (lane_f < e_lo + EXPERTS_PER_GROUP)
    ze = jnp.where(is_e, z, NEG)
    z1 = jnp.max(ze, axis=-1, keepdims=True)
    i1 = jnp.min(jnp.where(is_e & (ze == z1), lane_f, big), axis=-1, keepdims=True)
    ze2 = jnp.where(lane_f == i1, NEG, ze)
    z2 = jnp.max(ze2, axis=-1, keepdims=True)
    i2 = jnp.min(jnp.where(is_e & (ze2 == z2) & (lane_f != i1), lane_f, big), axis=-1, keepdims=True)
    r = jnp.exp(z2 - z1)
    gate1 = g_w / (1.0 + r)
    gate2 = g_w * r / (1.0 + r)
    e1 = i1 - N_GROUPS
    e2 = i2 - N_GROUPS
    route = jnp.where(lane == 0, gate1, jnp.where(lane == 1, gate2,
                      jnp.where(lane == 2, e1, jnp.where(lane == 3, e2, 0.0))))
    route_ref[...] = route

    @pl.when(pl.program_id(0) == 0)
    def _():
        cnt_ref[...] = jnp.zeros_like(cnt_ref)

    hits = jnp.where(lane_f == e1, 1.0, 0.0) + jnp.where(lane_f == e2, 1.0, 0.0)
    cnt_ref[...] += jnp.sum(hits, axis=0, keepdims=True)


def _route_pos_kernel(route_ref, base_ref, pos_ref, carry_ref):
    @pl.when(pl.program_id(0) == 0)
    def _():
        carry_ref[...] = base_ref[...]

    lane = _lane_iota((TM, LANES))
    lane_f = lane.astype(F32)
    r = lax.broadcasted_iota(jnp.int32, (TM, TM), 0)
    c = lax.broadcasted_iota(jnp.int32, (TM, TM), 1)
    tri = jnp.where(c <= r, 1.0, 0.0).astype(BF16)
    carry = carry_ref[...]
    for s in range(TILES_PER_SLAB):
        route = route_ref[s * TM:(s + 1) * TM, :]
        oh0 = jnp.where(lane_f == route[:, 2:3], 1.0, 0.0)
        oh1 = jnp.where(lane_f == route[:, 3:4], 1.0, 0.0)
        cnt0 = jnp.sum(oh0, axis=0, keepdims=True)
        pre0 = _dot(tri, oh0.astype(BF16)) - 1.0 + carry
        pre1 = _dot(tri, oh1.astype(BF16)) - 1.0 + carry + cnt0
        pos0 = jnp.sum(oh0 * pre0, axis=-1, keepdims=True)
        pos1 = jnp.sum(oh1 * pre1, axis=-1, keepdims=True)
        pos_ref[s * TM:(s + 1) * TM, :] = jnp.where(lane == 0, pos0, jnp.where(lane == 1, pos1, 0.0))
        carry = carry + cnt0 + jnp.sum(oh1, axis=0, keepdims=True)
    carry_ref[...] = carry


def _route_pos(route, base):
    t = route.shape[0]
    return pl.pallas_call(
        _route_pos_kernel,
        grid=(t // SL,),
        in_specs=[pl.BlockSpec((SL, LANES), lambda i: (i, 0)), pl.BlockSpec((1, LANES), lambda i: (0, 0))],
        out_specs=pl.BlockSpec((SL, LANES), lambda i: (i, 0)),
        out_shape=jax.ShapeDtypeStruct((t, LANES), F32),
        scratch_shapes=[pltpu.VMEM((1, LANES), F32)],
        compiler_params=_cparams("arbitrary"),
        name="route_pos",
    )(route, base)


def _oproj_router(attn2, x2, mod3, wo_bf, ng, wr_hi, wr_lo, br):
    t, d = x2.shape
    n_in = attn2.shape[1]
    full = lambda a: pl.BlockSpec(a.shape, lambda i: (0,) * a.ndim)
    row = lambda w: pl.BlockSpec((TM, w), lambda i: (i, 0))
    return pl.pallas_call(
        _oproj_kernel,
        grid=(t // TM,),
        in_specs=[row(n_in), row(d), full(wo_bf), _mod_spec(2), pl.BlockSpec((1, d), lambda i: (0, 0)),
                  _mod_spec(4), _mod_spec(3), full(wr_hi), full(wr_lo), full(br)],
        out_specs=[row(d), row(d), row(LANES), pl.BlockSpec((1, LANES), lambda i: (0, 0))],
        out_shape=[jax.ShapeDtypeStruct((t, d), F32), jax.ShapeDtypeStruct((t, d), BF16),
                   jax.ShapeDtypeStruct((t, LANES), F32), jax.ShapeDtypeStruct((1, LANES), F32)],
        compiler_params=_cparams("arbitrary"),
        name="oproj_router",
    )(attn2, x2, wo_bf, mod3, ng.reshape(1, d), mod3, mod3, wr_hi, wr_lo, br)


def _expert_kernel(be_ref, nu_ref, x_ref, wgu_ref, wd_ref, o_ref, wgu_bf, wd_bf):
    i = pl.program_id(0)
    new_expert = (i == 0) | (be_ref[i] != be_ref[jnp.maximum(i - 1, 0)])

    @pl.when(new_expert)
    def _():
        wgu_bf[...] = wgu_ref[...].astype(BF16)
        wd_bf[...] = wd_ref[...].astype(BF16)

    @pl.when(i < nu_ref[0])
    def _():
        gu = _dot(x_ref[...], wgu_bf[...])
        gte, up = gu[:, :D_EXPERT], gu[:, D_EXPERT:]
        a = (gte * jax.nn.sigmoid(gte) * up).astype(BF16)
        o_ref[...] = _dot(a, wd_bf[...]).astype(BF16)

    @pl.when(i >= nu_ref[0])
    def _():
        o_ref[...] = jnp.zeros_like(o_ref)


def _experts(xs, block_expert, n_used, wgu, wd, layer):
    n_rows, d = xs.shape
    n_blocks = n_rows // MOE_BLK
    return pl.pallas_call(
        _expert_kernel,
        grid_spec=pltpu.PrefetchScalarGridSpec(
            num_scalar_prefetch=2,
            grid=(n_blocks,),
            in_specs=[pl.BlockSpec((MOE_BLK, d), lambda i, be, nu: (i, 0)),
                      pl.BlockSpec((None, None, d, 2 * D_EXPERT), lambda i, be, nu: (layer, be[i], 0, 0)),
                      pl.BlockSpec((None, None, D_EXPERT, d), lambda i, be, nu: (layer, be[i], 0, 0))],
            out_specs=pl.BlockSpec((MOE_BLK, d), lambda i, be, nu: (i, 0)),
            scratch_shapes=[pltpu.VMEM((d, 2 * D_EXPERT), BF16), pltpu.VMEM((D_EXPERT, d), BF16)]),
        out_shape=jax.ShapeDtypeStruct((n_rows, d), BF16),
        compiler_params=_cparams("arbitrary"),
        name="experts",
    )(block_expert, n_used, xs, wgu, wd)


def _combine_kernel(x_ref, y0_ref, y1_ref, route_ref, g2_ref, o_ref):
    o_ref[...] = _residual_with_moe(x_ref, y0_ref, y1_ref, route_ref, g2_ref)


def _combine(x2, y0, y1, route, mod3):
    t, d = x2.shape
    row = lambda w: pl.BlockSpec((TM, w), lambda i: (i, 0))
    return pl.pallas_call(
        _combine_kernel,
        grid=(t // TM,),
        in_specs=[row(d), row(d), row(d), row(LANES), _mod_spec(5)],
        out_specs=row(d),
        out_shape=jax.ShapeDtypeStruct((t, d), F32),
        compiler_params=_cparams("parallel"),
        name="moe_combine",
    )(x2, y0, y1, route, mod3)


def _moe_dispatch(route, counts_f, t):
    counts = counts_f[0, :N_EXPERTS].astype(jnp.int32)
    padded = (counts + MOE_BLK - 1) // MOE_BLK * MOE_BLK
    seg_end = jnp.cumsum(padded)
    seg_start = seg_end - padded
    base = jnp.zeros((1, LANES), F32).at[0, :N_EXPERTS].set(seg_start.astype(F32))
    pos_f = _route_pos(route, base)
    pos = pos_f[:, :2].T.astype(jnp.int32)
    n_rows = -(-(2 * t + N_EXPERTS * (MOE_BLK - 1)) // MOE_BLK) * MOE_BLK
    n_blocks = n_rows // MOE_BLK
    token = jnp.tile(jnp.arange(t, dtype=jnp.int32), 2)
    row_token = jnp.zeros((n_rows,), jnp.int32).at[pos.reshape(-1)].set(
        token, mode="promise_in_bounds", unique_indices=True)
    block_start = jnp.arange(n_blocks, dtype=jnp.int32) * MOE_BLK
    block_expert = jnp.minimum(jnp.sum((seg_end[None, :] <= block_start[:, None]).astype(jnp.int32), axis=1),
                               N_EXPERTS - 1)
    n_used = (seg_end[-1] // MOE_BLK).astype(jnp.int32).reshape(1)
    return row_token, pos, block_expert, n_used


def _take_rows(a, idx):
    return a.at[idx].get(mode="promise_in_bounds")


def _moe_rows(h_bf, route, counts_f, wgu, wd, layer):
    t = h_bf.shape[0]
    row_token, pos, block_expert, n_used = _moe_dispatch(route, counts_f, t)
    xs = _take_rows(h_bf, row_token)
    ys = _experts(xs, block_expert, n_used, wgu, wd, layer)
    return _take_rows(ys, pos[0]), _take_rows(ys, pos[1])


def kernel(x, c, ctx, c_ctx, ada_w, ada_b, norm_g, moe_w_router_group, moe_b_router_group, moe_w_router_expert, moe_b_router_expert, moe_w_gate_up, moe_w_down, na_w_qkv, na_w_o, na_q_norm, na_k_norm, na_rpb, sw_w_qkv, sw_w_o, sw_q_norm, sw_k_norm, sw_sink, mla_w_dqkv, mla_q_a_norm, mla_kv_a_norm, mla_w_uq, mla_w_ukv, mla_q_norm, mla_k_norm, mla_w_o, diff_w_qkv, diff_q_norm, diff_k_norm, diff_lambda, diff_subln, diff_w_o):
    slab = _stack(x, c, ctx, c_ctx, ada_w, ada_b, norm_g, moe_w_router_group, moe_b_router_group, moe_w_router_expert, moe_b_router_expert, moe_w_gate_up, moe_w_down, na_w_qkv, na_w_o, na_q_norm, na_k_norm, na_rpb, sw_w_qkv, sw_w_o, sw_q_norm, sw_k_norm, sw_sink, mla_w_dqkv, mla_q_a_norm, mla_kv_a_norm, mla_w_uq, mla_w_ukv, mla_q_norm, mla_k_norm, mla_w_o, diff_w_qkv, diff_q_norm, diff_k_norm, diff_lambda, diff_subln, diff_w_o)
    return slab[:, :SEQ]


def _stack(x, c, ctx, c_ctx, ada_w, ada_b, norm_g, moe_w_router_group, moe_b_router_group, moe_w_router_expert, moe_b_router_expert, moe_w_gate_up, moe_w_down, na_w_qkv, na_w_o, na_q_norm, na_k_norm, na_rpb, sw_w_qkv, sw_w_o, sw_q_norm, sw_k_norm, sw_sink, mla_w_dqkv, mla_q_a_norm, mla_kv_a_norm, mla_w_uq, mla_w_ukv, mla_q_norm, mla_k_norm, mla_w_o, diff_w_qkv, diff_q_norm, diff_k_norm, diff_lambda, diff_subln, diff_w_o):
    b, s, d = x.shape
    assert (s, d) == (SEQ, D_MODEL) and ctx.shape == (b, CTX, d) and b <= 8
    depth = ada_w.shape[0]
    assert depth <= 4
    t = b * SL

    cc = jnp.zeros((16, d), F32).at[:b].set(c).at[8].set(c_ctx)
    mod = _ada_table(cc, ada_w, ada_b)
    x2 = jnp.concatenate([x, ctx], axis=1).reshape(t, d)

    lane = np.arange(LANES)
    bd64 = _block_diag(lane // 64)
    lm64 = lane % 64
    rope64 = _rope_tables(64, lm64)
    scale64 = 64 ** -0.5 * LOG2E

    r3 = lambda a: a.reshape(b, SL, a.shape[-1])
    for l in range(depth):
        mod3 = mod[l].reshape(16 * 6, 1, d)
        if l == 0:
            w = na_w_qkv[0].astype(BF16)
            q, k, v = _project(x2, mod3, norm_g[l, 0], w, _qkv_plan(1024, 1024, 1024), (1024, 1024, 1024),
                               bd64, _head64_vec(na_q_norm[0], na_k_norm[0], scale64), name="na_proj")
            attn = _na_attention(r3(q), r3(k), r3(v), _na_bias_table(na_rpb[0]), b)
            w_o = na_w_o[0]
        elif l == 1:
            w = sw_w_qkv[0].astype(BF16)
            q, k, v, x2 = _project(x2, mod3, norm_g[l, 0], w, _qkv_plan(1024, 256, 256), (1024, 256, 256),
                                   bd64, _head64_vec(sw_q_norm[0], sw_k_norm[0], scale64),
                                   rope=rope64, rope_half=32, pending=pending, name="sw_proj")
            attn = _sw_attention(r3(q), r3(k), r3(v), sw_sink[0] * LOG2E, b)
            w_o = sw_w_o[0]
        elif l == 2:
            p = dict(w_dqkv=mla_w_dqkv[0], q_a_norm=mla_q_a_norm[0], kv_a_norm=mla_kv_a_norm[0],
                     w_uq=mla_w_uq[0], w_ukv=mla_w_ukv[0], q_norm=mla_q_norm[0], k_norm=mla_k_norm[0])
            q, k, v, x2 = _mla_project(x2, mod3, norm_g[l, 0], p, pending)
            attn = _mla_attention(r3(q), r3(k), r3(v), b)
            w_o = mla_w_o[0]
        else:
            w = diff_w_qkv[0].astype(BF16)
            q, k, v, x2 = _project(x2, mod3, norm_g[l, 0], w, _qkv_plan(1024, 1024, 1024), (1024, 1024, 1024),
                                   bd64, _head64_vec(diff_q_norm[0], diff_k_norm[0], scale64),
                                   rope=rope64, rope_half=32, pending=pending, name="diff_proj")
            lambda_init = 0.8 - 0.6 * math.exp(-0.3 * l)
            attn = _diff_attention(r3(q), r3(k), r3(v), diff_lambda[0], diff_subln[0], lambda_init, b)
            w_o = diff_w_o[0]

        wr = jnp.concatenate([moe_w_router_group[l], moe_w_router_expert[l]], axis=1)
        wr = jnp.pad(wr, ((0, 0), (0, LANES - wr.shape[1])))
        wr_hi = wr.astype(BF16)
        wr_lo = (wr - wr_hi.astype(F32)).astype(BF16)
        br = jnp.concatenate([moe_b_router_group[l], moe_b_router_expert[l]])
        br = jnp.pad(br, (0, LANES - br.shape[0])).reshape(1, LANES)
        x2, h_bf, route, counts_f = _oproj_router(attn.reshape(t, -1), x2, mod3, w_o.astype(BF16), norm_g[l, 1],
                                                  wr_hi, wr_lo, br)
        y0, y1 = _moe_rows(h_bf, route, counts_f, moe_w_gate_up, moe_w_down, l)
        pending = (y0, y1, route, mod3)

    return _combine(x2, *pending).reshape(b, SL, d)
```

```python
import functools
import math

import jax
import jax.numpy as jnp
import numpy as np
from jax import lax
from jax.experimental import pallas as pl
from jax.experimental.pallas import tpu as pltpu

F32 = jnp.float32
BF16 = jnp.bfloat16

D_MODEL = 1024
SEQ = 4096
CTX = 256
SL = SEQ + CTX
GRID_W = 64
GRID_ROWS = SEQ // GRID_W
EPS = 1e-6
ROPE_BASE = 10000.0
NEG = -1e30
LOG2E = 1.4426950408889634

NA_KH, NA_KW = 8, 16
SW_WINDOW = 128
SW_HEADS, SW_KV_HEADS = 16, 4
MLA_Q_RANK, MLA_KV_RANK, MLA_NOPE, MLA_ROPE, MLA_V, MLA_HEADS = 384, 256, 64, 32, 64, 16
DIFF_HEADS = 8
N_GROUPS, EXPERTS_PER_GROUP, N_EXPERTS, D_EXPERT = 4, 8, 32, 256

LANES = 128
MXU_N = 256
TM = 256
TILES_PER_SLAB = SL // TM
MOE_BLK = 256
VMEM_LIMIT = 56 * 1024 * 1024


def _cparams(*sem):
    return pltpu.CompilerParams(dimension_semantics=sem, vmem_limit_bytes=VMEM_LIMIT)


def _lane_iota(shape):
    return lax.broadcasted_iota(jnp.int32, shape, len(shape) - 1)


def _dot(a, b):
    return jnp.dot(a, b, preferred_element_type=F32)


def _dot_nt(a, b):
    return lax.dot_general(a, b, (((1,), (1,)), ((), ())), preferred_element_type=F32)


def _ada_kernel(cc_ref, w_ref, b_ref, o_ref):
    a = cc_ref[...]
    a = a * jax.nn.sigmoid(a)
    hi = a.astype(BF16)
    lo = (a - hi.astype(F32)).astype(BF16)
    w = w_ref[...]
    whi = w.astype(BF16)
    wlo = (w - whi.astype(F32)).astype(BF16)
    o_ref[...] = _dot(hi, whi) + _dot(lo, whi) + _dot(hi, wlo) + b_ref[...]


def _ada_table(cc, ada_w, ada_b):
    depth, d, n = ada_w.shape
    tn = 1024
    return pl.pallas_call(
        _ada_kernel,
        grid=(depth, n // tn),
        in_specs=[pl.BlockSpec((16, d), lambda l, j: (0, 0)),
                  pl.BlockSpec((None, d, tn), lambda l, j: (l, 0, j)),
                  pl.BlockSpec((None, 1, tn), lambda l, j: (l, 0, j))],
        out_specs=pl.BlockSpec((None, 16, tn), lambda l, j: (l, 0, j)),
        out_shape=jax.ShapeDtypeStruct((depth, 16, n), F32),
        compiler_params=_cparams("parallel", "parallel"),
        name="ada_table",
    )(cc, ada_w, ada_b.reshape(depth, 1, n))


def _mod_spec(part):
    def index(i):
        m = jnp.where(i % TILES_PER_SLAB == TILES_PER_SLAB - 1, 8, i // TILES_PER_SLAB)
        return (m * 6 + part, 0, 0)
    return pl.BlockSpec((None, 1, D_MODEL), index)


def _modulated(x, g, sc, sh):
    ms = jnp.mean(x * x, axis=-1, keepdims=True)
    return (x * lax.rsqrt(ms + EPS)) * g * (1.0 + sc) + sh


def _group_norm_rope(y, bd, invcnt, gain, cos=None, sin=None, half=None):
    ss = _dot((y * y).astype(BF16), bd)
    yn = y * lax.rsqrt(ss * invcnt + EPS) * gain
    if cos is not None:
        lane = _lane_iota(yn.shape)
        fwd = pltpu.roll(yn, LANES - half, 1)
        bwd = pltpu.roll(yn, half, 1)
        partner = jnp.where((lane & (2 * half - 1)) < half, fwd, bwd)
        yn = yn * cos + partner * sin
    return yn


def _block_diag(group_of_lane):
    g = np.asarray(group_of_lane)
    return jnp.asarray((g[:, None] == g[None, :]).astype(np.float32), dtype=BF16)


def _rope_tables(rot_dim, lane_map):
    t = jnp.arange(SEQ, dtype=jnp.int32)
    n_freq = rot_dim // 4
    inv_freq = ROPE_BASE ** (-jnp.arange(n_freq, dtype=F32) / n_freq)
    rows = (t // GRID_W).astype(F32)
    cols = (t % GRID_W).astype(F32)
    ang = jnp.concatenate([rows[:, None] * inv_freq, cols[:, None] * inv_freq], axis=-1)
    cos, sin = jnp.cos(ang), jnp.sin(ang)
    lm = np.asarray(lane_map)
    rot = lm >= 0
    idx = np.where(rot, lm % (rot_dim // 2), 0)
    sign = np.where(lm < rot_dim // 2, -1.0, 1.0).astype(np.float32)
    cos_t = jnp.where(rot[None, :], cos[:, idx], 1.0)
    sin_t = jnp.where(rot[None, :], sin[:, idx] * sign[None, :], 0.0)
    ident_c = jnp.ones((CTX, LANES), F32)
    ident_s = jnp.zeros((CTX, LANES), F32)
    return jnp.concatenate([cos_t, ident_c], 0), jnp.concatenate([sin_t, ident_s], 0)


def _residual_with_moe(x_ref, y0_ref, y1_ref, route_ref, g2_ref):
    route = route_ref[...]
    y = route[:, 0:1] * y0_ref[...].astype(F32) + route[:, 1:2] * y1_ref[...].astype(F32)
    return x_ref[...] + g2_ref[...] * y


def _proj_kernel(plan, rope_half, has_pending, x_ref, g_ref, sc_ref, sh_ref, w_ref, bd_ref, vec_ref, *rest):
    if has_pending:
        x = _residual_with_moe(x_ref, *rest[:4])
        rest = rest[4:]
        rest[-1][...] = x
        rest = rest[:-1]
    else:
        x = x_ref[...]
    if rope_half is not None:
        cos_ref, sin_ref = rest[:2]
        outs = rest[2:]
        cos, sin = cos_ref[...], sin_ref[...]
    else:
        outs = rest
        cos = sin = None
    h = _modulated(x, g_ref[...], sc_ref[...], sh_ref[...]).astype(BF16)
    bd = bd_ref[...]
    for c0, ep, out_idx, oc0 in plan:
        acc = _dot(h, w_ref[:, c0:c0 + MXU_N])
        for s in range(MXU_N // LANES):
            y = acc[:, s * LANES:(s + 1) * LANES]
            if ep is not None:
                y = _group_norm_rope(y, bd, vec_ref[ep, 0:1, :], vec_ref[ep, 1:2, :], cos, sin, rope_half)
            lo = oc0 + s * LANES
            outs[out_idx][:, lo:lo + LANES] = y.astype(BF16)


def _pending_specs(pending, d):
    row = lambda w: pl.BlockSpec((TM, w), lambda i: (i, 0))
    return [row(d), row(d), row(LANES), _mod_spec(5)], list(pending)


def _project(x2, mod3, g, w_bf, plan, out_widths, bd, vec, rope=None, rope_half=None, pending=None, name="proj"):
    t, d = x2.shape
    n = w_bf.shape[1]
    n_tiles = t // TM
    in_specs = [pl.BlockSpec((TM, d), lambda i: (i, 0)),
                pl.BlockSpec((1, d), lambda i: (0, 0)),
                _mod_spec(1), _mod_spec(0),
                pl.BlockSpec((d, n), lambda i: (0, 0)),
                pl.BlockSpec(bd.shape, lambda i: (0, 0)),
                pl.BlockSpec(vec.shape, lambda i: (0, 0, 0))]
    args = [x2, g.reshape(1, d), mod3, mod3, w_bf, bd, vec]
    out_specs = [pl.BlockSpec((TM, w), lambda i: (i, 0)) for w in out_widths]
    out_shape = [jax.ShapeDtypeStruct((t, w), BF16) for w in out_widths]
    if pending is not None:
        specs, pargs = _pending_specs(pending, d)
        in_specs += specs
        args += pargs
        out_specs.append(pl.BlockSpec((TM, d), lambda i: (i, 0)))
        out_shape.append(jax.ShapeDtypeStruct((t, d), F32))
    if rope is not None:
        in_specs += [pl.BlockSpec((TM, LANES), lambda i: (i % TILES_PER_SLAB, 0))] * 2
        args += list(rope)
    return pl.pallas_call(
        functools.partial(_proj_kernel, plan, rope_half, pending is not None),
        grid=(n_tiles,),
        in_specs=in_specs,
        out_specs=out_specs,
        out_shape=out_shape,
        compiler_params=_cparams("parallel"),
        name=name,
    )(*args)


def _qkv_plan(nq, nk, nv):
    plan = []
    for c0 in range(0, nq, MXU_N):
        plan.append((c0, 0, 0, c0))
    for c0 in range(0, nk, MXU_N):
        plan.append((nq + c0, 1, 1, c0))
    for c0 in range(0, nv, MXU_N):
        plan.append((nq + nk + c0, None, 2, c0))
    return tuple(plan)


def _head64_vec(q_norm, k_norm, q_scale):
    inv = jnp.full((LANES,), 1.0 / 64, F32)
    z = jnp.zeros((6, LANES), F32)
    vq = jnp.concatenate([inv[None], (jnp.tile(q_norm, 2) * q_scale)[None], z], 0)
    vk = jnp.concatenate([inv[None], jnp.tile(k_norm, 2)[None], z], 0)
    return jnp.stack([vq, vk], 0)


def _mla_proj_kernel(x_ref, g_ref, sc_ref, sh_ref, wd_ref, wuq_ref, wuk_ref, wuv_ref, bd_ref, vec_ref,
                     an_ref, cos_ref, sin_ref, y0_ref, y1_ref, route_ref, g2_ref, q_out, k_out, v_out, x_out):
    x = _residual_with_moe(x_ref, y0_ref, y1_ref, route_ref, g2_ref)
    x_out[...] = x
    h = _modulated(x, g_ref[...], sc_ref[...], sh_ref[...]).astype(BF16)
    cos, sin = cos_ref[...], sin_ref[...]
    half = MLA_ROPE // 2
    cq = _dot(h, wd_ref[:, 0:MLA_Q_RANK])
    ckv = _dot(h, wd_ref[:, MLA_Q_RANK:MLA_Q_RANK + MLA_KV_RANK])
    kr = _dot(h, wd_ref[:, MLA_Q_RANK + MLA_KV_RANK:])
    cq = cq * lax.rsqrt(jnp.mean(cq * cq, axis=-1, keepdims=True) + EPS) * an_ref[0:1, 0:MLA_Q_RANK]
    ckv = ckv * lax.rsqrt(jnp.mean(ckv * ckv, axis=-1, keepdims=True) + EPS) * an_ref[1:2, 0:MLA_KV_RANK]
    cq = cq.astype(BF16)
    ckv = ckv.astype(BF16)
    kr = _group_norm_rope(kr, bd_ref[1], vec_ref[2, 0:1, :], vec_ref[2, 1:2, :], cos[:, LANES:], sin[:, LANES:], half)
    kr = pltpu.roll(kr, MLA_NOPE, 1)
    for hd in range(MLA_HEADS):
        c0 = hd * LANES
        qh = _dot(cq, wuq_ref[:, c0:c0 + LANES])
        qh = _group_norm_rope(qh, bd_ref[0], vec_ref[0, 0:1, :], vec_ref[0, 1:2, :], cos[:, :LANES], sin[:, :LANES], half)
        q_out[:, c0:c0 + LANES] = qh.astype(BF16)
        kh = _dot(ckv, wuk_ref[:, c0:c0 + LANES])
        kh = _group_norm_rope(kh, bd_ref[0], vec_ref[1, 0:1, :], vec_ref[1, 1:2, :]) + kr
        k_out[:, c0:c0 + LANES] = kh.astype(BF16)
    for c0 in range(0, MLA_HEADS * MLA_V, MXU_N):
        v_out[:, c0:c0 + MXU_N] = _dot(ckv, wuv_ref[:, c0:c0 + MXU_N]).astype(BF16)


def _mla_project(x2, mod3, g, p, pending):
    t, d = x2.shape
    hq = MLA_NOPE + MLA_ROPE
    n_tiles = t // TM
    wd = jnp.pad(p["w_dqkv"], ((0, 0), (0, LANES - MLA_ROPE))).astype(BF16)
    wuq = p["w_uq"].reshape(MLA_Q_RANK, MLA_HEADS, hq)
    wuq = jnp.pad(wuq, ((0, 0), (0, 0), (0, LANES - hq))).reshape(MLA_Q_RANK, MLA_HEADS * LANES).astype(BF16)
    wukv = p["w_ukv"].reshape(MLA_KV_RANK, MLA_HEADS, MLA_NOPE + MLA_V)
    wuk = jnp.pad(wukv[:, :, :MLA_NOPE], ((0, 0), (0, 0), (0, LANES - MLA_NOPE)))
    wuk = wuk.reshape(MLA_KV_RANK, MLA_HEADS * LANES).astype(BF16)
    wuv = wukv[:, :, MLA_NOPE:].reshape(MLA_KV_RANK, MLA_HEADS * MLA_V).astype(BF16)
    scale = (MLA_NOPE + MLA_ROPE) ** -0.5 * LOG2E
    lane = np.arange(LANES)
    grp_head = np.where(lane < 64, 0, np.where(lane < 96, 1, 2))
    grp_kr = np.where(lane < 32, 0, 1)
    bd = jnp.stack([_block_diag(grp_head), _block_diag(grp_kr)], 0)
    inv_head = jnp.asarray(np.where(lane < 64, 1 / 64, 1 / 32), F32)
    zpad = jnp.zeros((LANES - hq,), F32)
    gq = jnp.concatenate([p["q_norm"], zpad]) * scale
    gk = jnp.concatenate([p["k_norm"][:MLA_NOPE], jnp.zeros((LANES - MLA_NOPE,), F32)])
    gkr = jnp.concatenate([p["k_norm"][MLA_NOPE:], jnp.zeros((LANES - MLA_ROPE,), F32)])
    inv_kr = jnp.asarray(np.where(lane < 32, 1 / 32, 1 / 96), F32)
    z6 = jnp.zeros((6, LANES), F32)
    vec = jnp.stack([jnp.concatenate([inv_head[None], gq[None], z6], 0),
                     jnp.concatenate([inv_head[None], gk[None], z6], 0),
                     jnp.concatenate([inv_kr[None], gkr[None], z6], 0)], 0)
    an = jnp.zeros((8, MLA_Q_RANK), F32)
    an = an.at[0].set(p["q_a_norm"]).at[1, :MLA_KV_RANK].set(p["kv_a_norm"])
    lm_head = np.where((lane >= 64) & (lane < 96), lane - 64, -1)
    lm_kr = np.where(lane < 32, lane, -1)
    ch, sh_ = _rope_tables(MLA_ROPE, lm_head)
    ck, sk = _rope_tables(MLA_ROPE, lm_kr)
    cos2 = jnp.concatenate([ch, ck], 1)
    sin2 = jnp.concatenate([sh_, sk], 1)

    full = lambda a: pl.BlockSpec(a.shape, lambda i: (0,) * a.ndim)
    rope_spec = pl.BlockSpec((TM, 2 * LANES), lambda i: (i % TILES_PER_SLAB, 0))
    out_w = (MLA_HEADS * LANES, MLA_HEADS * LANES, MLA_HEADS * MLA_V)
    pend_specs, pend_args = _pending_specs(pending, d)
    return pl.pallas_call(
        _mla_proj_kernel,
        grid=(n_tiles,),
        in_specs=[pl.BlockSpec((TM, d), lambda i: (i, 0)), pl.BlockSpec((1, d), lambda i: (0, 0)),
                  _mod_spec(1), _mod_spec(0), full(wd), full(wuq), full(wuk), full(wuv), full(bd), full(vec),
                  full(an), rope_spec, rope_spec] + pend_specs,
        out_specs=[pl.BlockSpec((TM, w), lambda i: (i, 0)) for w in out_w + (d,)],
        out_shape=[jax.ShapeDtypeStruct((t, w), BF16) for w in out_w] + [jax.ShapeDtypeStruct((t, d), F32)],
        compiler_params=_cparams("parallel"),
        name="mla_proj",
    )(x2, g.reshape(1, d), mod3, mod3, wd, wuq, wuk, wuv, bd, vec, an, cos2, sin2, *pend_args)


def _softmax_pv(s_list, v_list, sink=None):
    m = functools.reduce(jnp.maximum, [jnp.max(s, axis=-1, keepdims=True) for s in s_list])
    if sink is not None:
        m = jnp.maximum(m, sink)
    ps = [jnp.exp2(s - m) for s in s_list]
    l = functools.reduce(lambda a, b: a + b, [jnp.sum(p, axis=-1, keepdims=True) for p in ps])
    if sink is not None:
        l = l + jnp.exp2(sink - m)
    o = functools.reduce(lambda a, b: a + b, [_dot(p.astype(BF16), v) for p, v in zip(ps, v_list)])
    return o * (1.0 / l)


def _split_pair(x, lo):
    zero = jnp.zeros_like(x)
    return jnp.where(lo, x, zero), jnp.where(lo, zero, x)


def _na_kernel(q_ref, k_ref, v_ref, bias_ref, o_ref):
    lo = _lane_iota((1, LANES)) < 64
    kc = k_ref[SEQ:SL, :]
    vc = v_ref[SEQ:SL, :]
    nb = NA_KH * GRID_W

    def pair_attend(q, ks, vs, bias):
        n = q.shape[0]
        qa, qb = _split_pair(q, lo)
        qs = jnp.concatenate([qa, qb], axis=0)
        s_list = [_dot_nt(qs, k) for k in ks]
        if bias is not None:
            s_list[0] = s_list[0] + bias
        o = _softmax_pv(s_list, vs)
        return jnp.where(lo, o[:n], o[n:])

    def row_body(r, carry):
        row0 = jnp.clip(r - NA_KH // 2, 0, GRID_ROWS - NA_KH)
        q = q_ref[pl.ds(pl.multiple_of(r * GRID_W, GRID_W), GRID_W), :]
        ks = pl.multiple_of(row0 * GRID_W, GRID_W)
        kn = k_ref[pl.ds(ks, nb), :]
        vn = v_ref[pl.ds(ks, nb), :]
        o = pair_attend(q, [kn, kc], [vn, vc], bias_ref[r - row0])
        o_ref[pl.ds(pl.multiple_of(r * GRID_W, GRID_W), GRID_W), :] = o.astype(BF16)
        return carry

    lax.fori_loop(0, GRID_ROWS, row_body, 0, unroll=8)
    o_ref[SEQ:SL, :] = pair_attend(q_ref[SEQ:SL, :], [kc], [vc], None).astype(BF16)


def _na_bias_table(rpb):
    heads = rpb.shape[0]
    n_col = 2 * NA_KW - 1
    r2 = rpb.reshape(heads, 2 * NA_KH - 1, n_col) * LOG2E
    padded = jnp.pad(r2, ((0, 0), (0, 0), (GRID_W, GRID_W)))
    base = GRID_W + NA_KW - 1
    toeplitz = jnp.stack([padded[:, :, base - qc:base - qc + GRID_W] for qc in range(GRID_W)], axis=2)
    qc = np.arange(GRID_W)[:, None, None]
    kc = np.arange(GRID_W)[None, None, :]
    col0 = np.clip(qc - NA_KW // 2, 0, GRID_W - NA_KW)
    valid = np.broadcast_to((kc >= col0) & (kc < col0 + NA_KW), (GRID_W, NA_KH, GRID_W))
    valid = valid.reshape(GRID_W, NA_KH * GRID_W)
    classes = []
    for c in range(NA_KH):
        t = toeplitz[:, NA_KH - 1 - c:2 * NA_KH - 1 - c]
        t = t.transpose(0, 2, 1, 3).reshape(heads, GRID_W, NA_KH * GRID_W)
        classes.append(jnp.where(valid[None], t, NEG))
    tab = jnp.stack(classes, 0)
    return tab.reshape(NA_KH, heads // 2, 2 * GRID_W, NA_KH * GRID_W)


def _na_attention(q, k, v, bias, b):
    hp = q.shape[-1] // LANES
    slab = lambda: pl.BlockSpec((None, SL, LANES), lambda i, j: (i, 0, j))
    return pl.pallas_call(
        _na_kernel,
        grid=(b, hp),
        in_specs=[slab(), slab(), slab(),
                  pl.BlockSpec((NA_KH, None, 2 * GRID_W, NA_KH * GRID_W), lambda i, j: (0, j, 0, 0))],
        out_specs=slab(),
        out_shape=jax.ShapeDtypeStruct(q.shape, BF16),
        compiler_params=_cparams("parallel", "parallel"),
        name="na_attention",
    )(q, k, v, bias)


Q_BLK = 128


def _sw_kernel(sink_ref, q_ref, k_ref, v_ref, o_ref):
    g = pl.program_id(1)
    lane = _lane_iota((1, LANES))
    lo = lane < 64
    first_half = (jnp.zeros((1, LANES), jnp.int32) + (g % 2)) == 0
    target = (lane // 64) == (g % 2)
    kc = k_ref[SEQ:SL, :]
    vc = v_ref[SEQ:SL, :]
    span = Q_BLK + 2 * SW_WINDOW
    sinks = [sink_ref[4 * g + t] for t in range(4)]

    def swap(x):
        return jnp.concatenate([x[:, 64:], x[:, :64]], axis=1)

    def stack_heads(q):
        parts = []
        for blk in range(2):
            qb = q[:, blk * LANES:(blk + 1) * LANES]
            qs = swap(qb)
            even = jnp.where(first_half, qb, qs)
            odd = jnp.where(first_half, qs, qb)
            zero = jnp.zeros_like(qb)
            parts += [jnp.where(target, even, zero), jnp.where(target, odd, zero)]
        return jnp.concatenate(parts, axis=0)

    def unstack_heads(o, n):
        blocks = []
        for blk in range(2):
            oe = o[(2 * blk) * n:(2 * blk + 1) * n]
            oo = o[(2 * blk + 1) * n:(2 * blk + 2) * n]
            oe = jnp.where(first_half, oe, pltpu.roll(oe, 64, 1))
            oo = jnp.where(first_half, pltpu.roll(oo, 64, 1), oo)
            blocks.append(jnp.where(lo, oe, oo))
        return jnp.concatenate(blocks, axis=1)

    def sink_col(n):
        row = lax.broadcasted_iota(jnp.int32, (4 * n, 1), 0)
        return jnp.where(row < n, sinks[0], jnp.where(row < 2 * n, sinks[1],
                         jnp.where(row < 3 * n, sinks[2], sinks[3])))

    def block_body(i, carry):
        q0 = pl.multiple_of(i * Q_BLK, Q_BLK)
        start = pl.multiple_of(jnp.clip((i - 1) * Q_BLK, 0, SEQ - span), Q_BLK)
        qs = stack_heads(q_ref[pl.ds(q0, Q_BLK), :])
        kw = k_ref[pl.ds(start, span), :]
        vw = v_ref[pl.ds(start, span), :]
        s_loc = _dot_nt(qs, kw)
        qpos = q0 + (lax.broadcasted_iota(jnp.int32, (4 * Q_BLK, 1), 0) & (Q_BLK - 1))
        kpos = start + _lane_iota((1, span))
        s_loc = jnp.where(jnp.abs(kpos - qpos) <= SW_WINDOW, s_loc, NEG)
        s_ctx = _dot_nt(qs, kc)
        o = _softmax_pv([s_loc, s_ctx], [vw, vc], sink_col(Q_BLK))
        o_ref[pl.ds(q0, Q_BLK), :] = unstack_heads(o, Q_BLK).astype(BF16)
        return carry

    lax.fori_loop(0, SEQ // Q_BLK, block_body, 0, unroll=4)
    qs = stack_heads(q_ref[SEQ:SL, :])
    o = _softmax_pv([_dot_nt(qs, kc)], [vc], sink_col(CTX))
    o_ref[SEQ:SL, :] = unstack_heads(o, CTX).astype(BF16)


def _sw_attention(q, k, v, sink2, b):
    nq = 4 * 64
    slabq = lambda: pl.BlockSpec((None, SL, nq), lambda i, g: (i, 0, g))
    slabk = lambda: pl.BlockSpec((None, SL, LANES), lambda i, g: (i, 0, g // 2))
    return pl.pallas_call(
        _sw_kernel,
        grid=(b, SW_KV_HEADS),
        in_specs=[pl.BlockSpec(memory_space=pltpu.SMEM), slabq(), slabk(), slabk()],
        out_specs=slabq(),
        out_shape=jax.ShapeDtypeStruct(q.shape, BF16),
        compiler_params=_cparams("parallel", "parallel"),
        name="sw_attention",
    )(sink2, q, k, v)


DENSE_TQ = 512
DENSE_TK = 1024


def _online_step(s, v, m_ref, l_ref, acc_ref):
    chunks = s.shape[1] // LANES
    m_prev = m_ref[...]
    m_new = jnp.maximum(m_prev, jnp.max(s, axis=-1, keepdims=True))
    alpha = jnp.exp2(m_prev - m_new)
    p = jnp.exp2(s - jnp.concatenate([m_new] * chunks, axis=1))
    psum = functools.reduce(lambda a, b: a + b, [p[:, c * LANES:(c + 1) * LANES] for c in range(chunks)])
    l_ref[...] = alpha * l_ref[...] + psum
    acc_ref[...] = alpha * acc_ref[...] + _dot(p.astype(BF16), v)
    m_ref[...] = m_new


def _dense_streams(streams, v_ref, n_lat, vc):
    views = []
    for qs, _, (m_ref, l_ref, acc_ref) in streams:
        rows = qs.shape[0]
        m_ref[0:rows, :] = jnp.full((rows, LANES), NEG, F32)
        l_ref[0:rows, :] = jnp.zeros((rows, LANES), F32)
        acc_ref[0:rows, :] = jnp.zeros((rows, LANES), F32)
        views.append((m_ref.at[0:rows, :], l_ref.at[0:rows, :], acc_ref.at[0:rows, :]))

    for c in range(n_lat // DENSE_TK):
        v = v_ref[c * DENSE_TK:(c + 1) * DENSE_TK, :]
        for (qs, k_ref, _), view in zip(streams, views):
            _online_step(_dot_nt(qs, k_ref[c * DENSE_TK:(c + 1) * DENSE_TK, :]), v, *view)
    outs = []
    for (qs, k_ref, _), (mr, lr, ar) in zip(streams, views):
        _online_step(_dot_nt(qs, k_ref[SEQ:SL, :]), vc, mr, lr, ar)
        outs.append(ar[...] * (1.0 / jnp.sum(lr[...], axis=-1, keepdims=True)))
    return outs


def _diff_kernel(lambda_init, lam_ref, subln_ref, q_ref, k_ref, v_ref, o_ref, m_ref, l_ref, acc_ref):
    lo = _lane_iota((1, LANES)) < 64
    lam = lam_ref[...]
    lam_full = (jnp.exp(jnp.sum(lam[0:1] * lam[1:2], axis=-1, keepdims=True))
                - jnp.exp(jnp.sum(lam[2:3] * lam[3:4], axis=-1, keepdims=True)) + lambda_init)
    vc = v_ref[SEQ:SL, :]
    gain = subln_ref[...] * (1.0 - lambda_init)

    def q_body(i, carry):
        q0 = pl.multiple_of(i * DENSE_TQ, DENSE_TQ)
        qa, qb = _split_pair(q_ref[pl.ds(q0, DENSE_TQ), :], lo)
        qs = jnp.concatenate([qa, qb], axis=0)
        o, = _dense_streams([(qs, k_ref, (m_ref, l_ref, acc_ref))], v_ref, SEQ, vc)
        od = o[:DENSE_TQ] - lam_full * o[DENSE_TQ:]
        od = od * lax.rsqrt(jnp.mean(od * od, axis=-1, keepdims=True) + EPS) * gain
        o_ref[pl.ds(q0, DENSE_TQ), :] = od.astype(BF16)
        return carry

    lax.fori_loop(0, SEQ // DENSE_TQ, q_body, 0)
    o_ref[SEQ:SL, :] = jnp.zeros((CTX, LANES), BF16)


def _diff_attention(q, k, v, lam, subln, lambda_init, b):
    slab = lambda: pl.BlockSpec((None, SL, LANES), lambda i, j: (i, 0, j))
    rows = 2 * DENSE_TQ
    return pl.pallas_call(
        functools.partial(_diff_kernel, lambda_init),
        grid=(b, DIFF_HEADS),
        in_specs=[pl.BlockSpec(lam.shape, lambda i, j: (0, 0)),
                  pl.BlockSpec((1, LANES), lambda i, j: (0, 0)),
                  slab(), slab(), slab()],
        out_specs=slab(),
        out_shape=jax.ShapeDtypeStruct(v.shape, BF16),
        scratch_shapes=[pltpu.VMEM((rows, LANES), F32)] * 3,
        compiler_params=_cparams("parallel", "parallel"),
        name="diff_attention",
    )(lam, subln.reshape(1, LANES), q, k, v)


def _mla_kernel(q_ref, k_ref, v_ref, o_ref, *scratch):
    lo = _lane_iota((1, LANES)) < 64
    vc = v_ref[SEQ:SL, :]

    def two_heads(q, n_lat):
        streams = []
        for hd in range(2):
            sl = slice(hd * LANES, (hd + 1) * LANES)
            streams.append((q[:, sl], k_ref.at[:, sl], scratch[3 * hd:3 * hd + 3]))
        outs = _dense_streams(streams, v_ref, n_lat, vc)
        return jnp.where(lo, outs[0], outs[1])

    def q_body(i, carry):
        q0 = pl.multiple_of(i * DENSE_TQ, DENSE_TQ)
        o = two_heads(q_ref[pl.ds(q0, DENSE_TQ), :], SEQ)
        o_ref[pl.ds(q0, DENSE_TQ), :] = o.astype(BF16)
        return carry

    lax.fori_loop(0, SEQ // DENSE_TQ, q_body, 0)
    o_ref[SEQ:SL, :] = two_heads(q_ref[SEQ:SL, :], 0).astype(BF16)


def _mla_attention(q, k, v, b):
    slabq = lambda: pl.BlockSpec((None, SL, 2 * LANES), lambda i, j: (i, 0, j))
    slabv = lambda: pl.BlockSpec((None, SL, LANES), lambda i, j: (i, 0, j))
    return pl.pallas_call(
        _mla_kernel,
        grid=(b, MLA_HEADS // 2),
        in_specs=[slabq(), slabq(), slabv()],
        out_specs=slabv(),
        out_shape=jax.ShapeDtypeStruct(v.shape, BF16),
        scratch_shapes=[pltpu.VMEM((DENSE_TQ, LANES), F32)] * 6,
        compiler_params=_cparams("parallel", "parallel"),
        name="mla_attention",
    )(q, k, v)


def _oproj_kernel(a_ref, x_ref, wo_ref, g1_ref, ng_ref, sc_ref, sh_ref, wr_hi_ref, wr_lo_ref, br_ref,
                  xo_ref, h_ref, route_ref, cnt_ref):
    y = _dot(a_ref[...], wo_ref[...])
    x = x_ref[...] + g1_ref[...] * y
    xo_ref[...] = x
    h = _modulated(x, ng_ref[...], sc_ref[...], sh_ref[...])
    hi = h.astype(BF16)
    h_ref[...] = hi
    lo = (h - hi.astype(F32)).astype(BF16)
    z = _dot(hi, wr_hi_ref[...]) + _dot(lo, wr_hi_ref[...]) + _dot(hi, wr_lo_ref[...]) + br_ref[...]
    lane = _lane_iota(z.shape)
    lane_f = lane.astype(F32)
    big = jnp.float32(1e9)
    is_g = lane < N_GROUPS
    zg = jnp.where(is_g, z, NEG)
    gmax = jnp.max(zg, axis=-1, keepdims=True)
    gsum = jnp.sum(jnp.where(is_g, jnp.exp(zg - gmax), 0.0), axis=-1, keepdims=True)
    g_w = 1.0 / gsum
    g_idx = jnp.min(jnp.where(is_g & (zg == gmax), lane_f, big), axis=-1, keepdims=True)
    e_lo = N_GROUPS + EXPERTS_PER_GROUP * g_idx
    is_e = (lane_f >= e_lo) & (lane_f < e_lo + EXPERTS_PER_GROUP)
    ze = jnp.where(is_e, z, NEG)
    z1 = jnp.max(ze, axis=-1, keepdims=True)
    i1 = jnp.min(jnp.where(is_e & (ze == z1), lane_f, big), axis=-1, keepdims=True)
    ze2 = jnp.where(lane_f == i1, NEG, ze)
    z2 = jnp.max(ze2, axis=-1, keepdims=True)
    i2 = jnp.min(jnp.where(is_e & (ze2 == z2) & (lane_f != i1), lane_f, big), axis=-1, keepdims=True)
    r = jnp.exp(z2 - z1)
    gate1 = g_w / (1.0 + r)
    gate2 = g_w * r / (1.0 + r)
    e1 = i1 - N_GROUPS
    e2 = i2 - N_GROUPS
    route = jnp.where(lane == 0, gate1, jnp.where(lane == 1, gate2,
                      jnp.where(lane == 2, e1, jnp.where(lane == 3, e2, 0.0))))
    route_ref[...] = route

    @pl.when(pl.program_id(0) == 0)
    def _():
        cnt_ref[...] = jnp.zeros_like(cnt_ref)

    hits = jnp.where(lane_f == e1, 1.0, 0.0) + jnp.where(lane_f == e2, 1.0, 0.0)
    cnt_ref[...] += jnp.sum(hits, axis=0, keepdims=True)


def _route_pos_kernel(route_ref, base_ref, pos_ref, carry_ref):
    @pl.when(pl.program_id(0) == 0)
    def _():
        carry_ref[...] = base_ref[...]

    lane = _lane_iota((TM, LANES))
    lane_f = lane.astype(F32)
    r = lax.broadcasted_iota(jnp.int32, (TM, TM), 0)
    c = lax.broadcasted_iota(jnp.int32, (TM, TM), 1)
    tri = jnp.where(c <= r, 1.0, 0.0).astype(BF16)
    carry = carry_ref[...]
    for s in range(TILES_PER_SLAB):
        route = route_ref[s * TM:(s + 1) * TM, :]
        oh0 = jnp.where(lane_f == route[:, 2:3], 1.0, 0.0)
        oh1 = jnp.where(lane_f == route[:, 3:4], 1.0, 0.0)
        cnt0 = jnp.sum(oh0, axis=0, keepdims=True)
        pre0 = _dot(tri, oh0.astype(BF16)) - 1.0 + carry
        pre1 = _dot(tri, oh1.astype(BF16)) - 1.0 + carry + cnt0
        pos0 = jnp.sum(oh0 * pre0, axis=-1, keepdims=True)
        pos1 = jnp.sum(oh1 * pre1, axis=-1, keepdims=True)
        pos_ref[s * TM:(s + 1) * TM, :] = jnp.where(lane == 0, pos0, jnp.where(lane == 1, pos1, 0.0))
        carry = carry + cnt0 + jnp.sum(oh1, axis=0, keepdims=True)
    carry_ref[...] = carry


def _route_pos(route, base):
    t = route.shape[0]
    return pl.pallas_call(
        _route_pos_kernel,
        grid=(t // SL,),
        in_specs=[pl.BlockSpec((SL, LANES), lambda i: (i, 0)), pl.BlockSpec((1, LANES), lambda i: (0, 0))],
        out_specs=pl.BlockSpec((SL, LANES), lambda i: (i, 0)),
        out_shape=jax.ShapeDtypeStruct((t, LANES), F32),
        scratch_shapes=[pltpu.VMEM((1, LANES), F32)],
        compiler_params=_cparams("arbitrary"),
        name="route_pos",
    )(route, base)


def _oproj_router(attn2, x2, mod3, wo_bf, ng, wr_hi, wr_lo, br):
    t, d = x2.shape
    n_in = attn2.shape[1]
    full = lambda a: pl.BlockSpec(a.shape, lambda i: (0,) * a.ndim)
    row = lambda w: pl.BlockSpec((TM, w), lambda i: (i, 0))
    return pl.pallas_call(
        _oproj_kernel,
        grid=(t // TM,),
        in_specs=[row(n_in), row(d), full(wo_bf), _mod_spec(2), pl.BlockSpec((1, d), lambda i: (0, 0)),
                  _mod_spec(4), _mod_spec(3), full(wr_hi), full(wr_lo), full(br)],
        out_specs=[row(d), row(d), row(LANES), pl.BlockSpec((1, LANES), lambda i: (0, 0))],
        out_shape=[jax.ShapeDtypeStruct((t, d), F32), jax.ShapeDtypeStruct((t, d), BF16),
                   jax.ShapeDtypeStruct((t, LANES), F32), jax.ShapeDtypeStruct((1, LANES), F32)],
        compiler_params=_cparams("arbitrary"),
        name="oproj_router",
    )(attn2, x2, wo_bf, mod3, ng.reshape(1, d), mod3, mod3, wr_hi, wr_lo, br)


def _expert_kernel(be_ref, nu_ref, x_ref, wgu_ref, wd_ref, o_ref, wgu_bf, wd_bf):
    i = pl.program_id(0)
    new_expert = (i == 0) | (be_ref[i] != be_ref[jnp.maximum(i - 1, 0)])

    @pl.when(new_expert)
    def _():
        wgu_bf[...] = wgu_ref[...].astype(BF16)
        wd_bf[...] = wd_ref[...].astype(BF16)

    @pl.when(i < nu_ref[0])
    def _():
        gu = _dot(x_ref[...], wgu_bf[...])
        gte, up = gu[:, :D_EXPERT], gu[:, D_EXPERT:]
        a = (gte * jax.nn.sigmoid(gte) * up).astype(BF16)
        o_ref[...] = _dot(a, wd_bf[...]).astype(BF16)

    @pl.when(i >= nu_ref[0])
    def _():
        o_ref[...] = jnp.zeros_like(o_ref)


def _experts(xs, block_expert, n_used, wgu, wd, layer):
    n_rows, d = xs.shape
    n_blocks = n_rows // MOE_BLK
    return pl.pallas_call(
        _expert_kernel,
        grid_spec=pltpu.PrefetchScalarGridSpec(
            num_scalar_prefetch=2,
            grid=(n_blocks,),
            in_specs=[pl.BlockSpec((MOE_BLK, d), lambda i, be, nu: (i, 0)),
                      pl.BlockSpec((None, None, d, 2 * D_EXPERT), lambda i, be, nu: (layer, be[i], 0, 0)),
                      pl.BlockSpec((None, None, D_EXPERT, d), lambda i, be, nu: (layer, be[i], 0, 0))],
            out_specs=pl.BlockSpec((MOE_BLK, d), lambda i, be, nu: (i, 0)),
            scratch_shapes=[pltpu.VMEM((d, 2 * D_EXPERT), BF16), pltpu.VMEM((D_EXPERT, d), BF16)]),
        out_shape=jax.ShapeDtypeStruct((n_rows, d), BF16),
        compiler_params=_cparams("arbitrary"),
        name="experts",
    )(block_expert, n_used, xs, wgu, wd)


def _combine_kernel(x_ref, y0_ref, y1_ref, route_ref, g2_ref, o_ref):
    o_ref[...] = _residual_with_moe(x_ref, y0_ref, y1_ref, route_ref, g2_ref)


def _combine(x2, y0, y1, route, mod3):
    t, d = x2.shape
    row = lambda w: pl.BlockSpec((TM, w), lambda i: (i, 0))
    return pl.pallas_call(
        _combine_kernel,
        grid=(t // TM,),
        in_specs=[row(d), row(d), row(d), row(LANES), _mod_spec(5)],
        out_specs=row(d),
        out_shape=jax.ShapeDtypeStruct((t, d), F32),
        compiler_params=_cparams("parallel"),
        name="moe_combine",
    )(x2, y0, y1, route, mod3)


def _moe_dispatch(route, counts_f, t):
    counts = counts_f[0, :N_EXPERTS].astype(jnp.int32)
    padded = (counts + MOE_BLK - 1) // MOE_BLK * MOE_BLK
    seg_end = jnp.cumsum(padded)
    seg_start = seg_end - padded
    base = jnp.zeros((1, LANES), F32).at[0, :N_EXPERTS].set(seg_start.astype(F32))
    pos_f = _route_pos(route, base)
    pos = pos_f[:, :2].T.astype(jnp.int32)
    n_rows = -(-(2 * t + N_EXPERTS * (MOE_BLK - 1)) // MOE_BLK) * MOE_BLK
    n_blocks = n_rows // MOE_BLK
    token = jnp.tile(jnp.arange(t, dtype=jnp.int32), 2)
    row_token = (jnp.arange(n_rows, dtype=jnp.int32) % t).at[pos.reshape(-1)].set(
        token, mode="promise_in_bounds", unique_indices=True)
    block_start = jnp.arange(n_blocks, dtype=jnp.int32) * MOE_BLK
    block_expert = jnp.minimum(jnp.sum((seg_end[None, :] <= block_start[:, None]).astype(jnp.int32), axis=1),
                               N_EXPERTS - 1)
    n_used = (seg_end[-1] // MOE_BLK).astype(jnp.int32).reshape(1)
    return row_token, pos, block_expert, n_used


def _take_rows(a, idx):
    return a.at[idx].get(mode="promise_in_bounds")


def _moe_rows(h_bf, route, counts_f, wgu, wd, layer):
    t = h_bf.shape[0]
    row_token, pos, block_expert, n_used = _moe_dispatch(route, counts_f, t)
    xs = _take_rows(h_bf, row_token)
    ys = _experts(xs, block_expert, n_used, wgu, wd, layer)
    return _take_rows(ys, pos[0]), _take_rows(ys, pos[1])


def kernel(x, c, ctx, c_ctx, ada_w, ada_b, norm_g, moe_w_router_group, moe_b_router_group, moe_w_router_expert, moe_b_router_expert, moe_w_gate_up, moe_w_down, na_w_qkv, na_w_o, na_q_norm, na_k_norm, na_rpb, sw_w_qkv, sw_w_o, sw_q_norm, sw_k_norm, sw_sink, mla_w_dqkv, mla_q_a_norm, mla_kv_a_norm, mla_w_uq, mla_w_ukv, mla_q_norm, mla_k_norm, mla_w_o, diff_w_qkv, diff_q_norm, diff_k_norm, diff_lambda, diff_subln, diff_w_o):
    slab = _stack(x, c, ctx, c_ctx, ada_w, ada_b, norm_g, moe_w_router_group, moe_b_router_group, moe_w_router_expert, moe_b_router_expert, moe_w_gate_up, moe_w_down, na_w_qkv, na_w_o, na_q_norm, na_k_norm, na_rpb, sw_w_qkv, sw_w_o, sw_q_norm, sw_k_norm, sw_sink, mla_w_dqkv, mla_q_a_norm, mla_kv_a_norm, mla_w_uq, mla_w_ukv, mla_q_norm, mla_k_norm, mla_w_o, diff_w_qkv, diff_q_norm, diff_k_norm, diff_lambda, diff_subln, diff_w_o)
    return slab[:, :SEQ]


def _stack(x, c, ctx, c_ctx, ada_w, ada_b, norm_g, moe_w_router_group, moe_b_router_group, moe_w_router_expert, moe_b_router_expert, moe_w_gate_up, moe_w_down, na_w_qkv, na_w_o, na_q_norm, na_k_norm, na_rpb, sw_w_qkv, sw_w_o, sw_q_norm, sw_k_norm, sw_sink, mla_w_dqkv, mla_q_a_norm, mla_kv_a_norm, mla_w_uq, mla_w_ukv, mla_q_norm, mla_k_norm, mla_w_o, diff_w_qkv, diff_q_norm, diff_k_norm, diff_lambda, diff_subln, diff_w_o):
    b, s, d = x.shape
    assert (s, d) == (SEQ, D_MODEL) and ctx.shape == (b, CTX, d) and b <= 8
    depth = ada_w.shape[0]
    assert depth <= 4
    t = b * SL

    cc = jnp.zeros((16, d), F32).at[:b].set(c).at[8].set(c_ctx)
    mod = _ada_table(cc, ada_w, ada_b)
    x2 = jnp.concatenate([x, ctx], axis=1).reshape(t, d)

    lane = np.arange(LANES)
    bd64 = _block_diag(lane // 64)
    lm64 = lane % 64
    rope64 = _rope_tables(64, lm64)
    scale64 = 64 ** -0.5 * LOG2E

    r3 = lambda a: a.reshape(b, SL, a.shape[-1])
    for l in range(depth):
        mod3 = mod[l].reshape(16 * 6, 1, d)
        if l == 0:
            w = na_w_qkv[0].astype(BF16)
            q, k, v = _project(x2, mod3, norm_g[l, 0], w, _qkv_plan(1024, 1024, 1024), (1024, 1024, 1024),
                               bd64, _head64_vec(na_q_norm[0], na_k_norm[0], scale64), name="na_proj")
            attn = _na_attention(r3(q), r3(k), r3(v), _na_bias_table(na_rpb[0]), b)
            w_o = na_w_o[0]
        elif l == 1:
            w = sw_w_qkv[0].astype(BF16)
            q, k, v, x2 = _project(x2, mod3, norm_g[l, 0], w, _qkv_plan(1024, 256, 256), (1024, 256, 256),
                                   bd64, _head64_vec(sw_q_norm[0], sw_k_norm[0], scale64),
                                   rope=rope64, rope_half=32, pending=pending, name="sw_proj")
            attn = _sw_attention(r3(q), r3(k), r3(v), sw_sink[0] * LOG2E, b)
            w_o = sw_w_o[0]
        elif l == 2:
            p = dict(w_dqkv=mla_w_dqkv[0], q_a_norm=mla_q_a_norm[0], kv_a_norm=mla_kv_a_norm[0],
                     w_uq=mla_w_uq[0], w_ukv=mla_w_ukv[0], q_norm=mla_q_norm[0], k_norm=mla_k_norm[0])
            q, k, v, x2 = _mla_project(x2, mod3, norm_g[l, 0], p, pending)
            attn = _mla_attention(r3(q), r3(k), r3(v), b)
            w_o = mla_w_o[0]
        else:
            w = diff_w_qkv[0].astype(BF16)
            q, k, v, x2 = _project(x2, mod3, norm_g[l, 0], w, _qkv_plan(1024, 1024, 1024), (1024, 1024, 1024),
                                   bd64, _head64_vec(diff_q_norm[0], diff_k_norm[0], scale64),
                                   rope=rope64, rope_half=32, pending=pending, name="diff_proj")
            lambda_init = 0.8 - 0.6 * math.exp(-0.3 * l)
            attn = _diff_attention(r3(q), r3(k), r3(v), diff_lambda[0], diff_subln[0], lambda_init, b)
            w_o = diff_w_o[0]

        wr = jnp.concatenate([moe_w_router_group[l], moe_w_router_expert[l]], axis=1)
        wr = jnp.pad(wr, ((0, 0), (0, LANES - wr.shape[1])))
        wr_hi = wr.astype(BF16)
        wr_lo = (wr - wr_hi.astype(F32)).astype(BF16)
        br = jnp.concatenate([moe_b_router_group[l], moe_b_router_expert[l]])
        br = jnp.pad(br, (0, LANES - br.shape[0])).reshape(1, LANES)
        x2, h_bf, route, counts_f = _oproj_router(attn.reshape(t, -1), x2, mod3, w_o.astype(BF16), norm_g[l, 1],
                                                  wr_hi, wr_lo, br)
        y0, y1 = _moe_rows(h_bf, route, counts_f, moe_w_gate_up, moe_w_down, l)
        pending = (y0, y1, route, mod3)

    return _combine(x2, *pending).reshape(b, SL, d)
```

```python
import functools
import math

import jax
import jax.numpy as jnp
import numpy as np
from jax import lax
from jax.experimental import pallas as pl
from jax.experimental.pallas import tpu as pltpu

F32 = jnp.float32
BF16 = jnp.bfloat16

D_MODEL = 1024
SEQ = 4096
CTX = 256
SL = SEQ + CTX
GRID_W = 64
GRID_ROWS = SEQ // GRID_W
EPS = 1e-6
ROPE_BASE = 10000.0
NEG = -1e30
LOG2E = 1.4426950408889634

NA_KH, NA_KW = 8, 16
SW_WINDOW = 128
SW_HEADS, SW_KV_HEADS = 16, 4
MLA_Q_RANK, MLA_KV_RANK, MLA_NOPE, MLA_ROPE, MLA_V, MLA_HEADS = 384, 256, 64, 32, 64, 16
DIFF_HEADS = 8
N_GROUPS, EXPERTS_PER_GROUP, N_EXPERTS, D_EXPERT = 4, 8, 32, 256

LANES = 128
MXU_N = 256
TM = 256
TILES_PER_SLAB = SL // TM
MOE_BLK = 256
VMEM_LIMIT = 56 * 1024 * 1024


def _cparams(*sem):
    return pltpu.CompilerParams(dimension_semantics=sem, vmem_limit_bytes=VMEM_LIMIT)


def _lane_iota(shape):
    return lax.broadcasted_iota(jnp.int32, shape, len(shape) - 1)


def _dot(a, b):
    return jnp.dot(a, b, preferred_element_type=F32)


def _dot_nt(a, b):
    return lax.dot_general(a, b, (((1,), (1,)), ((), ())), preferred_element_type=F32)


def _ada_kernel(cc_ref, w_ref, b_ref, o_ref):
    a = cc_ref[...]
    a = a * jax.nn.sigmoid(a)
    hi = a.astype(BF16)
    lo = (a - hi.astype(F32)).astype(BF16)
    w = w_ref[...]
    whi = w.astype(BF16)
    wlo = (w - whi.astype(F32)).astype(BF16)
    o_ref[...] = _dot(hi, whi) + _dot(lo, whi) + _dot(hi, wlo) + b_ref[...]


def _ada_table(cc, ada_w, ada_b):
    depth, d, n = ada_w.shape
    tn = 1024
    return pl.pallas_call(
        _ada_kernel,
        grid=(depth, n // tn),
        in_specs=[pl.BlockSpec((16, d), lambda l, j: (0, 0)),
                  pl.BlockSpec((None, d, tn), lambda l, j: (l, 0, j)),
                  pl.BlockSpec((None, 1, tn), lambda l, j: (l, 0, j))],
        out_specs=pl.BlockSpec((None, 16, tn), lambda l, j: (l, 0, j)),
        out_shape=jax.ShapeDtypeStruct((depth, 16, n), F32),
        compiler_params=_cparams("parallel", "parallel"),
        name="ada_table",
    )(cc, ada_w, ada_b.reshape(depth, 1, n))


def _mod_spec(part):
    def index(i):
        m = jnp.where(i % TILES_PER_SLAB == TILES_PER_SLAB - 1, 8, i // TILES_PER_SLAB)
        return (m * 6 + part, 0, 0)
    return pl.BlockSpec((None, 1, D_MODEL), index)


def _modulated(x, g, sc, sh):
    ms = jnp.mean(x * x, axis=-1, keepdims=True)
    return (x * lax.rsqrt(ms + EPS)) * g * (1.0 + sc) + sh


def _group_norm_rope(y, bd, invcnt, gain, cos=None, sin=None, half=None):
    ss = _dot((y * y).astype(BF16), bd)
    yn = y * lax.rsqrt(ss * invcnt + EPS) * gain
    if cos is not None:
        lane = _lane_iota(yn.shape)
        fwd = pltpu.roll(yn, LANES - half, 1)
        bwd = pltpu.roll(yn, half, 1)
        partner = jnp.where((lane & (2 * half - 1)) < half, fwd, bwd)
        yn = yn * cos + partner * sin
    return yn


def _block_diag(group_of_lane):
    g = np.asarray(group_of_lane)
    return jnp.asarray((g[:, None] == g[None, :]).astype(np.float32), dtype=BF16)


def _rope_tables(rot_dim, lane_map):
    t = jnp.arange(SEQ, dtype=jnp.int32)
    n_freq = rot_dim // 4
    inv_freq = ROPE_BASE ** (-jnp.arange(n_freq, dtype=F32) / n_freq)
    rows = (t // GRID_W).astype(F32)
    cols = (t % GRID_W).astype(F32)
    ang = jnp.concatenate([rows[:, None] * inv_freq, cols[:, None] * inv_freq], axis=-1)
    cos, sin = jnp.cos(ang), jnp.sin(ang)
    lm = np.asarray(lane_map)
    rot = lm >= 0
    idx = np.where(rot, lm % (rot_dim // 2), 0)
    sign = np.where(lm < rot_dim // 2, -1.0, 1.0).astype(np.float32)
    cos_t = jnp.where(rot[None, :], cos[:, idx], 1.0)
    sin_t = jnp.where(rot[None, :], sin[:, idx] * sign[None, :], 0.0)
    ident_c = jnp.ones((CTX, LANES), F32)
    ident_s = jnp.zeros((CTX, LANES), F32)
    return jnp.concatenate([cos_t, ident_c], 0), jnp.concatenate([sin_t, ident_s], 0)


def _residual_with_moe(x_ref, y0_ref, y1_ref, route_ref, g2_ref):
    route = route_ref[...]
    y = route[:, 0:1] * y0_ref[...].astype(F32) + route[:, 1:2] * y1_ref[...].astype(F32)
    return x_ref[...] + g2_ref[...] * y


def _proj_kernel(plan, rope_half, has_pending, x_ref, g_ref, sc_ref, sh_ref, w_ref, bd_ref, vec_ref, *rest):
    if has_pending:
        x = _residual_with_moe(x_ref, *rest[:4])
        rest = rest[4:]
        rest[-1][...] = x
        rest = rest[:-1]
    else:
        x = x_ref[...]
    if rope_half is not None:
        cos_ref, sin_ref = rest[:2]
        outs = rest[2:]
        cos, sin = cos_ref[...], sin_ref[...]
    else:
        outs = rest
        cos = sin = None
    h = _modulated(x, g_ref[...], sc_ref[...], sh_ref[...]).astype(BF16)
    bd = bd_ref[...]
    for c0, ep, out_idx, oc0 in plan:
        acc = _dot(h, w_ref[:, c0:c0 + MXU_N])
        for s in range(MXU_N // LANES):
            y = acc[:, s * LANES:(s + 1) * LANES]
            if ep is not None:
                y = _group_norm_rope(y, bd, vec_ref[ep, 0:1, :], vec_ref[ep, 1:2, :], cos, sin, rope_half)
            lo = oc0 + s * LANES
            outs[out_idx][:, lo:lo + LANES] = y.astype(BF16)


def _pending_specs(pending, d):
    row = lambda w: pl.BlockSpec((TM, w), lambda i: (i, 0))
    return [row(d), row(d), row(LANES), _mod_spec(5)], list(pending)


def _project(x2, mod3, g, w_bf, plan, out_widths, bd, vec, rope=None, rope_half=None, pending=None, name="proj"):
    t, d = x2.shape
    n = w_bf.shape[1]
    n_tiles = t // TM
    in_specs = [pl.BlockSpec((TM, d), lambda i: (i, 0)),
                pl.BlockSpec((1, d), lambda i: (0, 0)),
                _mod_spec(1), _mod_spec(0),
                pl.BlockSpec((d, n), lambda i: (0, 0)),
                pl.BlockSpec(bd.shape, lambda i: (0, 0)),
                pl.BlockSpec(vec.shape, lambda i: (0, 0, 0))]
    args = [x2, g.reshape(1, d), mod3, mod3, w_bf, bd, vec]
    out_specs = [pl.BlockSpec((TM, w), lambda i: (i, 0)) for w in out_widths]
    out_shape = [jax.ShapeDtypeStruct((t, w), BF16) for w in out_widths]
    if pending is not None:
        specs, pargs = _pending_specs(pending, d)
        in_specs += specs
        args += pargs
        out_specs.append(pl.BlockSpec((TM, d), lambda i: (i, 0)))
        out_shape.append(jax.ShapeDtypeStruct((t, d), F32))
    if rope is not None:
        in_specs += [pl.BlockSpec((TM, LANES), lambda i: (i % TILES_PER_SLAB, 0))] * 2
        args += list(rope)
    return pl.pallas_call(
        functools.partial(_proj_kernel, plan, rope_half, pending is not None),
        grid=(n_tiles,),
        in_specs=in_specs,
        out_specs=out_specs,
        out_shape=out_shape,
        compiler_params=_cparams("parallel"),
        name=name,
    )(*args)


def _qkv_plan(nq, nk, nv):
    plan = []
    for c0 in range(0, nq, MXU_N):
        plan.append((c0, 0, 0, c0))
    for c0 in range(0, nk, MXU_N):
        plan.append((nq + c0, 1, 1, c0))
    for c0 in range(0, nv, MXU_N):
        plan.append((nq + nk + c0, None, 2, c0))
    return tuple(plan)


def _head64_vec(q_norm, k_norm, q_scale):
    inv = jnp.full((LANES,), 1.0 / 64, F32)
    z = jnp.zeros((6, LANES), F32)
    vq = jnp.concatenate([inv[None], (jnp.tile(q_norm, 2) * q_scale)[None], z], 0)
    vk = jnp.concatenate([inv[None], jnp.tile(k_norm, 2)[None], z], 0)
    return jnp.stack([vq, vk], 0)


def _mla_proj_kernel(x_ref, g_ref, sc_ref, sh_ref, wd_ref, wuq_ref, wuk_ref, wuv_ref, bd_ref, vec_ref,
                     an_ref, cos_ref, sin_ref, y0_ref, y1_ref, route_ref, g2_ref, q_out, k_out, v_out, x_out):
    x = _residual_with_moe(x_ref, y0_ref, y1_ref, route_ref, g2_ref)
    x_out[...] = x
    h = _modulated(x, g_ref[...], sc_ref[...], sh_ref[...]).astype(BF16)
    cos, sin = cos_ref[...], sin_ref[...]
    half = MLA_ROPE // 2
    cq = _dot(h, wd_ref[:, 0:MLA_Q_RANK])
    ckv = _dot(h, wd_ref[:, MLA_Q_RANK:MLA_Q_RANK + MLA_KV_RANK])
    kr = _dot(h, wd_ref[:, MLA_Q_RANK + MLA_KV_RANK:])
    cq = cq * lax.rsqrt(jnp.mean(cq * cq, axis=-1, keepdims=True) + EPS) * an_ref[0:1, 0:MLA_Q_RANK]
    ckv = ckv * lax.rsqrt(jnp.mean(ckv * ckv, axis=-1, keepdims=True) + EPS) * an_ref[1:2, 0:MLA_KV_RANK]
    cq = cq.astype(BF16)
    ckv = ckv.astype(BF16)
    kr = _group_norm_rope(kr, bd_ref[1], vec_ref[2, 0:1, :], vec_ref[2, 1:2, :], cos[:, LANES:], sin[:, LANES:], half)
    kr = pltpu.roll(kr, MLA_NOPE, 1)
    for hd in range(MLA_HEADS):
        c0 = hd * LANES
        qh = _dot(cq, wuq_ref[:, c0:c0 + LANES])
        qh = _group_norm_rope(qh, bd_ref[0], vec_ref[0, 0:1, :], vec_ref[0, 1:2, :], cos[:, :LANES], sin[:, :LANES], half)
        q_out[:, c0:c0 + LANES] = qh.astype(BF16)
        kh = _dot(ckv, wuk_ref[:, c0:c0 + LANES])
        kh = _group_norm_rope(kh, bd_ref[0], vec_ref[1, 0:1, :], vec_ref[1, 1:2, :]) + kr
        k_out[:, c0:c0 + LANES] = kh.astype(BF16)
    for c0 in range(0, MLA_HEADS * MLA_V, MXU_N):
        v_out[:, c0:c0 + MXU_N] = _dot(ckv, wuv_ref[:, c0:c0 + MXU_N]).astype(BF16)


def _mla_project(x2, mod3, g, p, pending):
    t, d = x2.shape
    hq = MLA_NOPE + MLA_ROPE
    n_tiles = t // TM
    wd = jnp.pad(p["w_dqkv"], ((0, 0), (0, LANES - MLA_ROPE))).astype(BF16)
    wuq = p["w_uq"].reshape(MLA_Q_RANK, MLA_HEADS, hq)
    wuq = jnp.pad(wuq, ((0, 0), (0, 0), (0, LANES - hq))).reshape(MLA_Q_RANK, MLA_HEADS * LANES).astype(BF16)
    wukv = p["w_ukv"].reshape(MLA_KV_RANK, MLA_HEADS, MLA_NOPE + MLA_V)
    wuk = jnp.pad(wukv[:, :, :MLA_NOPE], ((0, 0), (0, 0), (0, LANES - MLA_NOPE)))
    wuk = wuk.reshape(MLA_KV_RANK, MLA_HEADS * LANES).astype(BF16)
    wuv = wukv[:, :, MLA_NOPE:].reshape(MLA_KV_RANK, MLA_HEADS * MLA_V).astype(BF16)
    scale = (MLA_NOPE + MLA_ROPE) ** -0.5 * LOG2E
    lane = np.arange(LANES)
    grp_head = np.where(lane < 64, 0, np.where(lane < 96, 1, 2))
    grp_kr = np.where(lane < 32, 0, 1)
    bd = jnp.stack([_block_diag(grp_head), _block_diag(grp_kr)], 0)
    inv_head = jnp.asarray(np.where(lane < 64, 1 / 64, 1 / 32), F32)
    zpad = jnp.zeros((LANES - hq,), F32)
    gq = jnp.concatenate([p["q_norm"], zpad]) * scale
    gk = jnp.concatenate([p["k_norm"][:MLA_NOPE], jnp.zeros((LANES - MLA_NOPE,), F32)])
    gkr = jnp.concatenate([p["k_norm"][MLA_NOPE:], jnp.zeros((LANES - MLA_ROPE,), F32)])
    inv_kr = jnp.asarray(np.where(lane < 32, 1 / 32, 1 / 96), F32)
    z6 = jnp.zeros((6, LANES), F32)
    vec = jnp.stack([jnp.concatenate([inv_head[None], gq[None], z6], 0),
                     jnp.concatenate([inv_head[None], gk[None], z6], 0),
                     jnp.concatenate([inv_kr[None], gkr[None], z6], 0)], 0)
    an = jnp.zeros((8, MLA_Q_RANK), F32)
    an = an.at[0].set(p["q_a_norm"]).at[1, :MLA_KV_RANK].set(p["kv_a_norm"])
    lm_head = np.where((lane >= 64) & (lane < 96), lane - 64, -1)
    lm_kr = np.where(lane < 32, lane, -1)
    ch, sh_ = _rope_tables(MLA_ROPE, lm_head)
    ck, sk = _rope_tables(MLA_ROPE, lm_kr)
    cos2 = jnp.concatenate([ch, ck], 1)
    sin2 = jnp.concatenate([sh_, sk], 1)

    full = lambda a: pl.BlockSpec(a.shape, lambda i: (0,) * a.ndim)
    rope_spec = pl.BlockSpec((TM, 2 * LANES), lambda i: (i % TILES_PER_SLAB, 0))
    out_w = (MLA_HEADS * LANES, MLA_HEADS * LANES, MLA_HEADS * MLA_V)
    pend_specs, pend_args = _pending_specs(pending, d)
    return pl.pallas_call(
        _mla_proj_kernel,
        grid=(n_tiles,),
        in_specs=[pl.BlockSpec((TM, d), lambda i: (i, 0)), pl.BlockSpec((1, d), lambda i: (0, 0)),
                  _mod_spec(1), _mod_spec(0), full(wd), full(wuq), full(wuk), full(wuv), full(bd), full(vec),
                  full(an), rope_spec, rope_spec] + pend_specs,
        out_specs=[pl.BlockSpec((TM, w), lambda i: (i, 0)) for w in out_w + (d,)],
        out_shape=[jax.ShapeDtypeStruct((t, w), BF16) for w in out_w] + [jax.ShapeDtypeStruct((t, d), F32)],
        compiler_params=_cparams("parallel"),
        name="mla_proj",
    )(x2, g.reshape(1, d), mod3, mod3, wd, wuq, wuk, wuv, bd, vec, an, cos2, sin2, *pend_args)


def _softmax_pv(s_list, v_list, sink=None):
    m = functools.reduce(jnp.maximum, [jnp.max(s, axis=-1, keepdims=True) for s in s_list])
    if sink is not None:
        m = jnp.maximum(m, sink)
    ps = [jnp.exp2(s - m) for s in s_list]
    l = functools.reduce(lambda a, b: a + b, [jnp.sum(p, axis=-1, keepdims=True) for p in ps])
    if sink is not None:
        l = l + jnp.exp2(sink - m)
    o = functools.reduce(lambda a, b: a + b, [_dot(p.astype(BF16), v) for p, v in zip(ps, v_list)])
    return o * (1.0 / l)


def _split_pair(x, lo):
    zero = jnp.zeros_like(x)
    return jnp.where(lo, x, zero), jnp.where(lo, zero, x)


def _na_kernel(q_ref, k_ref, v_ref, bias_ref, o_ref):
    lo = _lane_iota((1, LANES)) < 64
    kc = k_ref[SEQ:SL, :]
    vc = v_ref[SEQ:SL, :]
    nb = NA_KH * GRID_W

    def pair_attend(q, ks, vs, bias):
        n = q.shape[0]
        qa, qb = _split_pair(q, lo)
        qs = jnp.concatenate([qa, qb], axis=0)
        s_list = [_dot_nt(qs, k) for k in ks]
        if bias is not None:
            s_list[0] = s_list[0] + bias
        o = _softmax_pv(s_list, vs)
        return jnp.where(lo, o[:n], o[n:])

    def row_body(r, carry):
        row0 = jnp.clip(r - NA_KH // 2, 0, GRID_ROWS - NA_KH)
        q = q_ref[pl.ds(pl.multiple_of(r * GRID_W, GRID_W), GRID_W), :]
        ks = pl.multiple_of(row0 * GRID_W, GRID_W)
        kn = k_ref[pl.ds(ks, nb), :]
        vn = v_ref[pl.ds(ks, nb), :]
        o = pair_attend(q, [kn, kc], [vn, vc], bias_ref[r - row0])
        o_ref[pl.ds(pl.multiple_of(r * GRID_W, GRID_W), GRID_W), :] = o.astype(BF16)
        return carry

    lax.fori_loop(0, GRID_ROWS, row_body, 0, unroll=8)
    o_ref[SEQ:SL, :] = pair_attend(q_ref[SEQ:SL, :], [kc], [vc], None).astype(BF16)


def _na_bias_table(rpb):
    heads = rpb.shape[0]
    n_col = 2 * NA_KW - 1
    r2 = rpb.reshape(heads, 2 * NA_KH - 1, n_col) * LOG2E
    padded = jnp.pad(r2, ((0, 0), (0, 0), (GRID_W, GRID_W)))
    base = GRID_W + NA_KW - 1
    toeplitz = jnp.stack([padded[:, :, base - qc:base - qc + GRID_W] for qc in range(GRID_W)], axis=2)
    qc = np.arange(GRID_W)[:, None, None]
    kc = np.arange(GRID_W)[None, None, :]
    col0 = np.clip(qc - NA_KW // 2, 0, GRID_W - NA_KW)
    valid = np.broadcast_to((kc >= col0) & (kc < col0 + NA_KW), (GRID_W, NA_KH, GRID_W))
    valid = valid.reshape(GRID_W, NA_KH * GRID_W)
    classes = []
    for c in range(NA_KH):
        t = toeplitz[:, NA_KH - 1 - c:2 * NA_KH - 1 - c]
        t = t.transpose(0, 2, 1, 3).reshape(heads, GRID_W, NA_KH * GRID_W)
        classes.append(jnp.where(valid[None], t, NEG))
    tab = jnp.stack(classes, 0)
    return tab.reshape(NA_KH, heads // 2, 2 * GRID_W, NA_KH * GRID_W)


def _na_attention(q, k, v, bias, b):
    hp = q.shape[-1] // LANES
    slab = lambda: pl.BlockSpec((None, SL, LANES), lambda i, j: (i, 0, j))
    return pl.pallas_call(
        _na_kernel,
        grid=(b, hp),
        in_specs=[slab(), slab(), slab(),
                  pl.BlockSpec((NA_KH, None, 2 * GRID_W, NA_KH * GRID_W), lambda i, j: (0, j, 0, 0))],
        out_specs=slab(),
        out_shape=jax.ShapeDtypeStruct(q.shape, BF16),
        compiler_params=_cparams("parallel", "parallel"),
        name="na_attention",
    )(q, k, v, bias)


Q_BLK = 128


def _sw_kernel(sink_ref, q_ref, k_ref, v_ref, o_ref):
    g = pl.program_id(1)
    lane = _lane_iota((1, LANES))
    lo = lane < 64
    first_half = (jnp.zeros((1, LANES), jnp.int32) + (g % 2)) == 0
    target = (lane // 64) == (g % 2)
    kc = k_ref[SEQ:SL, :]
    vc = v_ref[SEQ:SL, :]
    span = Q_BLK + 2 * SW_WINDOW
    sinks = [sink_ref[4 * g + t] for t in range(4)]

    def swap(x):
        return jnp.concatenate([x[:, 64:], x[:, :64]], axis=1)

    def stack_heads(q):
        parts = []
        for blk in range(2):
            qb = q[:, blk * LANES:(blk + 1) * LANES]
            qs = swap(qb)
            even = jnp.where(first_half, qb, qs)
            odd = jnp.where(first_half, qs, qb)
            zero = jnp.zeros_like(qb)
            parts += [jnp.where(target, even, zero), jnp.where(target, odd, zero)]
        return jnp.concatenate(parts, axis=0)

    def unstack_heads(o, n):
        blocks = []
        for blk in range(2):
            oe = o[(2 * blk) * n:(2 * blk + 1) * n]
            oo = o[(2 * blk + 1) * n:(2 * blk + 2) * n]
            oe = jnp.where(first_half, oe, pltpu.roll(oe, 64, 1))
            oo = jnp.where(first_half, pltpu.roll(oo, 64, 1), oo)
            blocks.append(jnp.where(lo, oe, oo))
        return jnp.concatenate(blocks, axis=1)

    def sink_col(n):
        row = lax.broadcasted_iota(jnp.int32, (4 * n, 1), 0)
        return jnp.where(row < n, sinks[0], jnp.where(row < 2 * n, sinks[1],
                         jnp.where(row < 3 * n, sinks[2], sinks[3])))

    def block_body(i, carry):
        q0 = pl.multiple_of(i * Q_BLK, Q_BLK)
        start = pl.multiple_of(jnp.clip((i - 1) * Q_BLK, 0, SEQ - span), Q_BLK)
        qs = stack_heads(q_ref[pl.ds(q0, Q_BLK), :])
        kw = k_ref[pl.ds(start, span), :]
        vw = v_ref[pl.ds(start, span), :]
        s_loc = _dot_nt(qs, kw)
        qpos = q0 + (lax.broadcasted_iota(jnp.int32, (4 * Q_BLK, 1), 0) & (Q_BLK - 1))
        kpos = start + _lane_iota((1, span))
        s_loc = jnp.where(jnp.abs(kpos - qpos) <= SW_WINDOW, s_loc, NEG)
        s_ctx = _dot_nt(qs, kc)
        o = _softmax_pv([s_loc, s_ctx], [vw, vc], sink_col(Q_BLK))
        o_ref[pl.ds(q0, Q_BLK), :] = unstack_heads(o, Q_BLK).astype(BF16)
        return carry

    lax.fori_loop(0, SEQ // Q_BLK, block_body, 0, unroll=4)
    qs = stack_heads(q_ref[SEQ:SL, :])
    o = _softmax_pv([_dot_nt(qs, kc)], [vc], sink_col(CTX))
    o_ref[SEQ:SL, :] = unstack_heads(o, CTX).astype(BF16)


def _sw_attention(q, k, v, sink2, b):
    nq = 4 * 64
    slabq = lambda: pl.BlockSpec((None, SL, nq), lambda i, g: (i, 0, g))
    slabk = lambda: pl.BlockSpec((None, SL, LANES), lambda i, g: (i, 0, g // 2))
    return pl.pallas_call(
        _sw_kernel,
        grid=(b, SW_KV_HEADS),
        in_specs=[pl.BlockSpec(memory_space=pltpu.SMEM), slabq(), slabk(), slabk()],
        out_specs=slabq(),
        out_shape=jax.ShapeDtypeStruct(q.shape, BF16),
        compiler_params=_cparams("parallel", "parallel"),
        name="sw_attention",
    )(sink2, q, k, v)


DENSE_TQ = 512
DENSE_TK = 2048


def _online_step(s, v, m_ref, l_ref, acc_ref):
    chunks = s.shape[1] // LANES
    m_prev = m_ref[...]
    m_new = jnp.maximum(m_prev, jnp.max(s, axis=-1, keepdims=True))
    alpha = jnp.exp2(m_prev - m_new)
    p = jnp.exp2(s - jnp.concatenate([m_new] * chunks, axis=1))
    psum = functools.reduce(lambda a, b: a + b, [p[:, c * LANES:(c + 1) * LANES] for c in range(chunks)])
    l_ref[...] = alpha * l_ref[...] + psum
    acc_ref[...] = alpha * acc_ref[...] + _dot(p.astype(BF16), v)
    m_ref[...] = m_new


def _dense_streams(streams, v_ref, n_lat, vc):
    views = []
    for qs, _, (m_ref, l_ref, acc_ref) in streams:
        rows = qs.shape[0]
        m_ref[0:rows, :] = jnp.full((rows, LANES), NEG, F32)
        l_ref[0:rows, :] = jnp.zeros((rows, LANES), F32)
        acc_ref[0:rows, :] = jnp.zeros((rows, LANES), F32)
        views.append((m_ref.at[0:rows, :], l_ref.at[0:rows, :], acc_ref.at[0:rows, :]))

    for c in range(n_lat // DENSE_TK):
        v = v_ref[c * DENSE_TK:(c + 1) * DENSE_TK, :]
        for (qs, k_ref, _), view in zip(streams, views):
            _online_step(_dot_nt(qs, k_ref[c * DENSE_TK:(c + 1) * DENSE_TK, :]), v, *view)
    outs = []
    for (qs, k_ref, _), (mr, lr, ar) in zip(streams, views):
        _online_step(_dot_nt(qs, k_ref[SEQ:SL, :]), vc, mr, lr, ar)
        outs.append(ar[...] * (1.0 / jnp.sum(lr[...], axis=-1, keepdims=True)))
    return outs


def _diff_kernel(lambda_init, lam_ref, subln_ref, q_ref, k_ref, v_ref, o_ref, m_ref, l_ref, acc_ref):
    lo = _lane_iota((1, LANES)) < 64
    lam = lam_ref[...]
    lam_full = (jnp.exp(jnp.sum(lam[0:1] * lam[1:2], axis=-1, keepdims=True))
                - jnp.exp(jnp.sum(lam[2:3] * lam[3:4], axis=-1, keepdims=True)) + lambda_init)
    vc = v_ref[SEQ:SL, :]
    gain = subln_ref[...] * (1.0 - lambda_init)

    def q_body(i, carry):
        q0 = pl.multiple_of(i * DENSE_TQ, DENSE_TQ)
        qa, qb = _split_pair(q_ref[pl.ds(q0, DENSE_TQ), :], lo)
        qs = jnp.concatenate([qa, qb], axis=0)
        o, = _dense_streams([(qs, k_ref, (m_ref, l_ref, acc_ref))], v_ref, SEQ, vc)
        od = o[:DENSE_TQ] - lam_full * o[DENSE_TQ:]
        od = od * lax.rsqrt(jnp.mean(od * od, axis=-1, keepdims=True) + EPS) * gain
        o_ref[pl.ds(q0, DENSE_TQ), :] = od.astype(BF16)
        return carry

    lax.fori_loop(0, SEQ // DENSE_TQ, q_body, 0)
    o_ref[SEQ:SL, :] = jnp.zeros((CTX, LANES), BF16)


def _diff_attention(q, k, v, lam, subln, lambda_init, b):
    slab = lambda: pl.BlockSpec((None, SL, LANES), lambda i, j: (i, 0, j))
    rows = 2 * DENSE_TQ
    return pl.pallas_call(
        functools.partial(_diff_kernel, lambda_init),
        grid=(b, DIFF_HEADS),
        in_specs=[pl.BlockSpec(lam.shape, lambda i, j: (0, 0)),
                  pl.BlockSpec((1, LANES), lambda i, j: (0, 0)),
                  slab(), slab(), slab()],
        out_specs=slab(),
        out_shape=jax.ShapeDtypeStruct(v.shape, BF16),
        scratch_shapes=[pltpu.VMEM((rows, LANES), F32)] * 3,
        compiler_params=_cparams("parallel", "parallel"),
        name="diff_attention",
    )(lam, subln.reshape(1, LANES), q, k, v)


def _mla_kernel(q_ref, k_ref, v_ref, o_ref, *scratch):
    lo = _lane_iota((1, LANES)) < 64
    vc = v_ref[SEQ:SL, :]

    def two_heads(q, n_lat):
        streams = []
        for hd in range(2):
            sl = slice(hd * LANES, (hd + 1) * LANES)
            streams.append((q[:, sl], k_ref.at[:, sl], scratch[3 * hd:3 * hd + 3]))
        outs = _dense_streams(streams, v_ref, n_lat, vc)
        return jnp.where(lo, outs[0], outs[1])

    def q_body(i, carry):
        q0 = pl.multiple_of(i * DENSE_TQ, DENSE_TQ)
        o = two_heads(q_ref[pl.ds(q0, DENSE_TQ), :], SEQ)
        o_ref[pl.ds(q0, DENSE_TQ), :] = o.astype(BF16)
        return carry

    lax.fori_loop(0, SEQ // DENSE_TQ, q_body, 0)
    o_ref[SEQ:SL, :] = two_heads(q_ref[SEQ:SL, :], 0).astype(BF16)


def _mla_attention(q, k, v, b):
    slabq = lambda: pl.BlockSpec((None, SL, 2 * LANES), lambda i, j: (i, 0, j))
    slabv = lambda: pl.BlockSpec((None, SL, LANES), lambda i, j: (i, 0, j))
    return pl.pallas_call(
        _mla_kernel,
        grid=(b, MLA_HEADS // 2),
        in_specs=[slabq(), slabq(), slabv()],
        out_specs=slabv(),
        out_shape=jax.ShapeDtypeStruct(v.shape, BF16),
        scratch_shapes=[pltpu.VMEM((DENSE_TQ, LANES), F32)] * 6,
        compiler_params=_cparams("parallel", "parallel"),
        name="mla_attention",
    )(q, k, v)


def _oproj_kernel(a_ref, x_ref, wo_ref, g1_ref, ng_ref, sc_ref, sh_ref, wr_cat_ref, br_ref,
                  xo_ref, h_ref, route_ref, cnt_ref):
    y = _dot(a_ref[...], wo_ref[...])
    x = x_ref[...] + g1_ref[...] * y
    xo_ref[...] = x
    h = _modulated(x, ng_ref[...], sc_ref[...], sh_ref[...])
    hi = h.astype(BF16)
    h_ref[...] = hi
    lo = (h - hi.astype(F32)).astype(BF16)
    zz = _dot(hi, wr_cat_ref[...])
    z = zz[:, :LANES] + zz[:, LANES:] + _dot(lo, wr_cat_ref[:, :LANES]) + br_ref[...]
    lane = _lane_iota(z.shape)
    lane_f = lane.astype(F32)
    big = jnp.float32(1e9)
    is_g = lane < N_GROUPS
    zg = jnp.where(is_g, z, NEG)
    gmax = jnp.max(zg, axis=-1, keepdims=True)
    gsum = jnp.sum(jnp.where(is_g, jnp.exp(zg - gmax), 0.0), axis=-1, keepdims=True)
    g_w = 1.0 / gsum
    g_idx = jnp.min(jnp.where(is_g & (zg == gmax), lane_f, big), axis=-1, keepdims=True)
    e_lo = N_GROUPS + EXPERTS_PER_GROUP * g_idx
    is_e = (lane_f >= e_lo) & (lane_f < e_lo + EXPERTS_PER_GROUP)
    ze = jnp.where(is_e, z, NEG)
    z1 = jnp.max(ze, axis=-1, keepdims=True)
    i1 = jnp.min(jnp.where(is_e & (ze == z1), lane_f, big), axis=-1, keepdims=True)
    ze2 = jnp.where(lane_f == i1, NEG, ze)
    z2 = jnp.max(ze2, axis=-1, keepdims=True)
    i2 = jnp.min(jnp.where(is_e & (ze2 == z2) & (lane_f != i1), lane_f, big), axis=-1, keepdims=True)
    r = jnp.exp(z2 - z1)
    gate1 = g_w / (1.0 + r)
    gate2 = g_w * r / (1.0 + r)
    e1 = i1 - N_GROUPS
    e2 = i2 - N_GROUPS
    route = jnp.where(lane == 0, gate1, jnp.where(lane == 1, gate2,
                      jnp.where(lane == 2, e1, jnp.where(lane == 3, e2, 0.0))))
    route_ref[...] = route

    @pl.when(pl.program_id(0) == 0)
    def _():
        cnt_ref[...] = jnp.zeros_like(cnt_ref)

    hits = jnp.where(lane_f == e1, 1.0, 0.0) + jnp.where(lane_f == e2, 1.0, 0.0)
    cnt_ref[...] += jnp.sum(hits, axis=0, keepdims=True)


def _route_pos_kernel(route_ref, base_ref, pos_ref, carry_ref):
    @pl.when(pl.program_id(0) == 0)
    def _():
        carry_ref[...] = base_ref[...]

    lane = _lane_iota((TM, LANES))
    lane_f = lane.astype(F32)
    r = lax.broadcasted_iota(jnp.int32, (TM, TM), 0)
    c = lax.broadcasted_iota(jnp.int32, (TM, TM), 1)
    tri = jnp.where(c <= r, 1.0, 0.0).astype(BF16)
    carry = carry_ref[...]
    for s in range(TILES_PER_SLAB):
        route = route_ref[s * TM:(s + 1) * TM, :]
        oh0 = jnp.where(lane_f == route[:, 2:3], 1.0, 0.0)
        oh1 = jnp.where(lane_f == route[:, 3:4], 1.0, 0.0)
        cnt0 = jnp.sum(oh0, axis=0, keepdims=True)
        pre0 = _dot(tri, oh0.astype(BF16)) - 1.0 + carry
        pre1 = _dot(tri, oh1.astype(BF16)) - 1.0 + carry + cnt0
        pos0 = jnp.sum(oh0 * pre0, axis=-1, keepdims=True)
        pos1 = jnp.sum(oh1 * pre1, axis=-1, keepdims=True)
        pos_ref[s * TM:(s + 1) * TM, :] = jnp.where(lane == 0, pos0, jnp.where(lane == 1, pos1, 0.0))
        carry = carry + cnt0 + jnp.sum(oh1, axis=0, keepdims=True)
    carry_ref[...] = carry


def _route_pos(route, base):
    t = route.shape[0]
    return pl.pallas_call(
        _route_pos_kernel,
        grid=(t // SL,),
        in_specs=[pl.BlockSpec((SL, LANES), lambda i: (i, 0)), pl.BlockSpec((1, LANES), lambda i: (0, 0))],
        out_specs=pl.BlockSpec((SL, LANES), lambda i: (i, 0)),
        out_shape=jax.ShapeDtypeStruct((t, LANES), F32),
        scratch_shapes=[pltpu.VMEM((1, LANES), F32)],
        compiler_params=_cparams("arbitrary"),
        name="route_pos",
    )(route, base)


def _oproj_router(attn2, x2, mod3, wo_bf, ng, wr_cat, br):
    t, d = x2.shape
    n_in = attn2.shape[1]
    full = lambda a: pl.BlockSpec(a.shape, lambda i: (0,) * a.ndim)
    row = lambda w: pl.BlockSpec((TM, w), lambda i: (i, 0))
    return pl.pallas_call(
        _oproj_kernel,
        grid=(t // TM,),
        in_specs=[row(n_in), row(d), full(wo_bf), _mod_spec(2), pl.BlockSpec((1, d), lambda i: (0, 0)),
                  _mod_spec(4), _mod_spec(3), full(wr_cat), full(br)],
        out_specs=[row(d), row(d), row(LANES), pl.BlockSpec((1, LANES), lambda i: (0, 0))],
        out_shape=[jax.ShapeDtypeStruct((t, d), F32), jax.ShapeDtypeStruct((t, d), BF16),
                   jax.ShapeDtypeStruct((t, LANES), F32), jax.ShapeDtypeStruct((1, LANES), F32)],
        compiler_params=_cparams("arbitrary"),
        name="oproj_router",
    )(attn2, x2, wo_bf, mod3, ng.reshape(1, d), mod3, mod3, wr_cat, br)


def _expert_kernel(be_ref, nu_ref, x_ref, wgu_ref, wd_ref, o_ref, wgu_bf, wd_bf):
    i = pl.program_id(0)
    new_expert = (i == 0) | (be_ref[i] != be_ref[jnp.maximum(i - 1, 0)])

    @pl.when(new_expert)
    def _():
        wgu_bf[...] = wgu_ref[...].astype(BF16)
        wd_bf[...] = wd_ref[...].astype(BF16)

    @pl.when(i < nu_ref[0])
    def _():
        gu = _dot(x_ref[...], wgu_bf[...])
        gte, up = gu[:, :D_EXPERT], gu[:, D_EXPERT:]
        a = (gte * jax.nn.sigmoid(gte) * up).astype(BF16)
        o_ref[...] = _dot(a, wd_bf[...]).astype(BF16)

    @pl.when(i >= nu_ref[0])
    def _():
        o_ref[...] = jnp.zeros_like(o_ref)


def _experts(xs, block_expert, n_used, wgu, wd, layer):
    n_rows, d = xs.shape
    n_blocks = n_rows // MOE_BLK
    return pl.pallas_call(
        _expert_kernel,
        grid_spec=pltpu.PrefetchScalarGridSpec(
            num_scalar_prefetch=2,
            grid=(n_blocks,),
            in_specs=[pl.BlockSpec((MOE_BLK, d), lambda i, be, nu: (i, 0)),
                      pl.BlockSpec((None, None, d, 2 * D_EXPERT), lambda i, be, nu: (layer, be[i], 0, 0)),
                      pl.BlockSpec((None, None, D_EXPERT, d), lambda i, be, nu: (layer, be[i], 0, 0))],
            out_specs=pl.BlockSpec((MOE_BLK, d), lambda i, be, nu: (i, 0)),
            scratch_shapes=[pltpu.VMEM((d, 2 * D_EXPERT), BF16), pltpu.VMEM((D_EXPERT, d), BF16)]),
        out_shape=jax.ShapeDtypeStruct((n_rows, d), BF16),
        compiler_params=_cparams("arbitrary"),
        name="experts",
    )(block_expert, n_used, xs, wgu, wd)


def _combine_kernel(x_ref, y0_ref, y1_ref, route_ref, g2_ref, o_ref):
    o_ref[...] = _residual_with_moe(x_ref, y0_ref, y1_ref, route_ref, g2_ref)


def _combine(x2, y0, y1, route, mod3):
    t, d = x2.shape
    row = lambda w: pl.BlockSpec((TM, w), lambda i: (i, 0))
    return pl.pallas_call(
        _combine_kernel,
        grid=(t // TM,),
        in_specs=[row(d), row(d), row(d), row(LANES), _mod_spec(5)],
        out_specs=row(d),
        out_shape=jax.ShapeDtypeStruct((t, d), F32),
        compiler_params=_cparams("parallel"),
        name="moe_combine",
    )(x2, y0, y1, route, mod3)


def _moe_dispatch(route, counts_f, t):
    counts = counts_f[0, :N_EXPERTS].astype(jnp.int32)
    padded = (counts + MOE_BLK - 1) // MOE_BLK * MOE_BLK
    seg_end = jnp.cumsum(padded)
    seg_start = seg_end - padded
    base = jnp.zeros((1, LANES), F32).at[0, :N_EXPERTS].set(seg_start.astype(F32))
    pos_f = _route_pos(route, base)
    pos = pos_f[:, :2].T.astype(jnp.int32)
    n_rows = -(-(2 * t + N_EXPERTS * (MOE_BLK - 1)) // MOE_BLK) * MOE_BLK
    n_blocks = n_rows // MOE_BLK
    token = jnp.tile(jnp.arange(t, dtype=jnp.int32), 2)
    row_token = (jnp.arange(n_rows, dtype=jnp.int32) % t).at[pos.reshape(-1)].set(
        token, mode="promise_in_bounds", unique_indices=True)
    block_start = jnp.arange(n_blocks, dtype=jnp.int32) * MOE_BLK
    block_expert = jnp.minimum(jnp.sum((seg_end[None, :] <= block_start[:, None]).astype(jnp.int32), axis=1),
                               N_EXPERTS - 1)
    n_used = (seg_end[-1] // MOE_BLK).astype(jnp.int32).reshape(1)
    return row_token, pos, block_expert, n_used


def _take_rows(a, idx):
    return a.at[idx].get(mode="promise_in_bounds")


def _moe_rows(h_bf, route, counts_f, wgu, wd, layer):
    t = h_bf.shape[0]
    row_token, pos, block_expert, n_used = _moe_dispatch(route, counts_f, t)
    xs = _take_rows(h_bf, row_token)
    ys = _experts(xs, block_expert, n_used, wgu, wd, layer)
    return _take_rows(ys, pos[0]), _take_rows(ys, pos[1])


def kernel(x, c, ctx, c_ctx, ada_w, ada_b, norm_g, moe_w_router_group, moe_b_router_group, moe_w_router_expert, moe_b_router_expert, moe_w_gate_up, moe_w_down, na_w_qkv, na_w_o, na_q_norm, na_k_norm, na_rpb, sw_w_qkv, sw_w_o, sw_q_norm, sw_k_norm, sw_sink, mla_w_dqkv, mla_q_a_norm, mla_kv_a_norm, mla_w_uq, mla_w_ukv, mla_q_norm, mla_k_norm, mla_w_o, diff_w_qkv, diff_q_norm, diff_k_norm, diff_lambda, diff_subln, diff_w_o):
    slab = _stack(x, c, ctx, c_ctx, ada_w, ada_b, norm_g, moe_w_router_group, moe_b_router_group, moe_w_router_expert, moe_b_router_expert, moe_w_gate_up, moe_w_down, na_w_qkv, na_w_o, na_q_norm, na_k_norm, na_rpb, sw_w_qkv, sw_w_o, sw_q_norm, sw_k_norm, sw_sink, mla_w_dqkv, mla_q_a_norm, mla_kv_a_norm, mla_w_uq, mla_w_ukv, mla_q_norm, mla_k_norm, mla_w_o, diff_w_qkv, diff_q_norm, diff_k_norm, diff_lambda, diff_subln, diff_w_o)
    return slab[:, :SEQ]


def _stack(x, c, ctx, c_ctx, ada_w, ada_b, norm_g, moe_w_router_group, moe_b_router_group, moe_w_router_expert, moe_b_router_expert, moe_w_gate_up, moe_w_down, na_w_qkv, na_w_o, na_q_norm, na_k_norm, na_rpb, sw_w_qkv, sw_w_o, sw_q_norm, sw_k_norm, sw_sink, mla_w_dqkv, mla_q_a_norm, mla_kv_a_norm, mla_w_uq, mla_w_ukv, mla_q_norm, mla_k_norm, mla_w_o, diff_w_qkv, diff_q_norm, diff_k_norm, diff_lambda, diff_subln, diff_w_o):
    b, s, d = x.shape
    assert (s, d) == (SEQ, D_MODEL) and ctx.shape == (b, CTX, d) and b <= 8
    depth = ada_w.shape[0]
    assert depth <= 4
    t = b * SL

    cc = jnp.zeros((16, d), F32).at[:b].set(c).at[8].set(c_ctx)
    mod = _ada_table(cc, ada_w, ada_b)
    x2 = jnp.concatenate([x, ctx], axis=1).reshape(t, d)

    lane = np.arange(LANES)
    bd64 = _block_diag(lane // 64)
    lm64 = lane % 64
    rope64 = _rope_tables(64, lm64)
    scale64 = 64 ** -0.5 * LOG2E

    r3 = lambda a: a.reshape(b, SL, a.shape[-1])
    for l in range(depth):
        mod3 = mod[l].reshape(16 * 6, 1, d)
        if l == 0:
            w = na_w_qkv[0].astype(BF16)
            q, k, v = _project(x2, mod3, norm_g[l, 0], w, _qkv_plan(1024, 1024, 1024), (1024, 1024, 1024),
                               bd64, _head64_vec(na_q_norm[0], na_k_norm[0], scale64), name="na_proj")
            attn = _na_attention(r3(q), r3(k), r3(v), _na_bias_table(na_rpb[0]), b)
            w_o = na_w_o[0]
        elif l == 1:
            w = sw_w_qkv[0].astype(BF16)
            q, k, v, x2 = _project(x2, mod3, norm_g[l, 0], w, _qkv_plan(1024, 256, 256), (1024, 256, 256),
                                   bd64, _head64_vec(sw_q_norm[0], sw_k_norm[0], scale64),
                                   rope=rope64, rope_half=32, pending=pending, name="sw_proj")
            attn = _sw_attention(r3(q), r3(k), r3(v), sw_sink[0] * LOG2E, b)
            w_o = sw_w_o[0]
        elif l == 2:
            p = dict(w_dqkv=mla_w_dqkv[0], q_a_norm=mla_q_a_norm[0], kv_a_norm=mla_kv_a_norm[0],
                     w_uq=mla_w_uq[0], w_ukv=mla_w_ukv[0], q_norm=mla_q_norm[0], k_norm=mla_k_norm[0])
            q, k, v, x2 = _mla_project(x2, mod3, norm_g[l, 0], p, pending)
            attn = _mla_attention(r3(q), r3(k), r3(v), b)
            w_o = mla_w_o[0]
        else:
            w = diff_w_qkv[0].astype(BF16)
            q, k, v, x2 = _project(x2, mod3, norm_g[l, 0], w, _qkv_plan(1024, 1024, 1024), (1024, 1024, 1024),
                                   bd64, _head64_vec(diff_q_norm[0], diff_k_norm[0], scale64),
                                   rope=rope64, rope_half=32, pending=pending, name="diff_proj")
            lambda_init = 0.8 - 0.6 * math.exp(-0.3 * l)
            attn = _diff_attention(r3(q), r3(k), r3(v), diff_lambda[0], diff_subln[0], lambda_init, b)
            w_o = diff_w_o[0]

        wr = jnp.concatenate([moe_w_router_group[l], moe_w_router_expert[l]], axis=1)
        wr = jnp.pad(wr, ((0, 0), (0, LANES - wr.shape[1])))
        wr_hi = wr.astype(BF16)
        wr_lo = (wr - wr_hi.astype(F32)).astype(BF16)
        br = jnp.concatenate([moe_b_router_group[l], moe_b_router_expert[l]])
        br = jnp.pad(br, (0, LANES - br.shape[0])).reshape(1, LANES)
        x2, h_bf, route, counts_f = _oproj_router(attn.reshape(t, -1), x2, mod3, w_o.astype(BF16), norm_g[l, 1],
                                                  jnp.concatenate([wr_hi, wr_lo], axis=1), br)
        y0, y1 = _moe_rows(h_bf, route, counts_f, moe_w_gate_up, moe_w_down, l)
        pending = (y0, y1, route, mod3)

    return _combine(x2, *pending).reshape(b, SL, d)
```

```python
import functools
import math

import jax
import jax.numpy as jnp
import numpy as np
from jax import lax
from jax.experimental import pallas as pl
from jax.experimental.pallas import tpu as pltpu

F32 = jnp.float32
BF16 = jnp.bfloat16

D_MODEL = 1024
SEQ = 4096
CTX = 256
SL = SEQ + CTX
GRID_W = 64
GRID_ROWS = SEQ // GRID_W
EPS = 1e-6
ROPE_BASE = 10000.0
NEG = -1e30
LOG2E = 1.4426950408889634

NA_KH, NA_KW = 8, 16
SW_WINDOW = 128
SW_HEADS, SW_KV_HEADS = 16, 4
MLA_Q_RANK, MLA_KV_RANK, MLA_NOPE, MLA_ROPE, MLA_V, MLA_HEADS = 384, 256, 64, 32, 64, 16
DIFF_HEADS = 8
N_GROUPS, EXPERTS_PER_GROUP, N_EXPERTS, D_EXPERT = 4, 8, 32, 256

LANES = 128
MXU_N = 256
TM = 256
TILES_PER_SLAB = SL // TM
MOE_BLK = 512
VMEM_LIMIT = 56 * 1024 * 1024


def _cparams(*sem):
    return pltpu.CompilerParams(dimension_semantics=sem, vmem_limit_bytes=VMEM_LIMIT)


def _lane_iota(shape):
    return lax.broadcasted_iota(jnp.int32, shape, len(shape) - 1)


def _dot(a, b):
    return jnp.dot(a, b, preferred_element_type=F32)


def _dot_nt(a, b):
    return lax.dot_general(a, b, (((1,), (1,)), ((), ())), preferred_element_type=F32)


def _ada_kernel(cc_ref, w_ref, b_ref, o_ref):
    a = cc_ref[...]
    a = a * jax.nn.sigmoid(a)
    hi = a.astype(BF16)
    lo = (a - hi.astype(F32)).astype(BF16)
    w = w_ref[...]
    whi = w.astype(BF16)
    wlo = (w - whi.astype(F32)).astype(BF16)
    o_ref[...] = _dot(hi, whi) + _dot(lo, whi) + _dot(hi, wlo) + b_ref[...]


def _ada_table(cc, ada_w, ada_b):
    depth, d, n = ada_w.shape
    tn = 1024
    return pl.pallas_call(
        _ada_kernel,
        grid=(depth, n // tn),
        in_specs=[pl.BlockSpec((16, d), lambda l, j: (0, 0)),
                  pl.BlockSpec((None, d, tn), lambda l, j: (l, 0, j)),
                  pl.BlockSpec((None, 1, tn), lambda l, j: (l, 0, j))],
        out_specs=pl.BlockSpec((None, 16, tn), lambda l, j: (l, 0, j)),
        out_shape=jax.ShapeDtypeStruct((depth, 16, n), F32),
        compiler_params=_cparams("parallel", "parallel"),
        name="ada_table",
    )(cc, ada_w, ada_b.reshape(depth, 1, n))


def _mod_spec(part):
    def index(i):
        m = jnp.where(i % TILES_PER_SLAB == TILES_PER_SLAB - 1, 8, i // TILES_PER_SLAB)
        return (m * 6 + part, 0, 0)
    return pl.BlockSpec((None, 1, D_MODEL), index)


def _modulated(x, g, sc, sh):
    ms = jnp.mean(x * x, axis=-1, keepdims=True)
    return (x * lax.rsqrt(ms + EPS)) * g * (1.0 + sc) + sh


def _group_norm_rope(y, bd, invcnt, gain, cos=None, sin=None, half=None):
    ss = _dot((y * y).astype(BF16), bd)
    yn = y * lax.rsqrt(ss * invcnt + EPS) * gain
    if cos is not None:
        lane = _lane_iota(yn.shape)
        fwd = pltpu.roll(yn, LANES - half, 1)
        bwd = pltpu.roll(yn, half, 1)
        partner = jnp.where((lane & (2 * half - 1)) < half, fwd, bwd)
        yn = yn * cos + partner * sin
    return yn


def _block_diag(group_of_lane):
    g = np.asarray(group_of_lane)
    return jnp.asarray((g[:, None] == g[None, :]).astype(np.float32), dtype=BF16)


def _rope_tables(rot_dim, lane_map):
    t = jnp.arange(SEQ, dtype=jnp.int32)
    n_freq = rot_dim // 4
    inv_freq = ROPE_BASE ** (-jnp.arange(n_freq, dtype=F32) / n_freq)
    rows = (t // GRID_W).astype(F32)
    cols = (t % GRID_W).astype(F32)
    ang = jnp.concatenate([rows[:, None] * inv_freq, cols[:, None] * inv_freq], axis=-1)
    cos, sin = jnp.cos(ang), jnp.sin(ang)
    lm = np.asarray(lane_map)
    rot = lm >= 0
    idx = np.where(rot, lm % (rot_dim // 2), 0)
    sign = np.where(lm < rot_dim // 2, -1.0, 1.0).astype(np.float32)
    cos_t = jnp.where(rot[None, :], cos[:, idx], 1.0)
    sin_t = jnp.where(rot[None, :], sin[:, idx] * sign[None, :], 0.0)
    ident_c = jnp.ones((CTX, LANES), F32)
    ident_s = jnp.zeros((CTX, LANES), F32)
    return jnp.concatenate([cos_t, ident_c], 0), jnp.concatenate([sin_t, ident_s], 0)


def _residual_with_moe(x_ref, y0_ref, y1_ref, route_ref, g2_ref):
    route = route_ref[...]
    y = route[:, 0:1] * y0_ref[...].astype(F32) + route[:, 1:2] * y1_ref[...].astype(F32)
    return x_ref[...] + g2_ref[...] * y


def _proj_kernel(plan, rope_half, has_pending, x_ref, g_ref, sc_ref, sh_ref, w_ref, bd_ref, vec_ref, *rest):
    if has_pending:
        x = _residual_with_moe(x_ref, *rest[:4])
        rest = rest[4:]
        rest[-1][...] = x
        rest = rest[:-1]
    else:
        x = x_ref[...]
    if rope_half is not None:
        cos_ref, sin_ref = rest[:2]
        outs = rest[2:]
        cos, sin = cos_ref[...], sin_ref[...]
    else:
        outs = rest
        cos = sin = None
    h = _modulated(x, g_ref[...], sc_ref[...], sh_ref[...]).astype(BF16)
    bd = bd_ref[...]
    for c0, ep, out_idx, oc0 in plan:
        acc = _dot(h, w_ref[:, c0:c0 + MXU_N])
        for s in range(MXU_N // LANES):
            y = acc[:, s * LANES:(s + 1) * LANES]
            if ep is not None:
                y = _group_norm_rope(y, bd, vec_ref[ep, 0:1, :], vec_ref[ep, 1:2, :], cos, sin, rope_half)
            lo = oc0 + s * LANES
            outs[out_idx][:, lo:lo + LANES] = y.astype(BF16)


def _pending_specs(pending, d):
    row = lambda w: pl.BlockSpec((TM, w), lambda i: (i, 0))
    return [row(d), row(d), row(LANES), _mod_spec(5)], list(pending)


def _project(x2, mod3, g, w_bf, plan, out_widths, bd, vec, rope=None, rope_half=None, pending=None, name="proj"):
    t, d = x2.shape
    n = w_bf.shape[1]
    n_tiles = t // TM
    in_specs = [pl.BlockSpec((TM, d), lambda i: (i, 0)),
                pl.BlockSpec((1, d), lambda i: (0, 0)),
                _mod_spec(1), _mod_spec(0),
                pl.BlockSpec((d, n), lambda i: (0, 0)),
                pl.BlockSpec(bd.shape, lambda i: (0, 0)),
                pl.BlockSpec(vec.shape, lambda i: (0, 0, 0))]
    args = [x2, g.reshape(1, d), mod3, mod3, w_bf, bd, vec]
    out_specs = [pl.BlockSpec((TM, w), lambda i: (i, 0)) for w in out_widths]
    out_shape = [jax.ShapeDtypeStruct((t, w), BF16) for w in out_widths]
    if pending is not None:
        specs, pargs = _pending_specs(pending, d)
        in_specs += specs
        args += pargs
        out_specs.append(pl.BlockSpec((TM, d), lambda i: (i, 0)))
        out_shape.append(jax.ShapeDtypeStruct((t, d), F32))
    if rope is not None:
        in_specs += [pl.BlockSpec((TM, LANES), lambda i: (i % TILES_PER_SLAB, 0))] * 2
        args += list(rope)
    return pl.pallas_call(
        functools.partial(_proj_kernel, plan, rope_half, pending is not None),
        grid=(n_tiles,),
        in_specs=in_specs,
        out_specs=out_specs,
        out_shape=out_shape,
        compiler_params=_cparams("parallel"),
        name=name,
    )(*args)


def _qkv_plan(nq, nk, nv):
    plan = []
    for c0 in range(0, nq, MXU_N):
        plan.append((c0, 0, 0, c0))
    for c0 in range(0, nk, MXU_N):
        plan.append((nq + c0, 1, 1, c0))
    for c0 in range(0, nv, MXU_N):
        plan.append((nq + nk + c0, None, 2, c0))
    return tuple(plan)


def _head64_vec(q_norm, k_norm, q_scale):
    inv = jnp.full((LANES,), 1.0 / 64, F32)
    z = jnp.zeros((6, LANES), F32)
    vq = jnp.concatenate([inv[None], (jnp.tile(q_norm, 2) * q_scale)[None], z], 0)
    vk = jnp.concatenate([inv[None], jnp.tile(k_norm, 2)[None], z], 0)
    return jnp.stack([vq, vk], 0)


def _mla_proj_kernel(x_ref, g_ref, sc_ref, sh_ref, wd_ref, wuq_ref, wuk_ref, wuv_ref, bd_ref, vec_ref,
                     an_ref, cos_ref, sin_ref, y0_ref, y1_ref, route_ref, g2_ref, q_out, k_out, v_out, x_out):
    x = _residual_with_moe(x_ref, y0_ref, y1_ref, route_ref, g2_ref)
    x_out[...] = x
    h = _modulated(x, g_ref[...], sc_ref[...], sh_ref[...]).astype(BF16)
    cos, sin = cos_ref[...], sin_ref[...]
    half = MLA_ROPE // 2
    cq = _dot(h, wd_ref[:, 0:MLA_Q_RANK])
    ckv = _dot(h, wd_ref[:, MLA_Q_RANK:MLA_Q_RANK + MLA_KV_RANK])
    kr = _dot(h, wd_ref[:, MLA_Q_RANK + MLA_KV_RANK:])
    cq = cq * lax.rsqrt(jnp.mean(cq * cq, axis=-1, keepdims=True) + EPS) * an_ref[0:1, 0:MLA_Q_RANK]
    ckv = ckv * lax.rsqrt(jnp.mean(ckv * ckv, axis=-1, keepdims=True) + EPS) * an_ref[1:2, 0:MLA_KV_RANK]
    cq = cq.astype(BF16)
    ckv = ckv.astype(BF16)
    kr = _group_norm_rope(kr, bd_ref[1], vec_ref[2, 0:1, :], vec_ref[2, 1:2, :], cos[:, LANES:], sin[:, LANES:], half)
    kr = pltpu.roll(kr, MLA_NOPE, 1)
    for hd in range(MLA_HEADS):
        c0 = hd * LANES
        qh = _dot(cq, wuq_ref[:, c0:c0 + LANES])
        qh = _group_norm_rope(qh, bd_ref[0], vec_ref[0, 0:1, :], vec_ref[0, 1:2, :], cos[:, :LANES], sin[:, :LANES], half)
        q_out[:, c0:c0 + LANES] = qh.astype(BF16)
        kh = _dot(ckv, wuk_ref[:, c0:c0 + LANES])
        kh = _group_norm_rope(kh, bd_ref[0], vec_ref[1, 0:1, :], vec_ref[1, 1:2, :]) + kr
        k_out[:, c0:c0 + LANES] = kh.astype(BF16)
    for c0 in range(0, MLA_HEADS * MLA_V, MXU_N):
        v_out[:, c0:c0 + MXU_N] = _dot(ckv, wuv_ref[:, c0:c0 + MXU_N]).astype(BF16)


def _mla_project(x2, mod3, g, p, pending):
    t, d = x2.shape
    hq = MLA_NOPE + MLA_ROPE
    n_tiles = t // TM
    wd = jnp.pad(p["w_dqkv"], ((0, 0), (0, LANES - MLA_ROPE))).astype(BF16)
    wuq = p["w_uq"].reshape(MLA_Q_RANK, MLA_HEADS, hq)
    wuq = jnp.pad(wuq, ((0, 0), (0, 0), (0, LANES - hq))).reshape(MLA_Q_RANK, MLA_HEADS * LANES).astype(BF16)
    wukv = p["w_ukv"].reshape(MLA_KV_RANK, MLA_HEADS, MLA_NOPE + MLA_V)
    wuk = jnp.pad(wukv[:, :, :MLA_NOPE], ((0, 0), (0, 0), (0, LANES - MLA_NOPE)))
    wuk = wuk.reshape(MLA_KV_RANK, MLA_HEADS * LANES).astype(BF16)
    wuv = wukv[:, :, MLA_NOPE:].reshape(MLA_KV_RANK, MLA_HEADS * MLA_V).astype(BF16)
    scale = (MLA_NOPE + MLA_ROPE) ** -0.5 * LOG2E
    lane = np.arange(LANES)
    grp_head = np.where(lane < 64, 0, np.where(lane < 96, 1, 2))
    grp_kr = np.where(lane < 32, 0, 1)
    bd = jnp.stack([_block_diag(grp_head), _block_diag(grp_kr)], 0)
    inv_head = jnp.asarray(np.where(lane < 64, 1 / 64, 1 / 32), F32)
    zpad = jnp.zeros((LANES - hq,), F32)
    gq = jnp.concatenate([p["q_norm"], zpad]) * scale
    gk = jnp.concatenate([p["k_norm"][:MLA_NOPE], jnp.zeros((LANES - MLA_NOPE,), F32)])
    gkr = jnp.concatenate([p["k_norm"][MLA_NOPE:], jnp.zeros((LANES - MLA_ROPE,), F32)])
    inv_kr = jnp.asarray(np.where(lane < 32, 1 / 32, 1 / 96), F32)
    z6 = jnp.zeros((6, LANES), F32)
    vec = jnp.stack([jnp.concatenate([inv_head[None], gq[None], z6], 0),
                     jnp.concatenate([inv_head[None], gk[None], z6], 0),
                     jnp.concatenate([inv_kr[None], gkr[None], z6], 0)], 0)
    an = jnp.zeros((8, MLA_Q_RANK), F32)
    an = an.at[0].set(p["q_a_norm"]).at[1, :MLA_KV_RANK].set(p["kv_a_norm"])
    lm_head = np.where((lane >= 64) & (lane < 96), lane - 64, -1)
    lm_kr = np.where(lane < 32, lane, -1)
    ch, sh_ = _rope_tables(MLA_ROPE, lm_head)
    ck, sk = _rope_tables(MLA_ROPE, lm_kr)
    cos2 = jnp.concatenate([ch, ck], 1)
    sin2 = jnp.concatenate([sh_, sk], 1)

    full = lambda a: pl.BlockSpec(a.shape, lambda i: (0,) * a.ndim)
    rope_spec = pl.BlockSpec((TM, 2 * LANES), lambda i: (i % TILES_PER_SLAB, 0))
    out_w = (MLA_HEADS * LANES, MLA_HEADS * LANES, MLA_HEADS * MLA_V)
    pend_specs, pend_args = _pending_specs(pending, d)
    return pl.pallas_call(
        _mla_proj_kernel,
        grid=(n_tiles,),
        in_specs=[pl.BlockSpec((TM, d), lambda i: (i, 0)), pl.BlockSpec((1, d), lambda i: (0, 0)),
                  _mod_spec(1), _mod_spec(0), full(wd), full(wuq), full(wuk), full(wuv), full(bd), full(vec),
                  full(an), rope_spec, rope_spec] + pend_specs,
        out_specs=[pl.BlockSpec((TM, w), lambda i: (i, 0)) for w in out_w + (d,)],
        out_shape=[jax.ShapeDtypeStruct((t, w), BF16) for w in out_w] + [jax.ShapeDtypeStruct((t, d), F32)],
        compiler_params=_cparams("parallel"),
        name="mla_proj",
    )(x2, g.reshape(1, d), mod3, mod3, wd, wuq, wuk, wuv, bd, vec, an, cos2, sin2, *pend_args)


def _softmax_pv(s_list, v_list, sink=None):
    m = functools.reduce(jnp.maximum, [jnp.max(s, axis=-1, keepdims=True) for s in s_list])
    if sink is not None:
        m = jnp.maximum(m, sink)
    ps = [jnp.exp2(s - m) for s in s_list]
    l = functools.reduce(lambda a, b: a + b, [jnp.sum(p, axis=-1, keepdims=True) for p in ps])
    if sink is not None:
        l = l + jnp.exp2(sink - m)
    o = functools.reduce(lambda a, b: a + b, [_dot(p.astype(BF16), v) for p, v in zip(ps, v_list)])
    return o * (1.0 / l)


def _split_pair(x, lo):
    zero = jnp.zeros_like(x)
    return jnp.where(lo, x, zero), jnp.where(lo, zero, x)


def _na_kernel(q_ref, k_ref, v_ref, bias_ref, o_ref):
    lo = _lane_iota((1, LANES)) < 64
    kc = k_ref[SEQ:SL, :]
    vc = v_ref[SEQ:SL, :]
    nb = NA_KH * GRID_W

    def pair_attend(q, ks, vs, bias):
        n = q.shape[0]
        qa, qb = _split_pair(q, lo)
        qs = jnp.concatenate([qa, qb], axis=0)
        s_list = [_dot_nt(qs, k) for k in ks]
        if bias is not None:
            s_list[0] = s_list[0] + bias
        o = _softmax_pv(s_list, vs)
        return jnp.where(lo, o[:n], o[n:])

    def row_body(r, carry):
        row0 = jnp.clip(r - NA_KH // 2, 0, GRID_ROWS - NA_KH)
        q = q_ref[pl.ds(pl.multiple_of(r * GRID_W, GRID_W), GRID_W), :]
        ks = pl.multiple_of(row0 * GRID_W, GRID_W)
        kn = k_ref[pl.ds(ks, nb), :]
        vn = v_ref[pl.ds(ks, nb), :]
        o = pair_attend(q, [kn, kc], [vn, vc], bias_ref[r - row0])
        o_ref[pl.ds(pl.multiple_of(r * GRID_W, GRID_W), GRID_W), :] = o.astype(BF16)
        return carry

    lax.fori_loop(0, GRID_ROWS, row_body, 0, unroll=8)
    o_ref[SEQ:SL, :] = pair_attend(q_ref[SEQ:SL, :], [kc], [vc], None).astype(BF16)


def _na_bias_table(rpb):
    heads = rpb.shape[0]
    n_col = 2 * NA_KW - 1
    r2 = rpb.reshape(heads, 2 * NA_KH - 1, n_col) * LOG2E
    padded = jnp.pad(r2, ((0, 0), (0, 0), (GRID_W, GRID_W)))
    base = GRID_W + NA_KW - 1
    toeplitz = jnp.stack([padded[:, :, base - qc:base - qc + GRID_W] for qc in range(GRID_W)], axis=2)
    qc = np.arange(GRID_W)[:, None, None]
    kc = np.arange(GRID_W)[None, None, :]
    col0 = np.clip(qc - NA_KW // 2, 0, GRID_W - NA_KW)
    valid = np.broadcast_to((kc >= col0) & (kc < col0 + NA_KW), (GRID_W, NA_KH, GRID_W))
    valid = valid.reshape(GRID_W, NA_KH * GRID_W)
    classes = []
    for c in range(NA_KH):
        t = toeplitz[:, NA_KH - 1 - c:2 * NA_KH - 1 - c]
        t = t.transpose(0, 2, 1, 3).reshape(heads, GRID_W, NA_KH * GRID_W)
        classes.append(jnp.where(valid[None], t, NEG))
    tab = jnp.stack(classes, 0)
    return tab.reshape(NA_KH, heads // 2, 2 * GRID_W, NA_KH * GRID_W)


def _na_attention(q, k, v, bias, b):
    hp = q.shape[-1] // LANES
    slab = lambda: pl.BlockSpec((None, SL, LANES), lambda i, j: (i, 0, j))
    return pl.pallas_call(
        _na_kernel,
        grid=(b, hp),
        in_specs=[slab(), slab(), slab(),
                  pl.BlockSpec((NA_KH, None, 2 * GRID_W, NA_KH * GRID_W), lambda i, j: (0, j, 0, 0))],
        out_specs=slab(),
        out_shape=jax.ShapeDtypeStruct(q.shape, BF16),
        compiler_params=_cparams("parallel", "parallel"),
        name="na_attention",
    )(q, k, v, bias)


Q_BLK = 128


def _sw_kernel(sink_ref, q_ref, k_ref, v_ref, o_ref):
    g = pl.program_id(1)
    lane = _lane_iota((1, LANES))
    lo = lane < 64
    first_half = (jnp.zeros((1, LANES), jnp.int32) + (g % 2)) == 0
    target = (lane // 64) == (g % 2)
    kc = k_ref[SEQ:SL, :]
    vc = v_ref[SEQ:SL, :]
    span = Q_BLK + 2 * SW_WINDOW
    sinks = [sink_ref[4 * g + t] for t in range(4)]

    def swap(x):
        return jnp.concatenate([x[:, 64:], x[:, :64]], axis=1)

    def stack_heads(q):
        parts = []
        for blk in range(2):
            qb = q[:, blk * LANES:(blk + 1) * LANES]
            qs = swap(qb)
            even = jnp.where(first_half, qb, qs)
            odd = jnp.where(first_half, qs, qb)
            zero = jnp.zeros_like(qb)
            parts += [jnp.where(target, even, zero), jnp.where(target, odd, zero)]
        return jnp.concatenate(parts, axis=0)

    def unstack_heads(o, n):
        blocks = []
        for blk in range(2):
            oe = o[(2 * blk) * n:(2 * blk + 1) * n]
            oo = o[(2 * blk + 1) * n:(2 * blk + 2) * n]
            oe = jnp.where(first_half, oe, pltpu.roll(oe, 64, 1))
            oo = jnp.where(first_half, pltpu.roll(oo, 64, 1), oo)
            blocks.append(jnp.where(lo, oe, oo))
        return jnp.concatenate(blocks, axis=1)

    def sink_col(n):
        row = lax.broadcasted_iota(jnp.int32, (4 * n, 1), 0)
        return jnp.where(row < n, sinks[0], jnp.where(row < 2 * n, sinks[1],
                         jnp.where(row < 3 * n, sinks[2], sinks[3])))

    def block_body(i, carry):
        q0 = pl.multiple_of(i * Q_BLK, Q_BLK)
        start = pl.multiple_of(jnp.clip((i - 1) * Q_BLK, 0, SEQ - span), Q_BLK)
        qs = stack_heads(q_ref[pl.ds(q0, Q_BLK), :])
        kw = k_ref[pl.ds(start, span), :]
        vw = v_ref[pl.ds(start, span), :]
        s_loc = _dot_nt(qs, kw)
        qpos = q0 + (lax.broadcasted_iota(jnp.int32, (4 * Q_BLK, 1), 0) & (Q_BLK - 1))
        kpos = start + _lane_iota((1, span))
        s_loc = jnp.where(jnp.abs(kpos - qpos) <= SW_WINDOW, s_loc, NEG)
        s_ctx = _dot_nt(qs, kc)
        o = _softmax_pv([s_loc, s_ctx], [vw, vc], sink_col(Q_BLK))
        o_ref[pl.ds(q0, Q_BLK), :] = unstack_heads(o, Q_BLK).astype(BF16)
        return carry

    lax.fori_loop(0, SEQ // Q_BLK, block_body, 0, unroll=4)
    qs = stack_heads(q_ref[SEQ:SL, :])
    o = _softmax_pv([_dot_nt(qs, kc)], [vc], sink_col(CTX))
    o_ref[SEQ:SL, :] = unstack_heads(o, CTX).astype(BF16)


def _sw_attention(q, k, v, sink2, b):
    nq = 4 * 64
    slabq = lambda: pl.BlockSpec((None, SL, nq), lambda i, g: (i, 0, g))
    slabk = lambda: pl.BlockSpec((None, SL, LANES), lambda i, g: (i, 0, g // 2))
    return pl.pallas_call(
        _sw_kernel,
        grid=(b, SW_KV_HEADS),
        in_specs=[pl.BlockSpec(memory_space=pltpu.SMEM), slabq(), slabk(), slabk()],
        out_specs=slabq(),
        out_shape=jax.ShapeDtypeStruct(q.shape, BF16),
        compiler_params=_cparams("parallel", "parallel"),
        name="sw_attention",
    )(sink2, q, k, v)


DENSE_TQ = 512
DENSE_TK = 2048


def _online_step(s, v, m_ref, l_ref, acc_ref):
    chunks = s.shape[1] // LANES
    m_prev = m_ref[...]
    m_new = jnp.maximum(m_prev, jnp.max(s, axis=-1, keepdims=True))
    alpha = jnp.exp2(m_prev - m_new)
    p = jnp.exp2(s - jnp.concatenate([m_new] * chunks, axis=1))
    psum = functools.reduce(lambda a, b: a + b, [p[:, c * LANES:(c + 1) * LANES] for c in range(chunks)])
    l_ref[...] = alpha * l_ref[...] + psum
    acc_ref[...] = alpha * acc_ref[...] + _dot(p.astype(BF16), v)
    m_ref[...] = m_new


def _dense_streams(streams, v_ref, n_lat, vc):
    views = []
    for qs, _, (m_ref, l_ref, acc_ref) in streams:
        rows = qs.shape[0]
        m_ref[0:rows, :] = jnp.full((rows, LANES), NEG, F32)
        l_ref[0:rows, :] = jnp.zeros((rows, LANES), F32)
        acc_ref[0:rows, :] = jnp.zeros((rows, LANES), F32)
        views.append((m_ref.at[0:rows, :], l_ref.at[0:rows, :], acc_ref.at[0:rows, :]))

    for c in range(n_lat // DENSE_TK):
        v = v_ref[c * DENSE_TK:(c + 1) * DENSE_TK, :]
        for (qs, k_ref, _), view in zip(streams, views):
            _online_step(_dot_nt(qs, k_ref[c * DENSE_TK:(c + 1) * DENSE_TK, :]), v, *view)
    outs = []
    for (qs, k_ref, _), (mr, lr, ar) in zip(streams, views):
        _online_step(_dot_nt(qs, k_ref[SEQ:SL, :]), vc, mr, lr, ar)
        outs.append(ar[...] * (1.0 / jnp.sum(lr[...], axis=-1, keepdims=True)))
    return outs


def _diff_kernel(lambda_init, lam_ref, subln_ref, q_ref, k_ref, v_ref, o_ref, m_ref, l_ref, acc_ref):
    lo = _lane_iota((1, LANES)) < 64
    lam = lam_ref[...]
    lam_full = (jnp.exp(jnp.sum(lam[0:1] * lam[1:2], axis=-1, keepdims=True))
                - jnp.exp(jnp.sum(lam[2:3] * lam[3:4], axis=-1, keepdims=True)) + lambda_init)
    vc = v_ref[SEQ:SL, :]
    gain = subln_ref[...] * (1.0 - lambda_init)

    def q_body(i, carry):
        q0 = pl.multiple_of(i * DENSE_TQ, DENSE_TQ)
        qa, qb = _split_pair(q_ref[pl.ds(q0, DENSE_TQ), :], lo)
        qs = jnp.concatenate([qa, qb], axis=0)
        o, = _dense_streams([(qs, k_ref, (m_ref, l_ref, acc_ref))], v_ref, SEQ, vc)
        od = o[:DENSE_TQ] - lam_full * o[DENSE_TQ:]
        od = od * lax.rsqrt(jnp.mean(od * od, axis=-1, keepdims=True) + EPS) * gain
        o_ref[pl.ds(q0, DENSE_TQ), :] = od.astype(BF16)
        return carry

    lax.fori_loop(0, SEQ // DENSE_TQ, q_body, 0)
    o_ref[SEQ:SL, :] = jnp.zeros((CTX, LANES), BF16)


def _diff_attention(q, k, v, lam, subln, lambda_init, b):
    slab = lambda: pl.BlockSpec((None, SL, LANES), lambda i, j: (i, 0, j))
    rows = 2 * DENSE_TQ
    return pl.pallas_call(
        functools.partial(_diff_kernel, lambda_init),
        grid=(b, DIFF_HEADS),
        in_specs=[pl.BlockSpec(lam.shape, lambda i, j: (0, 0)),
                  pl.BlockSpec((1, LANES), lambda i, j: (0, 0)),
                  slab(), slab(), slab()],
        out_specs=slab(),
        out_shape=jax.ShapeDtypeStruct(v.shape, BF16),
        scratch_shapes=[pltpu.VMEM((rows, LANES), F32)] * 3,
        compiler_params=_cparams("parallel", "parallel"),
        name="diff_attention",
    )(lam, subln.reshape(1, LANES), q, k, v)


def _mla_kernel(q_ref, k_ref, v_ref, o_ref, *scratch):
    lo = _lane_iota((1, LANES)) < 64
    vc = v_ref[SEQ:SL, :]

    def two_heads(q, n_lat):
        streams = []
        for hd in range(2):
            sl = slice(hd * LANES, (hd + 1) * LANES)
            streams.append((q[:, sl], k_ref.at[:, sl], scratch[3 * hd:3 * hd + 3]))
        outs = _dense_streams(streams, v_ref, n_lat, vc)
        return jnp.where(lo, outs[0], outs[1])

    def q_body(i, carry):
        q0 = pl.multiple_of(i * DENSE_TQ, DENSE_TQ)
        o = two_heads(q_ref[pl.ds(q0, DENSE_TQ), :], SEQ)
        o_ref[pl.ds(q0, DENSE_TQ), :] = o.astype(BF16)
        return carry

    lax.fori_loop(0, SEQ // DENSE_TQ, q_body, 0)
    o_ref[SEQ:SL, :] = two_heads(q_ref[SEQ:SL, :], 0).astype(BF16)


def _mla_attention(q, k, v, b):
    slabq = lambda: pl.BlockSpec((None, SL, 2 * LANES), lambda i, j: (i, 0, j))
    slabv = lambda: pl.BlockSpec((None, SL, LANES), lambda i, j: (i, 0, j))
    return pl.pallas_call(
        _mla_kernel,
        grid=(b, MLA_HEADS // 2),
        in_specs=[slabq(), slabq(), slabv()],
        out_specs=slabv(),
        out_shape=jax.ShapeDtypeStruct(v.shape, BF16),
        scratch_shapes=[pltpu.VMEM((DENSE_TQ, LANES), F32)] * 6,
        compiler_params=_cparams("parallel", "parallel"),
        name="mla_attention",
    )(q, k, v)


def _oproj_kernel(a_ref, x_ref, wo_ref, g1_ref, ng_ref, sc_ref, sh_ref, wr_cat_ref, br_ref,
                  xo_ref, h_ref, route_ref, cnt_ref):
    y = _dot(a_ref[...], wo_ref[...])
    x = x_ref[...] + g1_ref[...] * y
    xo_ref[...] = x
    h = _modulated(x, ng_ref[...], sc_ref[...], sh_ref[...])
    hi = h.astype(BF16)
    h_ref[...] = hi
    lo = (h - hi.astype(F32)).astype(BF16)
    zz = _dot(hi, wr_cat_ref[...])
    z = zz[:, :LANES] + zz[:, LANES:] + _dot(lo, wr_cat_ref[:, :LANES]) + br_ref[...]
    lane = _lane_iota(z.shape)
    lane_f = lane.astype(F32)
    big = jnp.float32(1e9)
    is_g = lane < N_GROUPS
    zg = jnp.where(is_g, z, NEG)
    gmax = jnp.max(zg, axis=-1, keepdims=True)
    gsum = jnp.sum(jnp.where(is_g, jnp.exp(zg - gmax), 0.0), axis=-1, keepdims=True)
    g_w = 1.0 / gsum
    g_idx = jnp.min(jnp.where(is_g & (zg == gmax), lane_f, big), axis=-1, keepdims=True)
    e_lo = N_GROUPS + EXPERTS_PER_GROUP * g_idx
    is_e = (lane_f >= e_lo) & (lane_f < e_lo + EXPERTS_PER_GROUP)
    ze = jnp.where(is_e, z, NEG)
    z1 = jnp.max(ze, axis=-1, keepdims=True)
    i1 = jnp.min(jnp.where(is_e & (ze == z1), lane_f, big), axis=-1, keepdims=True)
    ze2 = jnp.where(lane_f == i1, NEG, ze)
    z2 = jnp.max(ze2, axis=-1, keepdims=True)
    i2 = jnp.min(jnp.where(is_e & (ze2 == z2) & (lane_f != i1), lane_f, big), axis=-1, keepdims=True)
    r = jnp.exp(z2 - z1)
    gate1 = g_w / (1.0 + r)
    gate2 = g_w * r / (1.0 + r)
    e1 = i1 - N_GROUPS
    e2 = i2 - N_GROUPS
    route = jnp.where(lane == 0, gate1, jnp.where(lane == 1, gate2,
                      jnp.where(lane == 2, e1, jnp.where(lane == 3, e2, 0.0))))
    route_ref[...] = route

    @pl.when(pl.program_id(0) == 0)
    def _():
        cnt_ref[...] = jnp.zeros_like(cnt_ref)

    hits = jnp.where(lane_f == e1, 1.0, 0.0) + jnp.where(lane_f == e2, 1.0, 0.0)
    cnt_ref[...] += jnp.sum(hits, axis=0, keepdims=True)


def _route_pos_kernel(route_ref, base_ref, pos_ref, carry_ref):
    @pl.when(pl.program_id(0) == 0)
    def _():
        carry_ref[...] = base_ref[...]

    lane = _lane_iota((TM, LANES))
    lane_f = lane.astype(F32)
    r = lax.broadcasted_iota(jnp.int32, (TM, TM), 0)
    c = lax.broadcasted_iota(jnp.int32, (TM, TM), 1)
    tri = jnp.where(c <= r, 1.0, 0.0).astype(BF16)
    carry = carry_ref[...]
    for s in range(TILES_PER_SLAB):
        route = route_ref[s * TM:(s + 1) * TM, :]
        oh0 = jnp.where(lane_f == route[:, 2:3], 1.0, 0.0)
        oh1 = jnp.where(lane_f == route[:, 3:4], 1.0, 0.0)
        cnt0 = jnp.sum(oh0, axis=0, keepdims=True)
        pre0 = _dot(tri, oh0.astype(BF16)) - 1.0 + carry
        pre1 = _dot(tri, oh1.astype(BF16)) - 1.0 + carry + cnt0
        pos0 = jnp.sum(oh0 * pre0, axis=-1, keepdims=True)
        pos1 = jnp.sum(oh1 * pre1, axis=-1, keepdims=True)
        pos_ref[s * TM:(s + 1) * TM, :] = jnp.where(lane == 0, pos0, jnp.where(lane == 1, pos1, 0.0))
        carry = carry + cnt0 + jnp.sum(oh1, axis=0, keepdims=True)
    carry_ref[...] = carry


def _route_pos(route, base):
    t = route.shape[0]
    return pl.pallas_call(
        _route_pos_kernel,
        grid=(t // SL,),
        in_specs=[pl.BlockSpec((SL, LANES), lambda i: (i, 0)), pl.BlockSpec((1, LANES), lambda i: (0, 0))],
        out_specs=pl.BlockSpec((SL, LANES), lambda i: (i, 0)),
        out_shape=jax.ShapeDtypeStruct((t, LANES), F32),
        scratch_shapes=[pltpu.VMEM((1, LANES), F32)],
        compiler_params=_cparams("arbitrary"),
        name="route_pos",
    )(route, base)


def _oproj_router(attn2, x2, mod3, wo_bf, ng, wr_cat, br):
    t, d = x2.shape
    n_in = attn2.shape[1]
    full = lambda a: pl.BlockSpec(a.shape, lambda i: (0,) * a.ndim)
    row = lambda w: pl.BlockSpec((TM, w), lambda i: (i, 0))
    return pl.pallas_call(
        _oproj_kernel,
        grid=(t // TM,),
        in_specs=[row(n_in), row(d), full(wo_bf), _mod_spec(2), pl.BlockSpec((1, d), lambda i: (0, 0)),
                  _mod_spec(4), _mod_spec(3), full(wr_cat), full(br)],
        out_specs=[row(d), row(d), row(LANES), pl.BlockSpec((1, LANES), lambda i: (0, 0))],
        out_shape=[jax.ShapeDtypeStruct((t, d), F32), jax.ShapeDtypeStruct((t, d), BF16),
                   jax.ShapeDtypeStruct((t, LANES), F32), jax.ShapeDtypeStruct((1, LANES), F32)],
        compiler_params=_cparams("arbitrary"),
        name="oproj_router",
    )(attn2, x2, wo_bf, mod3, ng.reshape(1, d), mod3, mod3, wr_cat, br)


def _expert_kernel(be_ref, nu_ref, x_ref, wgu_ref, wd_ref, o_ref, wgu_bf, wd_bf):
    i = pl.program_id(0)
    new_expert = (i == 0) | (be_ref[i] != be_ref[jnp.maximum(i - 1, 0)])

    @pl.when(new_expert)
    def _():
        wgu_bf[...] = wgu_ref[...].astype(BF16)
        wd_bf[...] = wd_ref[...].astype(BF16)

    @pl.when(i < nu_ref[0])
    def _():
        gu = _dot(x_ref[...], wgu_bf[...])
        gte, up = gu[:, :D_EXPERT], gu[:, D_EXPERT:]
        a = (gte * jax.nn.sigmoid(gte) * up).astype(BF16)
        o_ref[...] = _dot(a, wd_bf[...]).astype(BF16)

    @pl.when(i >= nu_ref[0])
    def _():
        o_ref[...] = jnp.zeros_like(o_ref)


def _experts(xs, block_expert, n_used, wgu, wd, layer):
    n_rows, d = xs.shape
    n_blocks = n_rows // MOE_BLK
    return pl.pallas_call(
        _expert_kernel,
        grid_spec=pltpu.PrefetchScalarGridSpec(
            num_scalar_prefetch=2,
            grid=(n_blocks,),
            in_specs=[pl.BlockSpec((MOE_BLK, d), lambda i, be, nu: (i, 0)),
                      pl.BlockSpec((None, None, d, 2 * D_EXPERT), lambda i, be, nu: (layer, be[i], 0, 0)),
                      pl.BlockSpec((None, None, D_EXPERT, d), lambda i, be, nu: (layer, be[i], 0, 0))],
            out_specs=pl.BlockSpec((MOE_BLK, d), lambda i, be, nu: (i, 0)),
            scratch_shapes=[pltpu.VMEM((d, 2 * D_EXPERT), BF16), pltpu.VMEM((D_EXPERT, d), BF16)]),
        out_shape=jax.ShapeDtypeStruct((n_rows, d), BF16),
        compiler_params=_cparams("arbitrary"),
        name="experts",
    )(block_expert, n_used, xs, wgu, wd)


def _combine_kernel(x_ref, y0_ref, y1_ref, route_ref, g2_ref, o_ref):
    o_ref[...] = _residual_with_moe(x_ref, y0_ref, y1_ref, route_ref, g2_ref)


def _combine(x2, y0, y1, route, mod3):
    t, d = x2.shape
    row = lambda w: pl.BlockSpec((TM, w), lambda i: (i, 0))
    return pl.pallas_call(
        _combine_kernel,
        grid=(t // TM,),
        in_specs=[row(d), row(d), row(d), row(LANES), _mod_spec(5)],
        out_specs=row(d),
        out_shape=jax.ShapeDtypeStruct((t, d), F32),
        compiler_params=_cparams("parallel"),
        name="moe_combine",
    )(x2, y0, y1, route, mod3)


def _moe_dispatch(route, counts_f, t):
    counts = counts_f[0, :N_EXPERTS].astype(jnp.int32)
    padded = (counts + MOE_BLK - 1) // MOE_BLK * MOE_BLK
    seg_end = jnp.cumsum(padded)
    seg_start = seg_end - padded
    base = jnp.zeros((1, LANES), F32).at[0, :N_EXPERTS].set(seg_start.astype(F32))
    pos_f = _route_pos(route, base)
    pos = pos_f[:, :2].T.astype(jnp.int32)
    n_rows = -(-(2 * t + N_EXPERTS * (MOE_BLK - 1)) // MOE_BLK) * MOE_BLK
    n_blocks = n_rows // MOE_BLK
    token = jnp.tile(jnp.arange(t, dtype=jnp.int32), 2)
    row_token = (jnp.arange(n_rows, dtype=jnp.int32) % t).at[pos.reshape(-1)].set(
        token, mode="promise_in_bounds", unique_indices=True)
    block_start = jnp.arange(n_blocks, dtype=jnp.int32) * MOE_BLK
    block_expert = jnp.minimum(jnp.sum((seg_end[None, :] <= block_start[:, None]).astype(jnp.int32), axis=1),
                               N_EXPERTS - 1)
    n_used = (seg_end[-1] // MOE_BLK).astype(jnp.int32).reshape(1)
    return row_token, pos, block_expert, n_used


def _take_rows(a, idx):
    return a.at[idx].get(mode="promise_in_bounds")


def _moe_rows(h_bf, route, counts_f, wgu, wd, layer):
    t = h_bf.shape[0]
    row_token, pos, block_expert, n_used = _moe_dispatch(route, counts_f, t)
    xs = _take_rows(h_bf, row_token)
    ys = _experts(xs, block_expert, n_used, wgu, wd, layer)
    return _take_rows(ys, pos[0]), _take_rows(ys, pos[1])


def kernel(x, c, ctx, c_ctx, ada_w, ada_b, norm_g, moe_w_router_group, moe_b_router_group, moe_w_router_expert, moe_b_router_expert, moe_w_gate_up, moe_w_down, na_w_qkv, na_w_o, na_q_norm, na_k_norm, na_rpb, sw_w_qkv, sw_w_o, sw_q_norm, sw_k_norm, sw_sink, mla_w_dqkv, mla_q_a_norm, mla_kv_a_norm, mla_w_uq, mla_w_ukv, mla_q_norm, mla_k_norm, mla_w_o, diff_w_qkv, diff_q_norm, diff_k_norm, diff_lambda, diff_subln, diff_w_o):
    slab = _stack(x, c, ctx, c_ctx, ada_w, ada_b, norm_g, moe_w_router_group, moe_b_router_group, moe_w_router_expert, moe_b_router_expert, moe_w_gate_up, moe_w_down, na_w_qkv, na_w_o, na_q_norm, na_k_norm, na_rpb, sw_w_qkv, sw_w_o, sw_q_norm, sw_k_norm, sw_sink, mla_w_dqkv, mla_q_a_norm, mla_kv_a_norm, mla_w_uq, mla_w_ukv, mla_q_norm, mla_k_norm, mla_w_o, diff_w_qkv, diff_q_norm, diff_k_norm, diff_lambda, diff_subln, diff_w_o)
    return slab[:, :SEQ]


def _stack(x, c, ctx, c_ctx, ada_w, ada_b, norm_g, moe_w_router_group, moe_b_router_group, moe_w_router_expert, moe_b_router_expert, moe_w_gate_up, moe_w_down, na_w_qkv, na_w_o, na_q_norm, na_k_norm, na_rpb, sw_w_qkv, sw_w_o, sw_q_norm, sw_k_norm, sw_sink, mla_w_dqkv, mla_q_a_norm, mla_kv_a_norm, mla_w_uq, mla_w_ukv, mla_q_norm, mla_k_norm, mla_w_o, diff_w_qkv, diff_q_norm, diff_k_norm, diff_lambda, diff_subln, diff_w_o):
    b, s, d = x.shape
    assert (s, d) == (SEQ, D_MODEL) and ctx.shape == (b, CTX, d) and b <= 8
    depth = ada_w.shape[0]
    assert depth <= 4
    t = b * SL

    cc = jnp.zeros((16, d), F32).at[:b].set(c).at[8].set(c_ctx)
    mod = _ada_table(cc, ada_w, ada_b)
    x2 = jnp.concatenate([x, ctx], axis=1).reshape(t, d)

    lane = np.arange(LANES)
    bd64 = _block_diag(lane // 64)
    lm64 = lane % 64
    rope64 = _rope_tables(64, lm64)
    scale64 = 64 ** -0.5 * LOG2E

    r3 = lambda a: a.reshape(b, SL, a.shape[-1])
    for l in range(depth):
        mod3 = mod[l].reshape(16 * 6, 1, d)
        if l == 0:
            w = na_w_qkv[0].astype(BF16)
            q, k, v = _project(x2, mod3, norm_g[l, 0], w, _qkv_plan(1024, 1024, 1024), (1024, 1024, 1024),
                               bd64, _head64_vec(na_q_norm[0], na_k_norm[0], scale64), name="na_proj")
            attn = _na_attention(r3(q), r3(k), r3(v), _na_bias_table(na_rpb[0]), b)
            w_o = na_w_o[0]
        elif l == 1:
            w = sw_w_qkv[0].astype(BF16)
            q, k, v, x2 = _project(x2, mod3, norm_g[l, 0], w, _qkv_plan(1024, 256, 256), (1024, 256, 256),
                                   bd64, _head64_vec(sw_q_norm[0], sw_k_norm[0], scale64),
                                   rope=rope64, rope_half=32, pending=pending, name="sw_proj")
            attn = _sw_attention(r3(q), r3(k), r3(v), sw_sink[0] * LOG2E, b)
            w_o = sw_w_o[0]
        elif l == 2:
            p = dict(w_dqkv=mla_w_dqkv[0], q_a_norm=mla_q_a_norm[0], kv_a_norm=mla_kv_a_norm[0],
                     w_uq=mla_w_uq[0], w_ukv=mla_w_ukv[0], q_norm=mla_q_norm[0], k_norm=mla_k_norm[0])
            q, k, v, x2 = _mla_project(x2, mod3, norm_g[l, 0], p, pending)
            attn = _mla_attention(r3(q), r3(k), r3(v), b)
            w_o = mla_w_o[0]
        else:
            w = diff_w_qkv[0].astype(BF16)
            q, k, v, x2 = _project(x2, mod3, norm_g[l, 0], w, _qkv_plan(1024, 1024, 1024), (1024, 1024, 1024),
                                   bd64, _head64_vec(diff_q_norm[0], diff_k_norm[0], scale64),
                                   rope=rope64, rope_half=32, pending=pending, name="diff_proj")
            lambda_init = 0.8 - 0.6 * math.exp(-0.3 * l)
            attn = _diff_attention(r3(q), r3(k), r3(v), diff_lambda[0], diff_subln[0], lambda_init, b)
            w_o = diff_w_o[0]

        wr = jnp.concatenate([moe_w_router_group[l], moe_w_router_expert[l]], axis=1)
        wr = jnp.pad(wr, ((0, 0), (0, LANES - wr.shape[1])))
        wr_hi = wr.astype(BF16)
        wr_lo = (wr - wr_hi.astype(F32)).astype(BF16)
        br = jnp.concatenate([moe_b_router_group[l], moe_b_router_expert[l]])
        br = jnp.pad(br, (0, LANES - br.shape[0])).reshape(1, LANES)
        x2, h_bf, route, counts_f = _oproj_router(attn.reshape(t, -1), x2, mod3, w_o.astype(BF16), norm_g[l, 1],
                                                  jnp.concatenate([wr_hi, wr_lo], axis=1), br)
        y0, y1 = _moe_rows(h_bf, route, counts_f, moe_w_gate_up, moe_w_down, l)
        pending = (y0, y1, route, mod3)

    return _combine(x2, *pending).reshape(b, SL, d)
```
